```python
import math
import jax, jax.numpy as jnp
from jax import lax
import numpy as np

D_MODEL = 1024
BATCH = 16
SEQ = 2048
DEPTH = 1
DEC_BATCH = 8
DEC_SEQ = 4096
PAST_LEN = 128

RWKV_HEAD = 64
RWKV_WIDTH = D_MODEL
RWKV_HEADS = RWKV_WIDTH // RWKV_HEAD
DECAY_RANK = 64
ICLR_RANK = 64
GATE_RANK = 128
GN_EPS = 64e-5
DA_HEAD = 64
DA_HEADS = D_MODEL // (2 * DA_HEAD)
DA_WIDTH = DA_HEADS * 2 * DA_HEAD
ROPE_THETA = 500000.0
ROPE_DIM = DA_HEAD // 4
Q_BLOCK = 128
SUBLN_EPS = 1e-5
D_FF = 4 * D_MODEL
NORM_EPS = 1e-6
RWKV_COLS = 3 * RWKV_WIDTH + 2 * DECAY_RANK + 2 * ICLR_RANK + GATE_RANK
ATTN_COLS = 3 * DA_WIDTH
GATE_COLS = 2 * D_MODEL
IN_COLS = RWKV_COLS + ATTN_COLS + GATE_COLS
RWKV_SPLITS = (RWKV_WIDTH, 2 * RWKV_WIDTH, 3 * RWKV_WIDTH, 3 * RWKV_WIDTH + 2 * DECAY_RANK, 3 * RWKV_WIDTH + 2 * DECAY_RANK + 2 * ICLR_RANK)
ATTN_SPLITS = (DA_WIDTH, 2 * DA_WIDTH, 3 * DA_WIDTH, 3 * DA_WIDTH + D_MODEL)

kernel_name = 'hybrid_rwkv7_diffattn_encoder'


def rmsnorm(x, g, eps=NORM_EPS):
    xf = x.astype(jnp.float32)
    y = xf * lax.rsqrt(jnp.mean(xf * xf, axis=-1, keepdims=True) + eps)
    return (y * g.astype(jnp.float32)).astype(x.dtype)


def centred_shift(p, mu):
    pad = jnp.pad(p, ((0, 0), (1, 1), (0, 0)))
    nb = 0.5 * (pad[:, :-2] + pad[:, 2:])
    return p + (nb - p) * mu


def wkv7_scan(r, w, k, v, a, b, reverse):
    B, S, H, N = r.shape

    def step(st, inp):
        r_t, w_t, k_t, v_t, a_t, b_t = inp
        sa = jnp.einsum('bhvk,bhk->bhv', st, a_t)
        st = st * w_t[:, :, None, :] + sa[..., None] * b_t[:, :, None, :] + v_t[..., None] * k_t[:, :, None, :]
        y = jnp.einsum('bhvk,bhk->bhv', st, r_t)
        return st, y

    xs = tuple(jnp.moveaxis(t, 1, 0) for t in (r, w, k, v, a, b))
    s0 = jnp.zeros((B, H, N, N), jnp.float32)
    _, ys = lax.scan(step, s0, xs, reverse=reverse)
    return jnp.moveaxis(ys, 0, 1)


def rwkv7_mixer(r, k, v, lw, la, lg, w0, w_up, a0, a_up, g_up, k_k, k_a, r_k, ln_w, ln_b):
    B, S, C = r.shape
    H, N = RWKV_HEADS, RWKV_HEAD
    f32 = jnp.float32
    r, k, v = r.astype(f32), k.astype(f32), v.astype(f32)
    lw = lw.astype(f32).reshape(B, S, 2, DECAY_RANK)
    la = la.astype(f32).reshape(B, S, 2, ICLR_RANK)
    w_raw = w0.astype(f32) + jnp.einsum('bsdr,drc->bsdc', jnp.tanh(lw), w_up.astype(f32))
    decay = jnp.exp(-jnp.exp(-jax.nn.softplus(-w_raw) - 0.5))
    a = jax.nn.sigmoid(a0.astype(f32) + jnp.einsum('bsdr,drc->bsdc', la, a_up.astype(f32)))
    g = jax.nn.sigmoid(lg.astype(f32)) @ g_up.astype(f32)
    kk = (k * k_k.astype(f32)).reshape(B, S, H, N)
    kk = kk / jnp.maximum(jnp.sqrt(jnp.sum(kk * kk, axis=-1, keepdims=True)), 1e-12)
    k_dir = k[:, :, None, :] * (1.0 + (a - 1.0) * k_a.astype(f32))
    heads = lambda t: t.reshape(B, S, H, N)
    rh, vh = heads(r), heads(v)
    rk = r_k.astype(f32)
    y = jnp.zeros((B, S, H, N), f32)
    bonus = jnp.zeros((B, S, H, N), f32)
    for d, rev in ((0, False), (1, True)):
        kd = heads(k_dir[:, :, d])
        ad = heads(a[:, :, d])
        y = y + wkv7_scan(rh, heads(decay[:, :, d]), kd, vh, -kk, kk * ad, rev)
        bonus = bonus + jnp.sum(rh * kd * rk, axis=-1, keepdims=True) * vh
    mean = jnp.mean(y, axis=-1, keepdims=True)
    var = jnp.mean(jnp.square(y - mean), axis=-1, keepdims=True)
    gn = ((y - mean) * lax.rsqrt(var + GN_EPS)).reshape(B, S, C) * ln_w.astype(f32) + ln_b.astype(f32)
    return (gn + bonus.reshape(B, S, C)) * g


def partial_rope(x, cos, sin):
    half = ROPE_DIM // 2
    xr, xp = x[..., :ROPE_DIM], x[..., ROPE_DIM:]
    x1, x2 = xr[..., :half], xr[..., half:]
    c = cos[None, :, None, None, :]
    s = sin[None, :, None, None, :]
    return jnp.concatenate([x1 * c - x2 * s, x2 * c + x1 * s, xp], axis=-1)


def diff_attention(q, k, v, lq1, lk1, lq2, lk2, subln_w, lam_init):
    B, S, _ = q.shape
    H, d = DA_HEADS, DA_HEAD
    f32 = jnp.float32
    q = q.astype(f32).reshape(B, S, H, 2, d)
    k = k.astype(f32).reshape(B, S, H, 2, d)
    v = v.astype(f32).reshape(B, S, H, 2 * d).transpose(0, 2, 1, 3)
    pos = jnp.arange(S, dtype=f32)
    inv_freq = ROPE_THETA ** (-jnp.arange(0, ROPE_DIM, 2, dtype=f32) / ROPE_DIM)
    ang = pos[:, None] * inv_freq[None, :]
    cos, sin = jnp.cos(ang), jnp.sin(ang)
    q = partial_rope(q, cos, sin).transpose(0, 2, 3, 1, 4) * (d ** -0.5)
    k = partial_rope(k, cos, sin).transpose(0, 2, 3, 1, 4)
    lam = (jnp.exp(jnp.sum(lq1.astype(f32) * lk1.astype(f32))) - jnp.exp(jnp.sum(lq2.astype(f32) * lk2.astype(f32))) + lam_init)
    nblk = S // Q_BLOCK
    qb = q.reshape(B, H, 2, nblk, Q_BLOCK, d).transpose(3, 0, 1, 2, 4, 5)

    def attend(qblk):
        s = jnp.einsum('bhcqd,bhckd->bhcqk', qblk, k)
        p = jax.nn.softmax(s, axis=-1)
        att = p[:, :, 0] - lam * p[:, :, 1]
        return jnp.einsum('bhqk,bhkv->bhqv', att, v)

    o = lax.map(attend, qb)
    o = o.transpose(1, 0, 3, 2, 4).reshape(B, S, H, 2 * d)
    o = o * lax.rsqrt(jnp.mean(o * o, axis=-1, keepdims=True) + SUBLN_EPS) * subln_w.astype(f32)
    return (o * (1.0 - lam_init)).reshape(B, S, DA_WIDTH)


def encoder_layer(x, l, w_in, mu_shift, w0, w_lora_up, a0, a_lora_up, g_lora_up, k_k, k_a, r_k, ln_x_w, ln_x_b, lambda_q1, lambda_k1, lambda_q2, lambda_k2, subln_w, proj_a, proj_b, w_out, norm_mix, norm_mlp, w_mlp_in, w_mlp_out):
    dt = x.dtype
    xn = rmsnorm(x, norm_mix[l])
    proj = xn @ w_in[l]
    p_rwkv = centred_shift(proj[..., :RWKV_COLS], mu_shift[l])
    r, k, v, lw, la, lg = jnp.split(p_rwkv, RWKV_SPLITS, axis=-1)
    q_b, k_b, v_b, gate_a, gate_b = jnp.split(proj[..., RWKV_COLS:], ATTN_SPLITS, axis=-1)
    y_a = rwkv7_mixer(r, k, v, lw, la, lg, w0[l], w_lora_up[l], a0[l], a_lora_up[l], g_lora_up[l], k_k[l], k_a[l], r_k[l], ln_x_w[l], ln_x_b[l]).astype(dt)
    lam_init = 0.8 - 0.6 * math.exp(-0.3 * l)
    y_b = diff_attention(q_b, k_b, v_b, lambda_q1[l], lambda_k1[l], lambda_q2[l], lambda_k2[l], subln_w[l], lam_init).astype(dt)
    merged = jax.nn.sigmoid(gate_a) * (y_a @ proj_a[l]) + jax.nn.sigmoid(gate_b) * (y_b @ proj_b[l])
    h = x + merged @ w_out[l]
    hid = jnp.square(jax.nn.relu(rmsnorm(h, norm_mlp[l]) @ w_mlp_in[l]))
    return h + hid @ w_mlp_out[l]


def setup_inputs(seed: int = 0) -> dict:
    key = jax.random.key(seed)
    ks = jax.random.split(key, 32)
    f32 = jnp.float32
    nrm = lambda kk, shape, s: jax.random.normal(kk, shape, f32) * s
    L, C, D = DEPTH, RWKV_WIDTH, D_MODEL
    return {
        'x_prompt': nrm(ks[0], (BATCH, SEQ, D), 1.0),
        'x_sample': nrm(ks[1], (DEC_BATCH, DEC_SEQ, D), 1.0),
        'w_in': nrm(ks[2], (L, D, IN_COLS), D ** -0.5),
        'mu_shift': jax.random.uniform(ks[3], (L, RWKV_COLS), f32),
        'w0': jax.random.uniform(ks[4], (L, 2, C), f32, -6.0, 1.0),
        'w_lora_up': nrm(ks[5], (L, 2, DECAY_RANK, C), 0.3 * DECAY_RANK ** -0.5),
        'a0': nrm(ks[6], (L, 2, C), 0.1),
        'a_lora_up': nrm(ks[7], (L, 2, ICLR_RANK, C), 0.3 * ICLR_RANK ** -0.5),
        'g_lora_up': nrm(ks[8], (L, GATE_RANK, C), GATE_RANK ** -0.5),
        'k_k': 0.85 + nrm(ks[9], (L, C), 0.05),
        'k_a': 1.0 + nrm(ks[10], (L, C), 0.05),
        'r_k': nrm(ks[11], (L, RWKV_HEADS, RWKV_HEAD), 0.1),
        'ln_x_w': 1.0 + nrm(ks[12], (L, C), 0.02),
        'ln_x_b': nrm(ks[13], (L, C), 0.02),
        'lambda_q1': nrm(ks[14], (L, DA_HEAD), 0.1),
        'lambda_k1': nrm(ks[15], (L, DA_HEAD), 0.1),
        'lambda_q2': nrm(ks[16], (L, DA_HEAD), 0.1),
        'lambda_k2': nrm(ks[17], (L, DA_HEAD), 0.1),
        'subln_w': 1.0 + nrm(ks[18], (L, 2 * DA_HEAD), 0.02),
        'proj_a': nrm(ks[19], (L, C, D), C ** -0.5),
        'proj_b': nrm(ks[20], (L, DA_WIDTH, D), DA_WIDTH ** -0.5),
        'w_out': nrm(ks[21], (L, D, D), D ** -0.5),
        'norm_mix': 1.0 + nrm(ks[22], (L, D), 0.02),
        'norm_mlp': 1.0 + nrm(ks[23], (L, D), 0.02),
        'w_mlp_in': nrm(ks[24], (L, D, D_FF), D ** -0.5),
        'w_mlp_out': nrm(ks[25], (L, D_FF, D), D_FF ** -0.5),
        'norm_final': 1.0 + nrm(ks[26], (D,), 0.02),
    }


def reference(x_prompt, x_sample, w_in, mu_shift, w0, w_lora_up, a0, a_lora_up, g_lora_up, k_k, k_a, r_k, ln_x_w, ln_x_b, lambda_q1, lambda_k1, lambda_q2, lambda_k2, subln_w, proj_a, proj_b, w_out, norm_mix, norm_mlp, w_mlp_in, w_mlp_out, norm_final):
    h_p = x_prompt
    h_s = x_sample
    for l in range(DEPTH):
        h_p = encoder_layer(h_p, l, w_in, mu_shift, w0, w_lora_up, a0, a_lora_up, g_lora_up, k_k, k_a, r_k, ln_x_w, ln_x_b, lambda_q1, lambda_k1, lambda_q2, lambda_k2, subln_w, proj_a, proj_b, w_out, norm_mix, norm_mlp, w_mlp_in, w_mlp_out)
        h_s = encoder_layer(h_s, l, w_in, mu_shift, w0, w_lora_up, a0, a_lora_up, g_lora_up, k_k, k_a, r_k, ln_x_w, ln_x_b, lambda_q1, lambda_k1, lambda_q2, lambda_k2, subln_w, proj_a, proj_b, w_out, norm_mix, norm_mlp, w_mlp_in, w_mlp_out)
    y_prompt = rmsnorm(h_p, norm_final)
    y_sample = rmsnorm(h_s, norm_final)
    return (y_prompt, y_sample)
```

```python
import functools
import math

import jax
import jax.numpy as jnp
from jax import lax
from jax.experimental import pallas as pl
from jax.experimental.pallas import tpu as pltpu

F32 = jnp.float32
BF16 = jnp.bfloat16

D_MODEL = 1024
HEAD = 64
N_PAIR = D_MODEL // 128
DECAY_RANK = 64
ICLR_RANK = 64
GATE_RANK = 128
GN_EPS = 64e-5
ROPE_THETA = 500000.0
ROPE_DIM = 16
SUBLN_EPS = 1e-5
NORM_EPS = 1e-6
D_FF = 4 * D_MODEL
RWKV_MAIN = 3 * D_MODEL
RWKV_COLS = RWKV_MAIN + 2 * DECAY_RANK + 2 * ICLR_RANK + GATE_RANK
GROUP = 1024
N_GROUPS = 9
G_Q, G_KB, G_VB, G_GA, G_GB, G_LORA = 3, 4, 5, 6, 7, 8
LORA_BLK = G_LORA * GROUP // 128
CHUNK = 64
EXP_M05 = math.exp(-0.5)
VMEM_LIMIT = 56 * 1024 * 1024

NN = (((1,), (0,)), ((), ()))
NT = (((1,), (1,)), ((), ()))
TN = (((0,), (0,)), ((), ()))


def _dot(a, b, dims=NN):
    return lax.dot_general(a, b, dims, preferred_element_type=F32)


def _split(x):
    hi = x.astype(BF16)
    lo = (x - hi.astype(F32)).astype(BF16)
    return hi, lo


def _dot3(a, b, dims=NN):
    ah, al = _split(a)
    bh, bl = _split(b)
    return _dot(ah, bh, dims) + (_dot(ah, bl, dims) + _dot(al, bh, dims))


def _dot2_exact_rhs(a, b_bf16, dims=NN):
    ah, al = _split(a)
    return _dot(ah, b_bf16, dims) + _dot(al, b_bf16, dims)


def _sigmoid(x):
    return 1.0 / (1.0 + jnp.exp(-x))


def _rms(x, g):
    return x * lax.rsqrt(jnp.mean(x * x, axis=-1, keepdims=True) + NORM_EPS) * g


def _inproj_kernel(x_ref, xp_ref, xn_ref, g_ref, w_ref, mu_ref, cos_ref, sa_ref, sb_ref,
                   o_ref, xs_ref, hs_ref, *, tm, tiles_per_seq):
    i = pl.program_id(0)
    j = pl.program_id(1)

    @pl.when(j == 0)
    def _():
        g = g_ref[...]
        xs_ref[...] = _rms(x_ref[...], g).astype(BF16)
        hs_ref[0:8, :] = _rms(xp_ref[...], g)
        hs_ref[8:16, :] = _rms(xn_ref[...], g)

    w = w_ref[...]
    p = _dot(xs_ref[...], w)

    is_shift = jnp.logical_or(j < 3, j == G_LORA)
    is_rope = jnp.logical_or(j == G_Q, j == G_KB)

    @pl.when(is_shift)
    def _():
        ph = _dot(hs_ref[...].astype(BF16), w)
        t_in_seq = i % tiles_per_seq
        prev = jnp.where(t_in_seq == 0, 0.0, ph[7:8, :])
        nxt = jnp.where(t_in_seq == tiles_per_seq - 1, 0.0, ph[8:9, :])
        rows = lax.broadcasted_iota(jnp.int32, (tm, 1), 0)
        up = jnp.where(rows == 0, prev, pltpu.roll(p, 1, 0))
        dn = jnp.where(rows == tm - 1, nxt, pltpu.roll(p, tm - 1, 0))
        o_ref[...] = p + (0.5 * (up + dn) - p) * mu_ref[...]

    @pl.when(is_rope)
    def _():
        scale = jnp.where(j == G_Q, HEAD ** -0.5, 1.0)
        c = jnp.tile(cos_ref[...], (1, N_PAIR))
        sa = jnp.tile(sa_ref[...], (1, N_PAIR))
        sb = jnp.tile(sb_ref[...], (1, N_PAIR))
        half = ROPE_DIM // 2
        o_ref[...] = (p * c + pltpu.roll(p, half, 1) * sa + pltpu.roll(p, GROUP - half, 1) * sb) * scale

    @pl.when(jnp.logical_not(jnp.logical_or(is_shift, is_rope)))
    def _():
        o_ref[...] = p


def _inproj(x2d, seq, g, w_re, mu_re, cos_t, sa_t, sb_t):
    t_total = x2d.shape[0]
    tm = min(512, seq)
    tiles_per_seq = seq // tm
    n_tiles = t_total // tm
    last8 = t_total // 8 - 1
    kern = functools.partial(_inproj_kernel, tm=tm, tiles_per_seq=tiles_per_seq)
    return pl.pallas_call(
        kern,
        grid=(n_tiles, N_GROUPS),
        in_specs=[
            pl.BlockSpec((tm, D_MODEL), lambda i, j: (i, 0)),
            pl.BlockSpec((8, D_MODEL), lambda i, j: (jnp.maximum(i * (tm // 8) - 1, 0), 0)),
            pl.BlockSpec((8, D_MODEL), lambda i, j: (jnp.minimum((i + 1) * (tm // 8), last8), 0)),
            pl.BlockSpec((1, D_MODEL), lambda i, j: (0, 0)),
            pl.BlockSpec((D_MODEL, GROUP), lambda i, j: (0, j)),
            pl.BlockSpec((1, GROUP), lambda i, j: (0, j)),
            pl.BlockSpec((tm, 128), lambda i, j: (i % tiles_per_seq, 0)),
            pl.BlockSpec((tm, 128), lambda i, j: (i % tiles_per_seq, 0)),
            pl.BlockSpec((tm, 128), lambda i, j: (i % tiles_per_seq, 0)),
        ],
        out_specs=pl.BlockSpec((tm, GROUP), lambda i, j: (i, j)),
        out_shape=jax.ShapeDtypeStruct((t_total, N_GROUPS * GROUP), F32),
        scratch_shapes=[pltpu.VMEM((tm, D_MODEL), BF16), pltpu.VMEM((16, D_MODEL), F32)],
        compiler_params=pltpu.CompilerParams(
            dimension_semantics=("arbitrary", "arbitrary"), vmem_limit_bytes=VMEM_LIMIT),
        name="inproj",
    )(x2d, x2d, x2d, g, w_re, mu_re, cos_t, sa_t, sb_t)


def _scan_kernel(r_ref, k_ref, v_ref, lw_ref, la_ref, wuh_ref, wul_ref, auh_ref, aul_ref,
                 w0_ref, a0_ref, kk_ref, ka_ref, rk_ref, seg_ref, y_ref, s_ref, h_ref):
    d = pl.program_id(1)
    c = pl.program_id(2)
    C = CHUNK

    @pl.when(c == 0)
    def _():
        h_ref[...] = jnp.zeros_like(h_ref)

    sgn = 1 - 2 * d
    r = r_ref[...]
    k = k_ref[...]
    v = v_ref[...]

    th = jnp.tanh(lw_ref[...])
    thh, thl = _split(th)
    w_raw = w0_ref[0] + (_dot(thh, wuh_ref[0]) + (_dot(thh, wul_ref[0]) + _dot(thl, wuh_ref[0])))
    logw = -EXP_M05 * _sigmoid(w_raw)
    lah, lal = _split(la_ref[...])
    a = _sigmoid(a0_ref[0] + (_dot(lah, auh_ref[0]) + (_dot(lah, aul_ref[0]) + _dot(lal, auh_ref[0]))))

    rr = lax.broadcasted_iota(jnp.int32, (C, C), 0)
    cc = lax.broadcasted_iota(jnp.int32, (C, C), 1)
    tri = jnp.where((rr - cc) * sgn >= 0, 1.0, 0.0).astype(BF16)
    l1 = logw.astype(BF16)
    e1 = logw - l1.astype(F32)
    l2 = e1.astype(BF16)
    l3 = (e1 - l2.astype(F32)).astype(BF16)
    cum = _dot(tri, l1) + (_dot(tri, l2) + _dot(tri, l3))
    total = jnp.sum(logw, axis=0, keepdims=True)
    g_incl = jnp.exp(cum)
    g_excl = jnp.exp(cum - logw)
    g_inv = jnp.exp(-cum)
    g_tail = jnp.exp(total - cum)
    g_tot = jnp.exp(total)

    kkv = k * kk_ref[...]
    kd = k * (1.0 + (a - 1.0) * ka_ref[...])
    sq = kkv * kkv

    s_ref[0] = _dot2_exact_rhs(r * kd * rk_ref[...], seg_ref[...])

    lane = lax.broadcasted_iota(jnp.int32, (1, 128), 1)
    first = lane < HEAD
    tr = lax.broadcasted_iota(jnp.int32, (128, 128), 0) % C
    tc = lax.broadcasted_iota(jnp.int32, (128, 128), 1) % C
    dd = (tr - tc) * sgn
    strict = dd > 0
    incl = dd >= 0
    same_head = (lax.broadcasted_iota(jnp.int32, (128, 128), 0) // C) == (
        lax.broadcasted_iota(jnp.int32, (128, 128), 1) // C)
    bd_ones = jnp.where(same_head, 1.0, 0.0).astype(BF16)
    eye = jnp.where(jnp.logical_and(dd == 0, same_head), 1.0, 0.0)

    def stack(x):
        return jnp.concatenate([jnp.where(first, x, 0.0), jnp.where(first, 0.0, x)], axis=0)

    for p in range(N_PAIR):
        sl = slice(128 * p, 128 * (p + 1))
        ss = _dot2_exact_rhs(sq[:, sl], bd_ones)
        kkn = kkv[:, sl] * lax.rsqrt(jnp.maximum(ss, 1e-24))
        a_p = a[:, sl]
        bvec = kkn * a_p
        at_s = stack(-kkn * g_excl[:, sl])
        rt_s = stack(r[:, sl] * g_incl[:, sl])
        bt_s = stack(bvec * g_inv[:, sl])
        kt_s = stack(kd[:, sl] * g_inv[:, sl])
        bh_s = stack(bvec * g_tail[:, sl])
        kh_s = stack(kd[:, sl] * g_tail[:, sl])
        v_s = stack(v[:, sl])
        ht = h_ref[p]

        lhs = jnp.concatenate([at_s, rt_s], axis=0)
        rhs = jnp.concatenate([bt_s, kt_s], axis=0)
        gm = _dot3(lhs, rhs, NT)
        a_ab = jnp.where(strict, gm[0:128, 0:128], 0.0)
        a_ak = jnp.where(strict, gm[0:128, 128:256], 0.0)
        a_rb = jnp.where(incl, gm[128:256, 0:128], 0.0)
        a_rk = jnp.where(incl, gm[128:256, 128:256], 0.0)

        tinv = eye + jnp.where((tr // 2) == (tc // 2), a_ab, 0.0)
        m = 2
        while m < C:
            off = jnp.where(jnp.logical_and((tr // (2 * m)) == (tc // (2 * m)), (tr // m) != (tc // m)),
                            a_ab, 0.0)
            tinv = tinv + _dot3(tinv, _dot3(off, tinv))
            m *= 2

        ar = _dot3(lhs, ht, NT)
        u = _dot3(tinv, ar[0:128] + _dot3(a_ak, v_s))
        y_s = ar[128:256] + _dot3(a_rb, u) + _dot3(a_rk, v_s)
        y_ref[0, :, sl] = y_s[0:C] + y_s[C:2 * C]
        h_ref[p] = ht * g_tot[:, sl] + _dot3(u, bh_s, TN) + _dot3(v_s, kh_s, TN)


def _wkv_scan(proj, batch, seq, wuh, wul, auh, aul, w0, a0, k_k, k_a, r_k, seg):
    t_total = batch * seq
    nc = seq // CHUNK

    def rowblk(b, d, c):
        return b * nc + c + d * (nc - 1 - 2 * c)

    def col(jblk):
        return lambda b, d, c: (rowblk(b, d, c), jblk)

    full = lambda b, d, c: (0, 0)
    perdir = lambda b, d, c: (d, 0, 0)
    return pl.pallas_call(
        _scan_kernel,
        grid=(batch, 2, nc),
        in_specs=[
            pl.BlockSpec((CHUNK, GROUP), col(0)),
            pl.BlockSpec((CHUNK, GROUP), col(1)),
            pl.BlockSpec((CHUNK, GROUP), col(2)),
            pl.BlockSpec((CHUNK, 128), col(LORA_BLK)),
            pl.BlockSpec((CHUNK, 128), col(LORA_BLK + 1)),
            pl.BlockSpec((1, 128, D_MODEL), perdir),
            pl.BlockSpec((1, 128, D_MODEL), perdir),
            pl.BlockSpec((1, 128, D_MODEL), perdir),
            pl.BlockSpec((1, 128, D_MODEL), perdir),
            pl.BlockSpec((1, 1, D_MODEL), perdir),
            pl.BlockSpec((1, 1, D_MODEL), perdir),
            pl.BlockSpec((1, D_MODEL), full),
            pl.BlockSpec((1, D_MODEL), full),
            pl.BlockSpec((1, D_MODEL), full),
            pl.BlockSpec((D_MODEL, 128), full),
        ],
        out_specs=[
            pl.BlockSpec((1, CHUNK, D_MODEL), lambda b, d, c: (d, rowblk(b, d, c), 0)),
            pl.BlockSpec((1, CHUNK, 128), lambda b, d, c: (d, rowblk(b, d, c), 0)),
        ],
        out_shape=[
            jax.ShapeDtypeStruct((2, t_total, D_MODEL), F32),
            jax.ShapeDtypeStruct((2, t_total, 128), F32),
        ],
        scratch_shapes=[pltpu.VMEM((N_PAIR, 128, 128), F32)],
        compiler_params=pltpu.CompilerParams(
            dimension_semantics=("arbitrary", "arbitrary", "arbitrary"), vmem_limit_bytes=VMEM_LIMIT),
        name="wkv_scan",
    )(proj, proj, proj, proj, proj, wuh, wul, auh, aul, w0, a0, k_k, k_a, r_k, seg)


def _attn_kernel(q_ref, k_ref, v_ref, lq1_ref, lk1_ref, lq2_ref, lk2_ref, sw_ref, o_ref,
                 k0_ref, k1_ref, vt_ref, *, seq, tq, tk, lam_init):
    qi = pl.program_id(2)
    n_kc = seq // tk
    lane = lax.broadcasted_iota(jnp.int32, (1, 128), 1)
    first = lane < HEAD

    @pl.when(qi == 0)
    def _():
        def prep(kc, carry):
            rows = pl.ds(pl.multiple_of(kc * tk, tk), tk)
            kf = k_ref[rows, :]
            k0_ref[rows, :] = jnp.where(first, kf, 0.0).astype(BF16)
            k1_ref[rows, :] = jnp.where(first, 0.0, kf).astype(BF16)
            vt_ref[:, rows] = v_ref[rows, :].T.astype(BF16)
            return carry
        lax.fori_loop(0, n_kc, prep, 0)

    q = q_ref[...].astype(BF16)

    def body(kc, carry):
        m0, l0, acc0, m1, l1, acc1 = carry
        rows = pl.ds(pl.multiple_of(kc * tk, tk), tk)
        vt = vt_ref[:, rows]

        def one(kref, m, l, acc):
            s = _dot(kref[rows, :], q, NT)
            mn = jnp.maximum(m, jnp.max(s, axis=0, keepdims=True))
            pr = jnp.exp(s - mn)
            alpha = jnp.exp(m - mn)
            l = alpha * l + jnp.sum(pr, axis=0, keepdims=True)
            acc = alpha * acc + _dot(vt, pr.astype(BF16))
            return mn, l, acc

        m0, l0, acc0 = one(k0_ref, m0, l0, acc0)
        m1, l1, acc1 = one(k1_ref, m1, l1, acc1)
        return m0, l0, acc0, m1, l1, acc1

    neg = jnp.full((1, tq), -jnp.inf, F32)
    zl = jnp.zeros((1, tq), F32)
    za = jnp.zeros((128, tq), F32)
    m0, l0, acc0, m1, l1, acc1 = lax.fori_loop(0, n_kc, body, (neg, zl, za, neg, zl, za))

    lam = (jnp.exp(jnp.sum(lq1_ref[...] * lk1_ref[...], axis=-1, keepdims=True))
           - jnp.exp(jnp.sum(lq2_ref[...] * lk2_ref[...], axis=-1, keepdims=True)) + lam_init)
    o_t = acc0 / l0 - lam * (acc1 / l1)
    o = o_t.T
    o = o * lax.rsqrt(jnp.mean(o * o, axis=-1, keepdims=True) + SUBLN_EPS) * sw_ref[...]
    o_ref[...] = o * (1.0 - lam_init)


def _diff_attn(proj, batch, seq, lq1, lk1, lq2, lk2, subln_w, lam_init):
    t_total = batch * seq
    tq = min(256, seq)
    tk = min(512, seq)
    nq = seq // tq
    n_heads = N_PAIR
    qb, kb, vb = (G_Q * GROUP // 128, G_KB * GROUP // 128, G_VB * GROUP // 128)
    kern = functools.partial(_attn_kernel, seq=seq, tq=tq, tk=tk, lam_init=lam_init)
    small = lambda b, h, i: (0, 0)
    return pl.pallas_call(
        kern,
        grid=(batch, n_heads, nq),
        in_specs=[
            pl.BlockSpec((tq, 128), lambda b, h, i: (b * nq + i, qb + h)),
            pl.BlockSpec((seq, 128), lambda b, h, i: (b, kb + h)),
            pl.BlockSpec((seq, 128), lambda b, h, i: (b, vb + h)),
            pl.BlockSpec((1, HEAD), small),
            pl.BlockSpec((1, HEAD), small),
            pl.BlockSpec((1, HEAD), small),
            pl.BlockSpec((1, HEAD), small),
            pl.BlockSpec((1, 128), small),
        ],
        out_specs=pl.BlockSpec((tq, 128), lambda b, h, i: (b * nq + i, h)),
        out_shape=jax.ShapeDtypeStruct((t_total, D_MODEL), F32),
        scratch_shapes=[pltpu.VMEM((seq, 128), BF16), pltpu.VMEM((seq, 128), BF16),
                        pltpu.VMEM((128, seq), BF16)],
        compiler_params=pltpu.CompilerParams(
            dimension_semantics=("arbitrary", "arbitrary", "arbitrary"), vmem_limit_bytes=VMEM_LIMIT),
        name="diff_attn",
    )(proj, proj, proj, lq1, lk1, lq2, lk2, subln_w)


def _mix_kernel(x_ref, yf_ref, yb_ref, sf_ref, sb_ref, v_ref, lg_ref, ga_ref, gb_ref, o_ref,
                lnw_ref, lnb_ref, gup_ref, exp_ref, pa_ref, pb_ref, wo_ref, h_ref, ya_ref):
    y = yf_ref[0] + yb_ref[0]
    r128 = lax.broadcasted_iota(jnp.int32, (128, 128), 0) // HEAD
    c128 = lax.broadcasted_iota(jnp.int32, (128, 128), 1) // HEAD
    avg = jnp.where(r128 == c128, 1.0 / HEAD, 0.0).astype(BF16)
    for p in range(N_PAIR):
        sl = slice(128 * p, 128 * (p + 1))
        yp = y[:, sl]
        mean = _dot2_exact_rhs(yp, avg)
        cen = yp - mean
        var = _dot2_exact_rhs(cen * cen, avg)
        ya_ref[:, sl] = cen * lax.rsqrt(var + GN_EPS)
    coef = _dot2_exact_rhs(sf_ref[0] + sb_ref[0], exp_ref[...])
    gate = _dot(_sigmoid(lg_ref[...]).astype(BF16), gup_ref[...])
    y_a = (ya_ref[...] * lnw_ref[...] + lnb_ref[...] + coef * v_ref[...]) * gate
    pa = _dot(y_a.astype(BF16), pa_ref[...])
    pb = _dot(o_ref[...].astype(BF16), pb_ref[...])
    merged = _sigmoid(ga_ref[...]) * pa + _sigmoid(gb_ref[...]) * pb
    h_ref[...] = x_ref[...] + _dot(merged.astype(BF16), wo_ref[...])


def _mix(x2d, proj, y_dir, s_dir, o_attn, ln_w, ln_b, g_up, expand, proj_a, proj_b, w_out):
    t_total = x2d.shape[0]
    tm = 256
    row = lambda i: (i, 0)
    full = lambda i: (0, 0)
    wspec = pl.BlockSpec((D_MODEL, D_MODEL), full)
    return pl.pallas_call(
        _mix_kernel,
        grid=(t_total // tm,),
        in_specs=[
            pl.BlockSpec((tm, D_MODEL), row),
            pl.BlockSpec((1, tm, D_MODEL), lambda i: (0, i, 0)),
            pl.BlockSpec((1, tm, D_MODEL), lambda i: (1, i, 0)),
            pl.BlockSpec((1, tm, 128), lambda i: (0, i, 0)),
            pl.BlockSpec((1, tm, 128), lambda i: (1, i, 0)),
            pl.BlockSpec((tm, GROUP), lambda i: (i, 2)),
            pl.BlockSpec((tm, 128), lambda i: (i, LORA_BLK + 2)),
            pl.BlockSpec((tm, GROUP), lambda i: (i, G_GA)),
            pl.BlockSpec((tm, GROUP), lambda i: (i, G_GB)),
            pl.BlockSpec((tm, D_MODEL), row),
            pl.BlockSpec((1, D_MODEL), full),
            pl.BlockSpec((1, D_MODEL), full),
            pl.BlockSpec((GATE_RANK, D_MODEL), full),
            pl.BlockSpec((128, D_MODEL), full),
            wspec, wspec, wspec,
        ],
        out_specs=pl.BlockSpec((tm, D_MODEL), row),
        out_shape=jax.ShapeDtypeStruct((t_total, D_MODEL), F32),
        scratch_shapes=[pltpu.VMEM((tm, D_MODEL), F32)],
        compiler_params=pltpu.CompilerParams(
            dimension_semantics=("arbitrary",), vmem_limit_bytes=VMEM_LIMIT),
        name="mix",
    )(x2d, y_dir, y_dir, s_dir, s_dir, proj, proj, proj, proj, o_attn,
      ln_w, ln_b, g_up, expand, proj_a, proj_b, w_out)


def _mlp_kernel(h_ref, gm_ref, gf_ref, w1_ref, w2_ref, o_ref, *, ff_chunk):
    h = h_ref[...]
    xn = _rms(h, gm_ref[...]).astype(BF16)
    acc = h
    for c in range(D_FF // ff_chunk):
        sl = slice(c * ff_chunk, (c + 1) * ff_chunk)
        hid = jnp.maximum(_dot(xn, w1_ref[:, sl]), 0.0)
        acc = acc + _dot((hid * hid).astype(BF16), w2_ref[sl, :])
    o_ref[...] = _rms(acc, gf_ref[...])


def _mlp(h2d, norm_mlp, norm_final, w1, w2):
    t_total = h2d.shape[0]
    tm = 256
    row = lambda i: (i, 0)
    full = lambda i: (0, 0)
    kern = functools.partial(_mlp_kernel, ff_chunk=1024)
    return pl.pallas_call(
        kern,
        grid=(t_total // tm,),
        in_specs=[
            pl.BlockSpec((tm, D_MODEL), row),
            pl.BlockSpec((1, D_MODEL), full),
            pl.BlockSpec((1, D_MODEL), full),
            pl.BlockSpec((D_MODEL, D_FF), full),
            pl.BlockSpec((D_FF, D_MODEL), full),
        ],
        out_specs=pl.BlockSpec((tm, D_MODEL), row),
        out_shape=jax.ShapeDtypeStruct((t_total, D_MODEL), F32),
        compiler_params=pltpu.CompilerParams(
            dimension_semantics=("arbitrary",), vmem_limit_bytes=VMEM_LIMIT),
        name="mlp",
    )(h2d, norm_mlp, norm_final, w1, w2)


def _rope_tables(seq):
    half = ROPE_DIM // 2
    pos = jnp.arange(seq, dtype=F32)
    inv_freq = ROPE_THETA ** (-jnp.arange(0, ROPE_DIM, 2, dtype=F32) / ROPE_DIM)
    ang = pos[:, None] * inv_freq[None, :]
    cos, sin = jnp.cos(ang), jnp.sin(ang)
    pad = jnp.zeros((seq, HEAD - ROPE_DIM), F32)
    z8 = jnp.zeros((seq, half), F32)
    c64 = jnp.concatenate([cos, cos, pad + 1.0], axis=1)
    sa64 = jnp.concatenate([z8, sin, pad], axis=1)
    sb64 = jnp.concatenate([-sin, z8, pad], axis=1)
    two = lambda t: jnp.concatenate([t, t], axis=1)
    return two(c64), two(sa64), two(sb64)


def _layer(x, l, prm, norm_final):
    batch, seq, _ = x.shape
    x2d = x.reshape(batch * seq, D_MODEL)
    cos_t, sa_t, sb_t = _rope_tables(seq)
    proj = _inproj(x2d, seq, prm["norm_mix"], prm["w_re"], prm["mu_re"], cos_t, sa_t, sb_t)
    y_dir, s_dir = _wkv_scan(proj, batch, seq, prm["wuh"], prm["wul"], prm["auh"], prm["aul"],
                             prm["w0"], prm["a0"], prm["k_k"], prm["k_a"], prm["r_k"], prm["seg"])
    lam_init = 0.8 - 0.6 * math.exp(-0.3 * l)
    o_attn = _diff_attn(proj, batch, seq, prm["lq1"], prm["lk1"], prm["lq2"], prm["lk2"],
                        prm["subln_w"], lam_init)
    h = _mix(x2d, proj, y_dir, s_dir, o_attn, prm["ln_w"], prm["ln_b"], prm["g_up"], prm["expand"],
             prm["proj_a"], prm["proj_b"], prm["w_out"])
    return h


def _prep_layer(l, w_in, mu_shift, w0, w_lora_up, a0, a_lora_up, g_lora_up, k_k, k_a, r_k, ln_x_w,
                ln_x_b, lambda_q1, lambda_k1, lambda_q2, lambda_k2, subln_w, proj_a, proj_b, w_out,
                norm_mix, norm_mlp, w_mlp_in, w_mlp_out):
    w = w_in[l]
    pad_cols = N_GROUPS * GROUP - w.shape[1]
    w_re = jnp.concatenate([w[:, :RWKV_MAIN], w[:, RWKV_COLS:], w[:, RWKV_MAIN:RWKV_COLS],
                            jnp.zeros((D_MODEL, pad_cols), F32)], axis=1).astype(BF16)
    mu = mu_shift[l]
    mu_re = jnp.concatenate([mu[:RWKV_MAIN], jnp.zeros((5 * GROUP,), F32), mu[RWKV_MAIN:],
                             jnp.zeros((pad_cols,), F32)])[None, :]

    def lora_pad(up):
        z = jnp.zeros_like(up[0])
        return jnp.stack([jnp.concatenate([up[0], z], axis=0), jnp.concatenate([z, up[1]], axis=0)])

    wu = lora_pad(w_lora_up[l])
    au = lora_pad(a_lora_up[l])
    wuh = wu.astype(BF16)
    auh = au.astype(BF16)
    head_of_lane = jnp.arange(D_MODEL) // HEAD
    seg = (head_of_lane[:, None] == jnp.arange(128)[None, :]).astype(BF16)
    return dict(
        w_re=w_re, mu_re=mu_re, norm_mix=norm_mix[l][None, :],
        wuh=wuh, wul=(wu - wuh.astype(F32)).astype(BF16),
        auh=auh, aul=(au - auh.astype(F32)).astype(BF16),
        w0=w0[l][:, None, :], a0=a0[l][:, None, :],
        k_k=k_k[l][None, :], k_a=k_a[l][None, :], r_k=r_k[l].reshape(1, D_MODEL),
        seg=seg, expand=seg.T,
        lq1=lambda_q1[l][None, :], lk1=lambda_k1[l][None, :],
        lq2=lambda_q2[l][None, :], lk2=lambda_k2[l][None, :],
        subln_w=subln_w[l][None, :],
        ln_w=ln_x_w[l][None, :], ln_b=ln_x_b[l][None, :],
        g_up=g_lora_up[l].astype(BF16),
        proj_a=proj_a[l].astype(BF16), proj_b=proj_b[l].astype(BF16), w_out=w_out[l].astype(BF16),
        norm_mlp=norm_mlp[l][None, :],
        w1=w_mlp_in[l].astype(BF16), w2=w_mlp_out[l].astype(BF16),
    )


def kernel(x_prompt, x_sample, w_in, mu_shift, w0, w_lora_up, a0, a_lora_up, g_lora_up, k_k, k_a, r_k, ln_x_w, ln_x_b, lambda_q1, lambda_k1, lambda_q2, lambda_k2, subln_w, proj_a, proj_b, w_out, norm_mix, norm_mlp, w_mlp_in, w_mlp_out, norm_final):
    depth = w_in.shape[0]
    assert depth == 1, "the final norm is fused into the (single) layer's MLP kernel"
    prm = _prep_layer(0, w_in, mu_shift, w0, w_lora_up, a0, a_lora_up, g_lora_up, k_k, k_a, r_k, ln_x_w,
                      ln_x_b, lambda_q1, lambda_k1, lambda_q2, lambda_k2, subln_w, proj_a, proj_b,
                      w_out, norm_mix, norm_mlp, w_mlp_in, w_mlp_out)
    outs = []
    for x in (x_prompt, x_sample):
        h = _layer(x, 0, prm, norm_final)
        y = _mlp(h, prm["norm_mlp"], norm_final[None, :], prm["w1"], prm["w2"])
        outs.append(y.reshape(x.shape))
    return tuple(outs)
```

```python
import functools
import math

import jax
import jax.numpy as jnp
from jax import lax
from jax.experimental import pallas as pl
from jax.experimental.pallas import tpu as pltpu

F32 = jnp.float32
BF16 = jnp.bfloat16

D_MODEL = 1024
HEAD = 64
N_PAIR = D_MODEL // 128
DECAY_RANK = 64
ICLR_RANK = 64
GATE_RANK = 128
GN_EPS = 64e-5
ROPE_THETA = 500000.0
ROPE_DIM = 16
SUBLN_EPS = 1e-5
NORM_EPS = 1e-6
D_FF = 4 * D_MODEL
RWKV_MAIN = 3 * D_MODEL
RWKV_COLS = RWKV_MAIN + 2 * DECAY_RANK + 2 * ICLR_RANK + GATE_RANK
GROUP = 1024
N_GROUPS = 9
G_Q, G_KB, G_VB, G_GA, G_GB, G_LORA = 3, 4, 5, 6, 7, 8
LORA_BLK = G_LORA * GROUP // 128
CHUNK = 64
EXP_M05 = math.exp(-0.5)
LOG2_E = math.log2(math.e)
VMEM_LIMIT = 56 * 1024 * 1024

NN = (((1,), (0,)), ((), ()))
NT = (((1,), (1,)), ((), ()))
TN = (((0,), (0,)), ((), ()))


def _dot(a, b, dims=NN):
    return lax.dot_general(a, b, dims, preferred_element_type=F32)


def _split(x):
    hi = x.astype(BF16)
    lo = (x - hi.astype(F32)).astype(BF16)
    return hi, lo


def _dot3(a, b, dims=NN):
    ah, al = _split(a)
    bh, bl = _split(b)
    return _dot(ah, bh, dims) + (_dot(ah, bl, dims) + _dot(al, bh, dims))


def _dot2_exact_rhs(a, b_bf16, dims=NN):
    ah, al = _split(a)
    return _dot(ah, b_bf16, dims) + _dot(al, b_bf16, dims)


def _sigmoid(x):
    return 1.0 / (1.0 + jnp.exp(-x))


def _rms(x, g):
    return x * lax.rsqrt(jnp.mean(x * x, axis=-1, keepdims=True) + NORM_EPS) * g


def _inproj_kernel(x_ref, xp_ref, xn_ref, g_ref, w_ref, mu_ref, cos_ref, sa_ref, sb_ref,
                   o_ref, xs_ref, hs_ref, *, tm, tiles_per_seq):
    i = pl.program_id(0)
    j = pl.program_id(1)

    @pl.when(j == 0)
    def _():
        g = g_ref[...]
        xs_ref[...] = _rms(x_ref[...], g).astype(BF16)
        hs_ref[0:8, :] = _rms(xp_ref[...], g)
        hs_ref[8:16, :] = _rms(xn_ref[...], g)

    w = w_ref[...]
    p = _dot(xs_ref[...], w)

    is_shift = jnp.logical_or(j < 3, j == G_LORA)
    is_rope = jnp.logical_or(j == G_Q, j == G_KB)

    @pl.when(is_shift)
    def _():
        ph = _dot(hs_ref[...].astype(BF16), w)
        t_in_seq = i % tiles_per_seq
        prev = jnp.where(t_in_seq == 0, 0.0, ph[7:8, :])
        nxt = jnp.where(t_in_seq == tiles_per_seq - 1, 0.0, ph[8:9, :])
        rows = lax.broadcasted_iota(jnp.int32, (tm, 1), 0)
        up = jnp.where(rows == 0, prev, pltpu.roll(p, 1, 0))
        dn = jnp.where(rows == tm - 1, nxt, pltpu.roll(p, tm - 1, 0))
        o_ref[...] = p + (0.5 * (up + dn) - p) * mu_ref[...]

    @pl.when(is_rope)
    def _():
        scale = jnp.where(j == G_Q, HEAD ** -0.5, 1.0)
        c = jnp.tile(cos_ref[...], (1, N_PAIR))
        sa = jnp.tile(sa_ref[...], (1, N_PAIR))
        sb = jnp.tile(sb_ref[...], (1, N_PAIR))
        half = ROPE_DIM // 2
        o_ref[...] = (p * c + pltpu.roll(p, half, 1) * sa + pltpu.roll(p, GROUP - half, 1) * sb) * scale

    @pl.when(jnp.logical_not(jnp.logical_or(is_shift, is_rope)))
    def _():
        o_ref[...] = p


def _inproj(x2d, seq, g, w_re, mu_re, cos_t, sa_t, sb_t):
    t_total = x2d.shape[0]
    tm = min(512, seq)
    tiles_per_seq = seq // tm
    n_tiles = t_total // tm
    last8 = t_total // 8 - 1
    kern = functools.partial(_inproj_kernel, tm=tm, tiles_per_seq=tiles_per_seq)
    return pl.pallas_call(
        kern,
        grid=(n_tiles, N_GROUPS),
        in_specs=[
            pl.BlockSpec((tm, D_MODEL), lambda i, j: (i, 0)),
            pl.BlockSpec((8, D_MODEL), lambda i, j: (jnp.maximum(i * (tm // 8) - 1, 0), 0)),
            pl.BlockSpec((8, D_MODEL), lambda i, j: (jnp.minimum((i + 1) * (tm // 8), last8), 0)),
            pl.BlockSpec((1, D_MODEL), lambda i, j: (0, 0)),
            pl.BlockSpec((D_MODEL, GROUP), lambda i, j: (0, j)),
            pl.BlockSpec((1, GROUP), lambda i, j: (0, j)),
            pl.BlockSpec((tm, 128), lambda i, j: (i % tiles_per_seq, 0)),
            pl.BlockSpec((tm, 128), lambda i, j: (i % tiles_per_seq, 0)),
            pl.BlockSpec((tm, 128), lambda i, j: (i % tiles_per_seq, 0)),
        ],
        out_specs=pl.BlockSpec((tm, GROUP), lambda i, j: (i, j)),
        out_shape=jax.ShapeDtypeStruct((t_total, N_GROUPS * GROUP), F32),
        scratch_shapes=[pltpu.VMEM((tm, D_MODEL), BF16), pltpu.VMEM((16, D_MODEL), F32)],
        compiler_params=pltpu.CompilerParams(
            dimension_semantics=("arbitrary", "arbitrary"), vmem_limit_bytes=VMEM_LIMIT),
        name="inproj",
    )(x2d, x2d, x2d, g, w_re, mu_re, cos_t, sa_t, sb_t)


def _scan_kernel(r_ref, k_ref, v_ref, lw_ref, la_ref, wuh_ref, wul_ref, auh_ref, aul_ref,
                 w0_ref, a0_ref, kk_ref, ka_ref, rk_ref, seg_ref, y_ref, s_ref, h_ref):
    d = pl.program_id(1)
    c = pl.program_id(2)
    C = CHUNK

    @pl.when(c == 0)
    def _():
        h_ref[...] = jnp.zeros_like(h_ref)

    sgn = 1 - 2 * d
    r = r_ref[...]
    k = k_ref[...]
    v = v_ref[...]

    th = jnp.tanh(lw_ref[...])
    thh, thl = _split(th)
    w_raw = w0_ref[0] + (_dot(thh, wuh_ref[0]) + (_dot(thh, wul_ref[0]) + _dot(thl, wuh_ref[0])))
    logw = -EXP_M05 * _sigmoid(w_raw)
    lah, lal = _split(la_ref[...])
    a = _sigmoid(a0_ref[0] + (_dot(lah, auh_ref[0]) + (_dot(lah, aul_ref[0]) + _dot(lal, auh_ref[0]))))

    rr = lax.broadcasted_iota(jnp.int32, (C, C), 0)
    cc = lax.broadcasted_iota(jnp.int32, (C, C), 1)
    tri = jnp.where((rr - cc) * sgn >= 0, 1.0, 0.0).astype(BF16)
    l1 = logw.astype(BF16)
    e1 = logw - l1.astype(F32)
    l2 = e1.astype(BF16)
    l3 = (e1 - l2.astype(F32)).astype(BF16)
    cum = _dot(tri, l1) + (_dot(tri, l2) + _dot(tri, l3))
    total = jnp.sum(logw, axis=0, keepdims=True)
    g_incl = jnp.exp(cum)
    g_excl = jnp.exp(cum - logw)
    g_inv = jnp.exp(-cum)
    g_tail = jnp.exp(total - cum)
    g_tot = jnp.exp(total)

    kkv = k * kk_ref[...]
    kd = k * (1.0 + (a - 1.0) * ka_ref[...])
    sq = kkv * kkv

    s_ref[0] = _dot2_exact_rhs(r * kd * rk_ref[...], seg_ref[...])

    lane = lax.broadcasted_iota(jnp.int32, (1, 128), 1)
    first = lane < HEAD
    tr = lax.broadcasted_iota(jnp.int32, (128, 128), 0) % C
    tc = lax.broadcasted_iota(jnp.int32, (128, 128), 1) % C
    dd = (tr - tc) * sgn
    strict = dd > 0
    incl = dd >= 0
    same_head = (lax.broadcasted_iota(jnp.int32, (128, 128), 0) // C) == (
        lax.broadcasted_iota(jnp.int32, (128, 128), 1) // C)
    bd_ones = jnp.where(same_head, 1.0, 0.0).astype(BF16)
    eye = jnp.where(jnp.logical_and(dd == 0, same_head), 1.0, 0.0)

    def stack(x):
        return jnp.concatenate([jnp.where(first, x, 0.0), jnp.where(first, 0.0, x)], axis=0)

    pairs = range(N_PAIR)
    sls = [slice(128 * p, 128 * (p + 1)) for p in pairs]
    bf = lambda x: x.astype(BF16)

    ss = [_dot2_exact_rhs(sq[:, sl], bd_ones) for sl in sls]
    kkn = [kkv[:, sl] * lax.rsqrt(jnp.maximum(s, 1e-24)) for sl, s in zip(sls, ss)]
    bvec = [kn * a[:, sl] for sl, kn in zip(sls, kkn)]
    lhs = [bf(jnp.concatenate([stack(-kn * g_excl[:, sl]), stack(r[:, sl] * g_incl[:, sl])], axis=0))
           for sl, kn in zip(sls, kkn)]
    rhs = [bf(jnp.concatenate([stack(bv * g_inv[:, sl]), stack(kd[:, sl] * g_inv[:, sl])], axis=0))
           for sl, bv in zip(sls, bvec)]
    bkh = [bf(jnp.concatenate([stack(bv * g_tail[:, sl]), stack(kd[:, sl] * g_tail[:, sl])], axis=0))
           for sl, bv in zip(sls, bvec)]
    v_s = [bf(stack(v[:, sl])) for sl in sls]
    ht = [h_ref[p] for p in pairs]

    gm = [_dot(lhs[p], rhs[p], NT) for p in pairs]
    a_ab = [jnp.where(strict, g[0:128, 0:128], 0.0) for g in gm]
    a_ak = [bf(jnp.where(strict, g[0:128, 128:256], 0.0)) for g in gm]
    a_r = [bf(jnp.concatenate([jnp.where(incl, g[128:256, 0:128], 0.0),
                               jnp.where(incl, g[128:256, 128:256], 0.0)], axis=1)) for g in gm]
    ar = [_dot(lhs[p], bf(ht[p]), NT) for p in pairs]

    lvl = (tr // 2) == (tc // 2)
    tinv = [eye + jnp.where(lvl, x, 0.0) for x in a_ab]
    m = 2
    while m < C:
        lvl = jnp.logical_and((tr // (2 * m)) == (tc // (2 * m)), (tr // m) != (tc // m))
        tb = [bf(t) for t in tinv]
        wm = [_dot(bf(jnp.where(lvl, a_ab[p], 0.0)), tb[p]) for p in pairs]
        tinv = [tinv[p] + _dot(tb[p], bf(wm[p])) for p in pairs]
        m *= 2

    rhs_u = [ar[p][0:128] + _dot(a_ak[p], v_s[p]) for p in pairs]
    u = [bf(_dot(bf(tinv[p]), bf(rhs_u[p]))) for p in pairs]
    uv = [jnp.concatenate([u[p], v_s[p]], axis=0) for p in pairs]
    y_s = [ar[p][128:256] + _dot(a_r[p], uv[p]) for p in pairs]
    for p in pairs:
        y_ref[0, :, sls[p]] = y_s[p][0:C] + y_s[p][C:2 * C]
        h_ref[p] = ht[p] * g_tot[:, sls[p]] + _dot(uv[p], bkh[p], TN)


def _wkv_scan(proj, batch, seq, wuh, wul, auh, aul, w0, a0, k_k, k_a, r_k, seg):
    t_total = batch * seq
    nc = seq // CHUNK

    def rowblk(b, d, c):
        return b * nc + c + d * (nc - 1 - 2 * c)

    def col(jblk):
        return lambda b, d, c: (rowblk(b, d, c), jblk)

    full = lambda b, d, c: (0, 0)
    perdir = lambda b, d, c: (d, 0, 0)
    return pl.pallas_call(
        _scan_kernel,
        grid=(batch, 2, nc),
        in_specs=[
            pl.BlockSpec((CHUNK, GROUP), col(0)),
            pl.BlockSpec((CHUNK, GROUP), col(1)),
            pl.BlockSpec((CHUNK, GROUP), col(2)),
            pl.BlockSpec((CHUNK, 128), col(LORA_BLK)),
            pl.BlockSpec((CHUNK, 128), col(LORA_BLK + 1)),
            pl.BlockSpec((1, 128, D_MODEL), perdir),
            pl.BlockSpec((1, 128, D_MODEL), perdir),
            pl.BlockSpec((1, 128, D_MODEL), perdir),
            pl.BlockSpec((1, 128, D_MODEL), perdir),
            pl.BlockSpec((1, 1, D_MODEL), perdir),
            pl.BlockSpec((1, 1, D_MODEL), perdir),
            pl.BlockSpec((1, D_MODEL), full),
            pl.BlockSpec((1, D_MODEL), full),
            pl.BlockSpec((1, D_MODEL), full),
            pl.BlockSpec((D_MODEL, 128), full),
        ],
        out_specs=[
            pl.BlockSpec((1, CHUNK, D_MODEL), lambda b, d, c: (d, rowblk(b, d, c), 0)),
            pl.BlockSpec((1, CHUNK, 128), lambda b, d, c: (d, rowblk(b, d, c), 0)),
        ],
        out_shape=[
            jax.ShapeDtypeStruct((2, t_total, D_MODEL), F32),
            jax.ShapeDtypeStruct((2, t_total, 128), F32),
        ],
        scratch_shapes=[pltpu.VMEM((N_PAIR, 128, 128), F32)],
        compiler_params=pltpu.CompilerParams(
            dimension_semantics=("arbitrary", "arbitrary", "arbitrary"), vmem_limit_bytes=VMEM_LIMIT),
        name="wkv_scan",
    )(proj, proj, proj, proj, proj, wuh, wul, auh, aul, w0, a0, k_k, k_a, r_k, seg)


def _attn_kernel(q_ref, k_ref, v_ref, lq1_ref, lk1_ref, lq2_ref, lk2_ref, sw_ref, o_ref,
                 k0_ref, k1_ref, vt_ref, s_ref, *, seq, tq, tk, unroll, lam_init):
    qi = pl.program_id(2)
    n_kc = seq // tk
    lane = lax.broadcasted_iota(jnp.int32, (1, 128), 1)
    first = lane < HEAD

    @pl.when(qi == 0)
    def _():
        def prep(kc, carry):
            rows = pl.ds(pl.multiple_of(kc * tk, tk), tk)
            kf = k_ref[rows, :]
            k0_ref[rows, :] = jnp.where(first, kf, 0.0).astype(BF16)
            k1_ref[rows, :] = jnp.where(first, 0.0, kf).astype(BF16)
            vt_ref[:, rows] = v_ref[rows, :].T.astype(BF16)
            return carry
        lax.fori_loop(0, n_kc, prep, 0)

    q = (q_ref[...] * LOG2_E).astype(BF16)

    def rows_of(kc):
        return pl.ds(pl.multiple_of(kc * tk, tk), tk)

    def fold(op, s):
        return op(s.reshape(tk // 8, 8, tq), axis=0)

    def score_body(kc, carry):
        mx0, mx1 = carry
        rows = rows_of(kc)
        s0 = _dot(k0_ref[rows, :], q, NT)
        s1 = _dot(k1_ref[rows, :], q, NT)
        s_ref[0, rows, :] = s0
        s_ref[1, rows, :] = s1
        return jnp.maximum(mx0, fold(jnp.max, s0)), jnp.maximum(mx1, fold(jnp.max, s1))

    neg = jnp.full((8, tq), -jnp.inf, F32)
    mx0, mx1 = lax.fori_loop(0, n_kc, score_body, (neg, neg), unroll=unroll)
    m0 = jnp.max(mx0, axis=0, keepdims=True)
    m1 = jnp.max(mx1, axis=0, keepdims=True)

    def pv_body(kc, carry):
        l0, acc0, l1, acc1 = carry
        rows = rows_of(kc)
        vt = vt_ref[:, rows]
        p0 = jnp.exp2(s_ref[0, rows, :] - m0)
        p1 = jnp.exp2(s_ref[1, rows, :] - m1)
        l0 = l0 + fold(jnp.sum, p0)
        l1 = l1 + fold(jnp.sum, p1)
        acc0 = acc0 + _dot(vt, p0.astype(BF16))
        acc1 = acc1 + _dot(vt, p1.astype(BF16))
        return l0, acc0, l1, acc1

    zl = jnp.zeros((8, tq), F32)
    za = jnp.zeros((128, tq), F32)
    l0, acc0, l1, acc1 = lax.fori_loop(0, n_kc, pv_body, (zl, za, zl, za), unroll=unroll)
    l0 = jnp.sum(l0, axis=0, keepdims=True)
    l1 = jnp.sum(l1, axis=0, keepdims=True)

    lam = (jnp.exp(jnp.sum(lq1_ref[...] * lk1_ref[...], axis=-1, keepdims=True))
           - jnp.exp(jnp.sum(lq2_ref[...] * lk2_ref[...], axis=-1, keepdims=True)) + lam_init)
    o_t = acc0 / l0 - lam * (acc1 / l1)
    o = o_t.T
    o = o * lax.rsqrt(jnp.mean(o * o, axis=-1, keepdims=True) + SUBLN_EPS) * sw_ref[...]
    o_ref[...] = o * (1.0 - lam_init)


def _diff_attn(proj, batch, seq, lq1, lk1, lq2, lk2, subln_w, lam_init):
    t_total = batch * seq
    tq = min(256, seq)
    tk = min(256, seq)
    unroll = math.gcd(seq // tk, 4)
    nq = seq // tq
    n_heads = N_PAIR
    qb, kb, vb = (G_Q * GROUP // 128, G_KB * GROUP // 128, G_VB * GROUP // 128)
    kern = functools.partial(_attn_kernel, seq=seq, tq=tq, tk=tk, unroll=unroll, lam_init=lam_init)
    small = lambda b, h, i: (0, 0)
    return pl.pallas_call(
        kern,
        grid=(batch, n_heads, nq),
        in_specs=[
            pl.BlockSpec((tq, 128), lambda b, h, i: (b * nq + i, qb + h)),
            pl.BlockSpec((seq, 128), lambda b, h, i: (b, kb + h)),
            pl.BlockSpec((seq, 128), lambda b, h, i: (b, vb + h)),
            pl.BlockSpec((1, HEAD), small),
            pl.BlockSpec((1, HEAD), small),
            pl.BlockSpec((1, HEAD), small),
            pl.BlockSpec((1, HEAD), small),
            pl.BlockSpec((1, 128), small),
        ],
        out_specs=pl.BlockSpec((tq, 128), lambda b, h, i: (b * nq + i, h)),
        out_shape=jax.ShapeDtypeStruct((t_total, D_MODEL), F32),
        scratch_shapes=[pltpu.VMEM((seq, 128), BF16), pltpu.VMEM((seq, 128), BF16),
                        pltpu.VMEM((128, seq), BF16), pltpu.VMEM((2, seq, tq), F32)],
        compiler_params=pltpu.CompilerParams(
            dimension_semantics=("arbitrary", "arbitrary", "arbitrary"), vmem_limit_bytes=VMEM_LIMIT),
        name="diff_attn",
    )(proj, proj, proj, lq1, lk1, lq2, lk2, subln_w)


def _mix_kernel(x_ref, yf_ref, yb_ref, sf_ref, sb_ref, v_ref, lg_ref, ga_ref, gb_ref, o_ref,
                lnw_ref, lnb_ref, gup_ref, exp_ref, pa_ref, pb_ref, wo_ref, h_ref, ya_ref):
    y = yf_ref[0] + yb_ref[0]
    r128 = lax.broadcasted_iota(jnp.int32, (128, 128), 0) // HEAD
    c128 = lax.broadcasted_iota(jnp.int32, (128, 128), 1) // HEAD
    avg = jnp.where(r128 == c128, 1.0 / HEAD, 0.0).astype(BF16)
    for p in range(N_PAIR):
        sl = slice(128 * p, 128 * (p + 1))
        yp = y[:, sl]
        mean = _dot2_exact_rhs(yp, avg)
        cen = yp - mean
        var = _dot2_exact_rhs(cen * cen, avg)
        ya_ref[:, sl] = cen * lax.rsqrt(var + GN_EPS)
    coef = _dot2_exact_rhs(sf_ref[0] + sb_ref[0], exp_ref[...])
    gate = _dot(_sigmoid(lg_ref[...]).astype(BF16), gup_ref[...])
    y_a = (ya_ref[...] * lnw_ref[...] + lnb_ref[...] + coef * v_ref[...]) * gate
    pa = _dot(y_a.astype(BF16), pa_ref[...])
    pb = _dot(o_ref[...].astype(BF16), pb_ref[...])
    merged = _sigmoid(ga_ref[...]) * pa + _sigmoid(gb_ref[...]) * pb
    h_ref[...] = x_ref[...] + _dot(merged.astype(BF16), wo_ref[...])


def _mix(x2d, proj, y_dir, s_dir, o_attn, ln_w, ln_b, g_up, expand, proj_a, proj_b, w_out):
    t_total = x2d.shape[0]
    tm = 256
    row = lambda i: (i, 0)
    full = lambda i: (0, 0)
    wspec = pl.BlockSpec((D_MODEL, D_MODEL), full)
    return pl.pallas_call(
        _mix_kernel,
        grid=(t_total // tm,),
        in_specs=[
            pl.BlockSpec((tm, D_MODEL), row),
            pl.BlockSpec((1, tm, D_MODEL), lambda i: (0, i, 0)),
            pl.BlockSpec((1, tm, D_MODEL), lambda i: (1, i, 0)),
            pl.BlockSpec((1, tm, 128), lambda i: (0, i, 0)),
            pl.BlockSpec((1, tm, 128), lambda i: (1, i, 0)),
            pl.BlockSpec((tm, GROUP), lambda i: (i, 2)),
            pl.BlockSpec((tm, 128), lambda i: (i, LORA_BLK + 2)),
            pl.BlockSpec((tm, GROUP), lambda i: (i, G_GA)),
            pl.BlockSpec((tm, GROUP), lambda i: (i, G_GB)),
            pl.BlockSpec((tm, D_MODEL), row),
            pl.BlockSpec((1, D_MODEL), full),
            pl.BlockSpec((1, D_MODEL), full),
            pl.BlockSpec((GATE_RANK, D_MODEL), full),
            pl.BlockSpec((128, D_MODEL), full),
            wspec, wspec, wspec,
        ],
        out_specs=pl.BlockSpec((tm, D_MODEL), row),
        out_shape=jax.ShapeDtypeStruct((t_total, D_MODEL), F32),
        scratch_shapes=[pltpu.VMEM((tm, D_MODEL), F32)],
        compiler_params=pltpu.CompilerParams(
            dimension_semantics=("arbitrary",), vmem_limit_bytes=VMEM_LIMIT),
        name="mix",
    )(x2d, y_dir, y_dir, s_dir, s_dir, proj, proj, proj, proj, o_attn,
      ln_w, ln_b, g_up, expand, proj_a, proj_b, w_out)


def _mlp_kernel(h_ref, gm_ref, gf_ref, w1_ref, w2_ref, o_ref, *, ff_chunk):
    h = h_ref[...]
    xn = _rms(h, gm_ref[...]).astype(BF16)
    acc = h
    for c in range(D_FF // ff_chunk):
        sl = slice(c * ff_chunk, (c + 1) * ff_chunk)
        hid = jnp.maximum(_dot(xn, w1_ref[:, sl]), 0.0)
        acc = acc + _dot((hid * hid).astype(BF16), w2_ref[sl, :])
    o_ref[...] = _rms(acc, gf_ref[...])


def _mlp(h2d, norm_mlp, norm_final, w1, w2):
    t_total = h2d.shape[0]
    tm = 256
    row = lambda i: (i, 0)
    full = lambda i: (0, 0)
    kern = functools.partial(_mlp_kernel, ff_chunk=1024)
    return pl.pallas_call(
        kern,
        grid=(t_total // tm,),
        in_specs=[
            pl.BlockSpec((tm, D_MODEL), row),
            pl.BlockSpec((1, D_MODEL), full),
            pl.BlockSpec((1, D_MODEL), full),
            pl.BlockSpec((D_MODEL, D_FF), full),
            pl.BlockSpec((D_FF, D_MODEL), full),
        ],
        out_specs=pl.BlockSpec((tm, D_MODEL), row),
        out_shape=jax.ShapeDtypeStruct((t_total, D_MODEL), F32),
        compiler_params=pltpu.CompilerParams(
            dimension_semantics=("arbitrary",), vmem_limit_bytes=VMEM_LIMIT),
        name="mlp",
    )(h2d, norm_mlp, norm_final, w1, w2)


def _rope_tables(seq):
    half = ROPE_DIM // 2
    pos = jnp.arange(seq, dtype=F32)
    inv_freq = ROPE_THETA ** (-jnp.arange(0, ROPE_DIM, 2, dtype=F32) / ROPE_DIM)
    ang = pos[:, None] * inv_freq[None, :]
    cos, sin = jnp.cos(ang), jnp.sin(ang)
    pad = jnp.zeros((seq, HEAD - ROPE_DIM), F32)
    z8 = jnp.zeros((seq, half), F32)
    c64 = jnp.concatenate([cos, cos, pad + 1.0], axis=1)
    sa64 = jnp.concatenate([z8, sin, pad], axis=1)
    sb64 = jnp.concatenate([-sin, z8, pad], axis=1)
    two = lambda t: jnp.concatenate([t, t], axis=1)
    return two(c64), two(sa64), two(sb64)


def _layer(x, l, prm, norm_final):
    batch, seq, _ = x.shape
    x2d = x.reshape(batch * seq, D_MODEL)
    cos_t, sa_t, sb_t = _rope_tables(seq)
    proj = _inproj(x2d, seq, prm["norm_mix"], prm["w_re"], prm["mu_re"], cos_t, sa_t, sb_t)
    y_dir, s_dir = _wkv_scan(proj, batch, seq, prm["wuh"], prm["wul"], prm["auh"], prm["aul"],
                             prm["w0"], prm["a0"], prm["k_k"], prm["k_a"], prm["r_k"], prm["seg"])
    lam_init = 0.8 - 0.6 * math.exp(-0.3 * l)
    o_attn = _diff_attn(proj, batch, seq, prm["lq1"], prm["lk1"], prm["lq2"], prm["lk2"],
                        prm["subln_w"], lam_init)
    h = _mix(x2d, proj, y_dir, s_dir, o_attn, prm["ln_w"], prm["ln_b"], prm["g_up"], prm["expand"],
             prm["proj_a"], prm["proj_b"], prm["w_out"])
    return h


def _prep_layer(l, w_in, mu_shift, w0, w_lora_up, a0, a_lora_up, g_lora_up, k_k, k_a, r_k, ln_x_w,
                ln_x_b, lambda_q1, lambda_k1, lambda_q2, lambda_k2, subln_w, proj_a, proj_b, w_out,
                norm_mix, norm_mlp, w_mlp_in, w_mlp_out):
    w = w_in[l]
    pad_cols = N_GROUPS * GROUP - w.shape[1]
    w_re = jnp.concatenate([w[:, :RWKV_MAIN], w[:, RWKV_COLS:], w[:, RWKV_MAIN:RWKV_COLS],
                            jnp.zeros((D_MODEL, pad_cols), F32)], axis=1).astype(BF16)
    mu = mu_shift[l]
    mu_re = jnp.concatenate([mu[:RWKV_MAIN], jnp.zeros((5 * GROUP,), F32), mu[RWKV_MAIN:],
                             jnp.zeros((pad_cols,), F32)])[None, :]

    def lora_pad(up):
        z = jnp.zeros_like(up[0])
        return jnp.stack([jnp.concatenate([up[0], z], axis=0), jnp.concatenate([z, up[1]], axis=0)])

    wu = lora_pad(w_lora_up[l])
    au = lora_pad(a_lora_up[l])
    wuh = wu.astype(BF16)
    auh = au.astype(BF16)
    head_of_lane = jnp.arange(D_MODEL) // HEAD
    seg = (head_of_lane[:, None] == jnp.arange(128)[None, :]).astype(BF16)
    return dict(
        w_re=w_re, mu_re=mu_re, norm_mix=norm_mix[l][None, :],
        wuh=wuh, wul=(wu - wuh.astype(F32)).astype(BF16),
        auh=auh, aul=(au - auh.astype(F32)).astype(BF16),
        w0=w0[l][:, None, :], a0=a0[l][:, None, :],
        k_k=k_k[l][None, :], k_a=k_a[l][None, :], r_k=r_k[l].reshape(1, D_MODEL),
        seg=seg, expand=seg.T,
        lq1=lambda_q1[l][None, :], lk1=lambda_k1[l][None, :],
        lq2=lambda_q2[l][None, :], lk2=lambda_k2[l][None, :],
        subln_w=subln_w[l][None, :],
        ln_w=ln_x_w[l][None, :], ln_b=ln_x_b[l][None, :],
        g_up=g_lora_up[l].astype(BF16),
        proj_a=proj_a[l].astype(BF16), proj_b=proj_b[l].astype(BF16), w_out=w_out[l].astype(BF16),
        norm_mlp=norm_mlp[l][None, :],
        w1=w_mlp_in[l].astype(BF16), w2=w_mlp_out[l].astype(BF16),
    )


def kernel(x_prompt, x_sample, w_in, mu_shift, w0, w_lora_up, a0, a_lora_up, g_lora_up, k_k, k_a, r_k, ln_x_w, ln_x_b, lambda_q1, lambda_k1, lambda_q2, lambda_k2, subln_w, proj_a, proj_b, w_out, norm_mix, norm_mlp, w_mlp_in, w_mlp_out, norm_final):
    depth = w_in.shape[0]
    assert depth == 1, "the final norm is fused into the (single) layer's MLP kernel"
    prm = _prep_layer(0, w_in, mu_shift, w0, w_lora_up, a0, a_lora_up, g_lora_up, k_k, k_a, r_k, ln_x_w,
                      ln_x_b, lambda_q1, lambda_k1, lambda_q2, lambda_k2, subln_w, proj_a, proj_b,
                      w_out, norm_mix, norm_mlp, w_mlp_in, w_mlp_out)
    outs = []
    for x in (x_prompt, x_sample):
        h = _layer(x, 0, prm, norm_final)
        y = _mlp(h, prm["norm_mlp"], norm_final[None, :], prm["w1"], prm["w2"])
        outs.append(y.reshape(x.shape))
    return tuple(outs)
```

```python
import functools
import math

import jax
import jax.numpy as jnp
from jax import lax
from jax.experimental import pallas as pl
from jax.experimental.pallas import tpu as pltpu

F32 = jnp.float32
BF16 = jnp.bfloat16

D_MODEL = 1024
HEAD = 64
N_PAIR = D_MODEL // 128
DECAY_RANK = 64
ICLR_RANK = 64
GATE_RANK = 128
GN_EPS = 64e-5
ROPE_THETA = 500000.0
ROPE_DIM = 16
SUBLN_EPS = 1e-5
NORM_EPS = 1e-6
D_FF = 4 * D_MODEL
RWKV_MAIN = 3 * D_MODEL
RWKV_COLS = RWKV_MAIN + 2 * DECAY_RANK + 2 * ICLR_RANK + GATE_RANK
GROUP = 1024
N_GROUPS = 9
G_Q, G_KB, G_VB, G_GA, G_GB, G_LORA = 3, 4, 5, 6, 7, 8
GROUPS_PER_STEP = 3
LORA_BLK = G_LORA * GROUP // 128
CHUNK = 64
EXP_M05 = math.exp(-0.5)
LOG2_E = math.log2(math.e)
VMEM_LIMIT = 56 * 1024 * 1024

NN = (((1,), (0,)), ((), ()))
NT = (((1,), (1,)), ((), ()))
TN = (((0,), (0,)), ((), ()))


def _dot(a, b, dims=NN):
    return lax.dot_general(a, b, dims, preferred_element_type=F32)


def _split(x):
    hi = x.astype(BF16)
    lo = (x - hi.astype(F32)).astype(BF16)
    return hi, lo


def _dot3(a, b, dims=NN):
    ah, al = _split(a)
    bh, bl = _split(b)
    return _dot(ah, bh, dims) + (_dot(ah, bl, dims) + _dot(al, bh, dims))


def _dot2_exact_rhs(a, b_bf16, dims=NN):
    ah, al = _split(a)
    return _dot(ah, b_bf16, dims) + _dot(al, b_bf16, dims)


def _sigmoid(x):
    return 1.0 / (1.0 + jnp.exp(-x))


def _rms(x, g):
    return x * lax.rsqrt(jnp.mean(x * x, axis=-1, keepdims=True) + NORM_EPS) * g


def _inproj_kernel(x_ref, xp_ref, xn_ref, g_ref, w_ref, mu_ref, cos_ref, sa_ref, sb_ref,
                   o_ref, xs_ref, hs_ref, *, tm, tiles_per_seq):
    i = pl.program_id(0)
    j = pl.program_id(1)

    @pl.when(j == 0)
    def _():
        g = g_ref[...]
        xs_ref[...] = _rms(x_ref[...], g).astype(BF16)
        hs_ref[0:8, :] = _rms(xp_ref[...], g)
        hs_ref[8:16, :] = _rms(xn_ref[...], g)

    def shifted(p, w, cols):
        ph = _dot(hs_ref[...].astype(BF16), w)
        t_in_seq = i % tiles_per_seq
        prev = jnp.where(t_in_seq == 0, 0.0, ph[7:8, :])
        nxt = jnp.where(t_in_seq == tiles_per_seq - 1, 0.0, ph[8:9, :])
        rows = lax.broadcasted_iota(jnp.int32, (tm, 1), 0)
        up = jnp.where(rows == 0, prev, pltpu.roll(p, 1, 0))
        dn = jnp.where(rows == tm - 1, nxt, pltpu.roll(p, tm - 1, 0))
        return p + (0.5 * (up + dn) - p) * mu_ref[:, cols]

    def rotated(p, scale):
        c = jnp.tile(cos_ref[...], (1, N_PAIR))
        sa = jnp.tile(sa_ref[...], (1, N_PAIR))
        sb = jnp.tile(sb_ref[...], (1, N_PAIR))
        half = ROPE_DIM // 2
        out = p * c + pltpu.roll(p, half, 1) * sa + pltpu.roll(p, GROUP - half, 1) * sb
        return out if scale == 1.0 else out * scale

    for step in range(N_GROUPS // GROUPS_PER_STEP):
        @pl.when(j == step)
        def _(step=step):
            for gi in range(GROUPS_PER_STEP):
                g = step * GROUPS_PER_STEP + gi
                cols = slice(gi * GROUP, (gi + 1) * GROUP)
                w = w_ref[:, cols]
                p = _dot(xs_ref[...], w)
                if g < 3 or g == G_LORA:
                    p = shifted(p, w, cols)
                elif g == G_Q:
                    p = rotated(p, HEAD ** -0.5)
                elif g == G_KB:
                    p = rotated(p, 1.0)
                o_ref[:, cols] = p


def _inproj(x2d, seq, g, w_re, mu_re, cos_t, sa_t, sb_t):
    t_total = x2d.shape[0]
    tm = min(512, seq)
    tiles_per_seq = seq // tm
    n_tiles = t_total // tm
    last8 = t_total // 8 - 1
    kern = functools.partial(_inproj_kernel, tm=tm, tiles_per_seq=tiles_per_seq)
    return pl.pallas_call(
        kern,
        grid=(n_tiles, N_GROUPS // GROUPS_PER_STEP),
        in_specs=[
            pl.BlockSpec((tm, D_MODEL), lambda i, j: (i, 0)),
            pl.BlockSpec((8, D_MODEL), lambda i, j: (jnp.maximum(i * (tm // 8) - 1, 0), 0)),
            pl.BlockSpec((8, D_MODEL), lambda i, j: (jnp.minimum((i + 1) * (tm // 8), last8), 0)),
            pl.BlockSpec((1, D_MODEL), lambda i, j: (0, 0)),
            pl.BlockSpec((D_MODEL, GROUPS_PER_STEP * GROUP), lambda i, j: (0, j)),
            pl.BlockSpec((1, GROUPS_PER_STEP * GROUP), lambda i, j: (0, j)),
            pl.BlockSpec((tm, 128), lambda i, j: (i % tiles_per_seq, 0)),
            pl.BlockSpec((tm, 128), lambda i, j: (i % tiles_per_seq, 0)),
            pl.BlockSpec((tm, 128), lambda i, j: (i % tiles_per_seq, 0)),
        ],
        out_specs=pl.BlockSpec((tm, GROUPS_PER_STEP * GROUP), lambda i, j: (i, j)),
        out_shape=jax.ShapeDtypeStruct((t_total, N_GROUPS * GROUP), F32),
        scratch_shapes=[pltpu.VMEM((tm, D_MODEL), BF16), pltpu.VMEM((16, D_MODEL), F32)],
        compiler_params=pltpu.CompilerParams(
            dimension_semantics=("arbitrary", "arbitrary"), vmem_limit_bytes=VMEM_LIMIT),
        name="inproj",
    )(x2d, x2d, x2d, g, w_re, mu_re, cos_t, sa_t, sb_t)


QUAD = 256
N_QUAD = D_MODEL // QUAD
HEADS_PER_QUAD = QUAD // HEAD
SCAN_SUB = 4


def _scan_kernel(r_ref, k_ref, v_ref, lw_ref, la_ref, wuh_ref, wul_ref, auh_ref, aul_ref,
                 w0_ref, a0_ref, kk_ref, ka_ref, rk_ref, seg_ref, y_ref, s_ref, h_ref, *, n_sub):
    d = pl.program_id(1)
    c = pl.program_id(2)
    C = CHUNK
    bf = lambda x: x.astype(BF16)

    @pl.when(c == 0)
    def _():
        h_ref[...] = jnp.zeros_like(h_ref)

    sgn = 1 - 2 * d
    rr = lax.broadcasted_iota(jnp.int32, (C, C), 0)
    cc = lax.broadcasted_iota(jnp.int32, (C, C), 1)
    tri = jnp.where((rr - cc) * sgn >= 0, 1.0, 0.0).astype(BF16)

    tr = lax.broadcasted_iota(jnp.int32, (C, QUAD), 0)
    tc = lax.broadcasted_iota(jnp.int32, (C, QUAD), 1) % C
    dd = (tr - tc) * sgn
    strict = dd > 0
    incl = dd >= 0
    eye = jnp.where(dd == 0, 1.0, 0.0)
    levels = []
    m = 1
    while m < C:
        levels.append(jnp.logical_and((tr // (2 * m)) == (tc // (2 * m)), (tr // m) != (tc // m)))
        m *= 2
    lane_head = lax.broadcasted_iota(jnp.int32, (1, QUAD), 1) // HEAD
    head_lanes = [lane_head == h for h in range(HEADS_PER_QUAD)]
    same_head = (lax.broadcasted_iota(jnp.int32, (QUAD, QUAD), 0) // HEAD) == (
        lax.broadcasted_iota(jnp.int32, (QUAD, QUAD), 1) // HEAD)
    bd_ones = jnp.where(same_head, 1.0, 0.0).astype(BF16)

    def blockdiag(x):
        return jnp.concatenate([jnp.where(hl, x, 0.0) for hl in head_lanes], axis=0)

    quads = range(N_QUAD)
    sls = [slice(QUAD * q, QUAD * (q + 1)) for q in quads]

    staged = []
    for i in range(n_sub):
        jj = i + d * (n_sub - 1 - 2 * i)
        rows = pl.ds(pl.multiple_of(jj * C, C), C)
        r = r_ref[rows, :]
        k = k_ref[rows, :]
        v = v_ref[rows, :]

        thh, thl = _split(jnp.tanh(lw_ref[rows, :]))
        w_raw = w0_ref[0] + (_dot(thh, wuh_ref[0]) + (_dot(thh, wul_ref[0]) + _dot(thl, wuh_ref[0])))
        logw = -EXP_M05 * _sigmoid(w_raw)
        lah, lal = _split(la_ref[rows, :])
        a = _sigmoid(a0_ref[0] + (_dot(lah, auh_ref[0]) + (_dot(lah, aul_ref[0]) + _dot(lal, auh_ref[0]))))

        l1 = logw.astype(BF16)
        e1 = logw - l1.astype(F32)
        l2 = e1.astype(BF16)
        l3 = (e1 - l2.astype(F32)).astype(BF16)
        cum = _dot(tri, l1) + (_dot(tri, l2) + _dot(tri, l3))
        total = jnp.sum(logw, axis=0, keepdims=True)
        g_incl = jnp.exp(cum)
        g_excl = jnp.exp(cum - logw)
        g_inv = jnp.exp(-cum)
        g_tail = jnp.exp(total - cum)
        g_tot = jnp.exp(total)

        kkv = k * kk_ref[...]
        kd = k * (1.0 + (a - 1.0) * ka_ref[...])
        sq = kkv * kkv

        s_ref[0, rows, :] = _dot2_exact_rhs(r * kd * rk_ref[...], seg_ref[...])

        ss = [_dot2_exact_rhs(sq[:, sl], bd_ones) for sl in sls]
        kkn = [kkv[:, sl] * lax.rsqrt(jnp.maximum(s, 1e-24)) for sl, s in zip(sls, ss)]
        bvec = [kn * a[:, sl] for sl, kn in zip(sls, kkn)]
        lhs = [bf(jnp.concatenate([-kn * g_excl[:, sl], r[:, sl] * g_incl[:, sl]], axis=0))
               for sl, kn in zip(sls, kkn)]
        rhs = [jnp.concatenate([blockdiag(bf(bv * g_inv[:, sl])), blockdiag(bf(kd[:, sl] * g_inv[:, sl]))], axis=0)
               for sl, bv in zip(sls, bvec)]
        bkh = [bf(jnp.concatenate([bv * g_tail[:, sl], kd[:, sl] * g_tail[:, sl]], axis=0))
               for sl, bv in zip(sls, bvec)]
        vb = [bf(v[:, sl]) for sl in sls]
        v_bd = [blockdiag(x) for x in vb]

        gm = [_dot(lhs[q], rhs[q], NT) for q in quads]
        a_ab = [jnp.where(strict, g[0:C, 0:QUAD], 0.0) for g in gm]
        a_ak = [bf(jnp.where(strict, g[0:C, QUAD:2 * QUAD], 0.0)) for g in gm]
        a_r = [bf(jnp.concatenate([jnp.where(incl, g[C:2 * C, 0:QUAD], 0.0),
                                   jnp.where(incl, g[C:2 * C, QUAD:2 * QUAD], 0.0)], axis=1)) for g in gm]
        akv = [_dot(a_ak[q], v_bd[q]) for q in quads]
        staged.append(dict(rows=rows, lhs=lhs, a_ab=a_ab, a_r=a_r, akv=akv, vb=vb, v_bd=v_bd, bkh=bkh,
                           g_tot=g_tot))

    combos = [(i, q) for i in range(n_sub) for q in quads]
    tinv = {(i, q): eye + jnp.where(levels[0], staged[i]["a_ab"][q], 0.0) for i, q in combos}
    for lvl in levels[1:]:
        tb = {key: bf(t) for key, t in tinv.items()}
        wm = {(i, q): _dot(bf(jnp.where(lvl, staged[i]["a_ab"][q], 0.0)), blockdiag(tb[i, q])) for i, q in combos}
        tinv = {key: tinv[key] + _dot(tb[key], blockdiag(bf(wm[key]))) for key in combos}

    ht = [h_ref[q] for q in quads]
    for i in range(n_sub):
        st = staged[i]
        ar = [_dot(st["lhs"][q], bf(ht[q]), NT) for q in quads]
        rhs_u = [ar[q][0:C] + st["akv"][q] for q in quads]
        u = [bf(_dot(bf(tinv[i, q]), blockdiag(bf(rhs_u[q])))) for q in quads]
        uv_bd = [jnp.concatenate([blockdiag(u[q]), st["v_bd"][q]], axis=0) for q in quads]
        for q in quads:
            y_ref[0, st["rows"], sls[q]] = ar[q][C:2 * C] + _dot(st["a_r"][q], uv_bd[q])
        upd = [_dot(jnp.concatenate([u[q], st["vb"][q]], axis=0), st["bkh"][q], TN) for q in quads]
        ht = [ht[q] * st["g_tot"][:, sls[q]] + jnp.where(same_head, upd[q], 0.0) for q in quads]

    for q in quads:
        h_ref[q] = ht[q]


def _wkv_scan(proj, batch, seq, wuh, wul, auh, aul, w0, a0, k_k, k_a, r_k, seg):
    t_total = batch * seq
    n_sub = math.gcd(seq // CHUNK, SCAN_SUB)
    blk = n_sub * CHUNK
    nc = seq // blk

    def rowblk(b, d, c):
        return b * nc + c + d * (nc - 1 - 2 * c)

    def col(jblk):
        return lambda b, d, c: (rowblk(b, d, c), jblk)

    full = lambda b, d, c: (0, 0)
    perdir = lambda b, d, c: (d, 0, 0)
    return pl.pallas_call(
        functools.partial(_scan_kernel, n_sub=n_sub),
        grid=(batch, 2, nc),
        in_specs=[
            pl.BlockSpec((blk, GROUP), col(0)),
            pl.BlockSpec((blk, GROUP), col(1)),
            pl.BlockSpec((blk, GROUP), col(2)),
            pl.BlockSpec((blk, 128), col(LORA_BLK)),
            pl.BlockSpec((blk, 128), col(LORA_BLK + 1)),
            pl.BlockSpec((1, 128, D_MODEL), perdir),
            pl.BlockSpec((1, 128, D_MODEL), perdir),
            pl.BlockSpec((1, 128, D_MODEL), perdir),
            pl.BlockSpec((1, 128, D_MODEL), perdir),
            pl.BlockSpec((1, 1, D_MODEL), perdir),
            pl.BlockSpec((1, 1, D_MODEL), perdir),
            pl.BlockSpec((1, D_MODEL), full),
            pl.BlockSpec((1, D_MODEL), full),
            pl.BlockSpec((1, D_MODEL), full),
            pl.BlockSpec((D_MODEL, 128), full),
        ],
        out_specs=[
            pl.BlockSpec((1, blk, D_MODEL), lambda b, d, c: (d, rowblk(b, d, c), 0)),
            pl.BlockSpec((1, blk, 128), lambda b, d, c: (d, rowblk(b, d, c), 0)),
        ],
        out_shape=[
            jax.ShapeDtypeStruct((2, t_total, D_MODEL), F32),
            jax.ShapeDtypeStruct((2, t_total, 128), F32),
        ],
        scratch_shapes=[pltpu.VMEM((N_QUAD, QUAD, QUAD), F32)],
        compiler_params=pltpu.CompilerParams(
            dimension_semantics=("arbitrary", "arbitrary", "arbitrary"), vmem_limit_bytes=VMEM_LIMIT),
        name="wkv_scan",
    )(proj, proj, proj, proj, proj, wuh, wul, auh, aul, w0, a0, k_k, k_a, r_k, seg)


def _attn_kernel(q_ref, k_ref, v_ref, lq1_ref, lk1_ref, lq2_ref, lk2_ref, sw_ref, o_ref,
                 k0_ref, k1_ref, vt_ref, s_ref, m_ref, *, seq, tq, tk, unroll, nq, lam_init):
    i = pl.program_id(2)
    n_kc = seq // tk
    lane = lax.broadcasted_iota(jnp.int32, (1, 128), 1)
    first = lane < HEAD

    @pl.when(i == 0)
    def _():
        def prep(kc, carry):
            rows = pl.ds(pl.multiple_of(kc * tk, tk), tk)
            kf = k_ref[rows, :]
            k0_ref[rows, :] = jnp.where(first, kf, 0.0).astype(BF16)
            k1_ref[rows, :] = jnp.where(first, 0.0, kf).astype(BF16)
            vt_ref[:, rows] = v_ref[rows, :].T.astype(BF16)
            return carry
        lax.fori_loop(0, n_kc, prep, 0)

    def rows_of(kc):
        return pl.ds(pl.multiple_of(kc * tk, tk), tk)

    def fold(op, s):
        return op(s.reshape(tk // 8, 8, tq), axis=0)

    def query_t():
        return (q_ref[...] * LOG2_E).T.astype(BF16)

    neg = jnp.full((8, tq), -jnp.inf, F32)
    zl = jnp.zeros((8, tq), F32)
    za = jnp.zeros((128, tq), F32)

    def finish(l0, acc0, l1, acc1):
        l0 = jnp.sum(l0, axis=0, keepdims=True)
        l1 = jnp.sum(l1, axis=0, keepdims=True)
        lam = (jnp.exp(jnp.sum(lq1_ref[...] * lk1_ref[...], axis=-1, keepdims=True))
               - jnp.exp(jnp.sum(lq2_ref[...] * lk2_ref[...], axis=-1, keepdims=True)) + lam_init)
        o_t = acc0 / l0 - lam * (acc1 / l1)
        o = o_t.T
        o = o * lax.rsqrt(jnp.mean(o * o, axis=-1, keepdims=True) + SUBLN_EPS) * sw_ref[...]
        o_ref[...] = o * (1.0 - lam_init)

    def run(slot_a):
        slot_b = 1 - slot_a
        mine = (i % 2) == slot_a

        def score_part(kc, q_t, mx0, mx1):
            rows = rows_of(kc)
            s0 = _dot(k0_ref[rows, :], q_t)
            s1 = _dot(k1_ref[rows, :], q_t)
            s_ref[slot_a, 0, rows, :] = s0
            s_ref[slot_a, 1, rows, :] = s1
            return jnp.maximum(mx0, fold(jnp.max, s0)), jnp.maximum(mx1, fold(jnp.max, s1))

        def value_part(kc, m0, m1, l0, acc0, l1, acc1):
            rows = rows_of(kc)
            vt = vt_ref[:, rows]
            p0 = jnp.exp2(s_ref[slot_b, 0, rows, :] - m0)
            p1 = jnp.exp2(s_ref[slot_b, 1, rows, :] - m1)
            l0 = l0 + fold(jnp.sum, p0)
            l1 = l1 + fold(jnp.sum, p1)
            acc0 = acc0 + _dot(vt, p0.astype(BF16))
            acc1 = acc1 + _dot(vt, p1.astype(BF16))
            return l0, acc0, l1, acc1

        def prev_max():
            return (jnp.max(m_ref[slot_b, 0], axis=0, keepdims=True),
                    jnp.max(m_ref[slot_b, 1], axis=0, keepdims=True))

        def store_max(mx0, mx1):
            m_ref[slot_a, 0] = mx0
            m_ref[slot_a, 1] = mx1

        if slot_a == 0:
            @pl.when(i == 0)
            def _():
                q_t = query_t()
                mx = lax.fori_loop(0, n_kc, lambda kc, c: score_part(kc, q_t, *c), (neg, neg), unroll=unroll)
                store_max(*mx)

        @pl.when(jnp.logical_and(mine, jnp.logical_and(i > 0, i < nq)))
        def _():
            q_t = query_t()
            m0, m1 = prev_max()

            def body(kc, c):
                return score_part(kc, q_t, c[0], c[1]) + value_part(kc, m0, m1, *c[2:])

            c = lax.fori_loop(0, n_kc, body, (neg, neg, zl, za, zl, za), unroll=unroll)
            store_max(c[0], c[1])
            finish(*c[2:])

        if nq % 2 == slot_a:
            @pl.when(i == nq)
            def _():
                m0, m1 = prev_max()
                c = lax.fori_loop(0, n_kc, lambda kc, c: value_part(kc, m0, m1, *c), (zl, za, zl, za),
                                  unroll=unroll)
                finish(*c)

    run(0)
    run(1)


def _diff_attn(proj, batch, seq, lq1, lk1, lq2, lk2, subln_w, lam_init):
    t_total = batch * seq
    tq = min(256, seq)
    tk = min(256, seq)
    unroll = math.gcd(seq // tk, 8)
    nq = seq // tq
    n_heads = N_PAIR
    qb, kb, vb = (G_Q * GROUP // 128, G_KB * GROUP // 128, G_VB * GROUP // 128)
    kern = functools.partial(_attn_kernel, seq=seq, tq=tq, tk=tk, unroll=unroll, nq=nq, lam_init=lam_init)
    small = lambda b, h, i: (0, 0)
    return pl.pallas_call(
        kern,
        grid=(batch, n_heads, nq + 1),
        in_specs=[
            pl.BlockSpec((tq, 128), lambda b, h, i: (b * nq + jnp.minimum(i, nq - 1), qb + h)),
            pl.BlockSpec((seq, 128), lambda b, h, i: (b, kb + h)),
            pl.BlockSpec((seq, 128), lambda b, h, i: (b, vb + h)),
            pl.BlockSpec((1, HEAD), small),
            pl.BlockSpec((1, HEAD), small),
            pl.BlockSpec((1, HEAD), small),
            pl.BlockSpec((1, HEAD), small),
            pl.BlockSpec((1, 128), small),
        ],
        out_specs=pl.BlockSpec((tq, 128), lambda b, h, i: (b * nq + jnp.maximum(i - 1, 0), h)),
        out_shape=jax.ShapeDtypeStruct((t_total, D_MODEL), F32),
        scratch_shapes=[pltpu.VMEM((seq, 128), BF16), pltpu.VMEM((seq, 128), BF16),
                        pltpu.VMEM((128, seq), BF16), pltpu.VMEM((2, 2, seq, tq), F32),
                        pltpu.VMEM((2, 2, 8, tq), F32)],
        compiler_params=pltpu.CompilerParams(
            dimension_semantics=("arbitrary", "arbitrary", "arbitrary"), vmem_limit_bytes=VMEM_LIMIT),
        name="diff_attn",
    )(proj, proj, proj, lq1, lk1, lq2, lk2, subln_w)


def _mix_kernel(x_ref, yf_ref, yb_ref, sf_ref, sb_ref, v_ref, lg_ref, ga_ref, gb_ref, o_ref,
                lnw_ref, lnb_ref, gup_ref, exp_ref, pa_ref, pb_ref, wo_ref, h_ref, ya_ref):
    y = yf_ref[0] + yb_ref[0]
    r128 = lax.broadcasted_iota(jnp.int32, (128, 128), 0) // HEAD
    c128 = lax.broadcasted_iota(jnp.int32, (128, 128), 1) // HEAD
    avg = jnp.where(r128 == c128, 1.0 / HEAD, 0.0).astype(BF16)
    for p in range(N_PAIR):
        sl = slice(128 * p, 128 * (p + 1))
        yp = y[:, sl]
        mean = _dot2_exact_rhs(yp, avg)
        cen = yp - mean
        var = _dot2_exact_rhs(cen * cen, avg)
        ya_ref[:, sl] = cen * lax.rsqrt(var + GN_EPS)
    coef = _dot2_exact_rhs(sf_ref[0] + sb_ref[0], exp_ref[...])
    gate = _dot(_sigmoid(lg_ref[...]).astype(BF16), gup_ref[...])
    y_a = (ya_ref[...] * lnw_ref[...] + lnb_ref[...] + coef * v_ref[...]) * gate
    pa = _dot(y_a.astype(BF16), pa_ref[...])
    pb = _dot(o_ref[...].astype(BF16), pb_ref[...])
    merged = _sigmoid(ga_ref[...]) * pa + _sigmoid(gb_ref[...]) * pb
    h_ref[...] = x_ref[...] + _dot(merged.astype(BF16), wo_ref[...])


def _mix(x2d, proj, y_dir, s_dir, o_attn, ln_w, ln_b, g_up, expand, proj_a, proj_b, w_out):
    t_total = x2d.shape[0]
    tm = 256
    row = lambda i: (i, 0)
    full = lambda i: (0, 0)
    wspec = pl.BlockSpec((D_MODEL, D_MODEL), full)
    return pl.pallas_call(
        _mix_kernel,
        grid=(t_total // tm,),
        in_specs=[
            pl.BlockSpec((tm, D_MODEL), row),
            pl.BlockSpec((1, tm, D_MODEL), lambda i: (0, i, 0)),
            pl.BlockSpec((1, tm, D_MODEL), lambda i: (1, i, 0)),
            pl.BlockSpec((1, tm, 128), lambda i: (0, i, 0)),
            pl.BlockSpec((1, tm, 128), lambda i: (1, i, 0)),
            pl.BlockSpec((tm, GROUP), lambda i: (i, 2)),
            pl.BlockSpec((tm, 128), lambda i: (i, LORA_BLK + 2)),
            pl.BlockSpec((tm, GROUP), lambda i: (i, G_GA)),
            pl.BlockSpec((tm, GROUP), lambda i: (i, G_GB)),
            pl.BlockSpec((tm, D_MODEL), row),
            pl.BlockSpec((1, D_MODEL), full),
            pl.BlockSpec((1, D_MODEL), full),
            pl.BlockSpec((GATE_RANK, D_MODEL), full),
            pl.BlockSpec((128, D_MODEL), full),
            wspec, wspec, wspec,
        ],
        out_specs=pl.BlockSpec((tm, D_MODEL), row),
        out_shape=jax.ShapeDtypeStruct((t_total, D_MODEL), F32),
        scratch_shapes=[pltpu.VMEM((tm, D_MODEL), F32)],
        compiler_params=pltpu.CompilerParams(
            dimension_semantics=("arbitrary",), vmem_limit_bytes=VMEM_LIMIT),
        name="mix",
    )(x2d, y_dir, y_dir, s_dir, s_dir, proj, proj, proj, proj, o_attn,
      ln_w, ln_b, g_up, expand, proj_a, proj_b, w_out)


def _mlp_kernel(h_ref, gm_ref, gf_ref, w1_ref, w2_ref, o_ref, *, ff_chunk):
    h = h_ref[...]
    xn = _rms(h, gm_ref[...]).astype(BF16)
    acc = h
    for c in range(D_FF // ff_chunk):
        sl = slice(c * ff_chunk, (c + 1) * ff_chunk)
        hid = jnp.maximum(_dot(xn, w1_ref[:, sl]), 0.0)
        acc = acc + _dot((hid * hid).astype(BF16), w2_ref[sl, :])
    o_ref[...] = _rms(acc, gf_ref[...])


def _mlp(h2d, norm_mlp, norm_final, w1, w2):
    t_total = h2d.shape[0]
    tm = 256
    row = lambda i: (i, 0)
    full = lambda i: (0, 0)
    kern = functools.partial(_mlp_kernel, ff_chunk=1024)
    return pl.pallas_call(
        kern,
        grid=(t_total // tm,),
        in_specs=[
            pl.BlockSpec((tm, D_MODEL), row),
            pl.BlockSpec((1, D_MODEL), full),
            pl.BlockSpec((1, D_MODEL), full),
            pl.BlockSpec((D_MODEL, D_FF), full),
            pl.BlockSpec((D_FF, D_MODEL), full),
        ],
        out_specs=pl.BlockSpec((tm, D_MODEL), row),
        out_shape=jax.ShapeDtypeStruct((t_total, D_MODEL), F32),
        compiler_params=pltpu.CompilerParams(
            dimension_semantics=("arbitrary",), vmem_limit_bytes=VMEM_LIMIT),
        name="mlp",
    )(h2d, norm_mlp, norm_final, w1, w2)


def _rope_tables(seq):
    half = ROPE_DIM // 2
    pos = jnp.arange(seq, dtype=F32)
    inv_freq = ROPE_THETA ** (-jnp.arange(0, ROPE_DIM, 2, dtype=F32) / ROPE_DIM)
    ang = pos[:, None] * inv_freq[None, :]
    cos, sin = jnp.cos(ang), jnp.sin(ang)
    pad = jnp.zeros((seq, HEAD - ROPE_DIM), F32)
    z8 = jnp.zeros((seq, half), F32)
    c64 = jnp.concatenate([cos, cos, pad + 1.0], axis=1)
    sa64 = jnp.concatenate([z8, sin, pad], axis=1)
    sb64 = jnp.concatenate([-sin, z8, pad], axis=1)
    two = lambda t: jnp.concatenate([t, t], axis=1)
    return two(c64), two(sa64), two(sb64)


def _layer(x, l, prm, norm_final):
    batch, seq, _ = x.shape
    x2d = x.reshape(batch * seq, D_MODEL)
    cos_t, sa_t, sb_t = _rope_tables(seq)
    proj = _inproj(x2d, seq, prm["norm_mix"], prm["w_re"], prm["mu_re"], cos_t, sa_t, sb_t)
    y_dir, s_dir = _wkv_scan(proj, batch, seq, prm["wuh"], prm["wul"], prm["auh"], prm["aul"],
                             prm["w0"], prm["a0"], prm["k_k"], prm["k_a"], prm["r_k"], prm["seg"])
    lam_init = 0.8 - 0.6 * math.exp(-0.3 * l)
    o_attn = _diff_attn(proj, batch, seq, prm["lq1"], prm["lk1"], prm["lq2"], prm["lk2"],
                        prm["subln_w"], lam_init)
    h = _mix(x2d, proj, y_dir, s_dir, o_attn, prm["ln_w"], prm["ln_b"], prm["g_up"], prm["expand"],
             prm["proj_a"], prm["proj_b"], prm["w_out"])
    return h


def _prep_layer(l, w_in, mu_shift, w0, w_lora_up, a0, a_lora_up, g_lora_up, k_k, k_a, r_k, ln_x_w,
                ln_x_b, lambda_q1, lambda_k1, lambda_q2, lambda_k2, subln_w, proj_a, proj_b, w_out,
                norm_mix, norm_mlp, w_mlp_in, w_mlp_out):
    w = w_in[l]
    pad_cols = N_GROUPS * GROUP - w.shape[1]
    w_re = jnp.concatenate([w[:, :RWKV_MAIN], w[:, RWKV_COLS:], w[:, RWKV_MAIN:RWKV_COLS],
                            jnp.zeros((D_MODEL, pad_cols), F32)], axis=1).astype(BF16)
    mu = mu_shift[l]
    mu_re = jnp.concatenate([mu[:RWKV_MAIN], jnp.zeros((5 * GROUP,), F32), mu[RWKV_MAIN:],
                             jnp.zeros((pad_cols,), F32)])[None, :]

    def lora_pad(up):
        z = jnp.zeros_like(up[0])
        return jnp.stack([jnp.concatenate([up[0], z], axis=0), jnp.concatenate([z, up[1]], axis=0)])

    wu = lora_pad(w_lora_up[l])
    au = lora_pad(a_lora_up[l])
    wuh = wu.astype(BF16)
    auh = au.astype(BF16)
    head_of_lane = jnp.arange(D_MODEL) // HEAD
    seg = (head_of_lane[:, None] == jnp.arange(128)[None, :]).astype(BF16)
    return dict(
        w_re=w_re, mu_re=mu_re, norm_mix=norm_mix[l][None, :],
        wuh=wuh, wul=(wu - wuh.astype(F32)).astype(BF16),
        auh=auh, aul=(au - auh.astype(F32)).astype(BF16),
        w0=w0[l][:, None, :], a0=a0[l][:, None, :],
        k_k=k_k[l][None, :], k_a=k_a[l][None, :], r_k=r_k[l].reshape(1, D_MODEL),
        seg=seg, expand=seg.T,
        lq1=lambda_q1[l][None, :], lk1=lambda_k1[l][None, :],
        lq2=lambda_q2[l][None, :], lk2=lambda_k2[l][None, :],
        subln_w=subln_w[l][None, :],
        ln_w=ln_x_w[l][None, :], ln_b=ln_x_b[l][None, :],
        g_up=g_lora_up[l].astype(BF16),
        proj_a=proj_a[l].astype(BF16), proj_b=proj_b[l].astype(BF16), w_out=w_out[l].astype(BF16),
        norm_mlp=norm_mlp[l][None, :],
        w1=w_mlp_in[l].astype(BF16), w2=w_mlp_out[l].astype(BF16),
    )


def kernel(x_prompt, x_sample, w_in, mu_shift, w0, w_lora_up, a0, a_lora_up, g_lora_up, k_k, k_a, r_k, ln_x_w, ln_x_b, lambda_q1, lambda_k1, lambda_q2, lambda_k2, subln_w, proj_a, proj_b, w_out, norm_mix, norm_mlp, w_mlp_in, w_mlp_out, norm_final):
    depth = w_in.shape[0]
    assert depth == 1, "the final norm is fused into the (single) layer's MLP kernel"
    prm = _prep_layer(0, w_in, mu_shift, w0, w_lora_up, a0, a_lora_up, g_lora_up, k_k, k_a, r_k, ln_x_w,
                      ln_x_b, lambda_q1, lambda_k1, lambda_q2, lambda_k2, subln_w, proj_a, proj_b,
                      w_out, norm_mix, norm_mlp, w_mlp_in, w_mlp_out)
    outs = []
    for x in (x_prompt, x_sample):
        h = _layer(x, 0, prm, norm_final)
        y = _mlp(h, prm["norm_mlp"], norm_final[None, :], prm["w1"], prm["w2"])
        outs.append(y.reshape(x.shape))
    return tuple(outs)
```

```python
import functools
import math

import jax
import jax.numpy as jnp
from jax import lax
from jax.experimental import pallas as pl
from jax.experimental.pallas import tpu as pltpu

F32 = jnp.float32
BF16 = jnp.bfloat16

D_MODEL = 1024
HEAD = 64
N_PAIR = D_MODEL // 128
DECAY_RANK = 64
ICLR_RANK = 64
GATE_RANK = 128
GN_EPS = 64e-5
ROPE_THETA = 500000.0
ROPE_DIM = 16
SUBLN_EPS = 1e-5
NORM_EPS = 1e-6
D_FF = 4 * D_MODEL
RWKV_MAIN = 3 * D_MODEL
RWKV_COLS = RWKV_MAIN + 2 * DECAY_RANK + 2 * ICLR_RANK + GATE_RANK
GROUP = 1024
N_GROUPS = 9
G_Q, G_KB, G_VB, G_GA, G_GB, G_LORA = 3, 4, 5, 6, 7, 8
GROUPS_PER_STEP = 3
LORA_BLK = G_LORA * GROUP // 128
CHUNK = 64
EXP_M05 = math.exp(-0.5)
LOG2_E = math.log2(math.e)
VMEM_LIMIT = 56 * 1024 * 1024

NN = (((1,), (0,)), ((), ()))
NT = (((1,), (1,)), ((), ()))
TN = (((0,), (0,)), ((), ()))


def _dot(a, b, dims=NN):
    return lax.dot_general(a, b, dims, preferred_element_type=F32)


def _split(x):
    hi = x.astype(BF16)
    lo = (x - hi.astype(F32)).astype(BF16)
    return hi, lo


def _dot3(a, b, dims=NN):
    ah, al = _split(a)
    bh, bl = _split(b)
    return _dot(ah, bh, dims) + (_dot(ah, bl, dims) + _dot(al, bh, dims))


def _dot2_exact_rhs(a, b_bf16, dims=NN):
    ah, al = _split(a)
    return _dot(ah, b_bf16, dims) + _dot(al, b_bf16, dims)


def _sigmoid(x):
    return 1.0 / (1.0 + jnp.exp(-x))


def _rms(x, g):
    return x * lax.rsqrt(jnp.mean(x * x, axis=-1, keepdims=True) + NORM_EPS) * g


def _inproj_kernel(x_ref, xp_ref, xn_ref, g_ref, w_ref, mu_ref, cos_ref, sa_ref, sb_ref,
                   o_ref, xs_ref, hs_ref, *, tm, tiles_per_seq):
    i = pl.program_id(0)
    j = pl.program_id(1)

    @pl.when(j == 0)
    def _():
        g = g_ref[...]
        xs_ref[...] = _rms(x_ref[...], g).astype(BF16)
        hs_ref[0:8, :] = _rms(xp_ref[...], g)
        hs_ref[8:16, :] = _rms(xn_ref[...], g)

    def shifted(p, w, cols):
        ph = _dot(hs_ref[...].astype(BF16), w)
        t_in_seq = i % tiles_per_seq
        prev = jnp.where(t_in_seq == 0, 0.0, ph[7:8, :])
        nxt = jnp.where(t_in_seq == tiles_per_seq - 1, 0.0, ph[8:9, :])
        rows = lax.broadcasted_iota(jnp.int32, (tm, 1), 0)
        up = jnp.where(rows == 0, prev, pltpu.roll(p, 1, 0))
        dn = jnp.where(rows == tm - 1, nxt, pltpu.roll(p, tm - 1, 0))
        return p + (0.5 * (up + dn) - p) * mu_ref[:, cols]

    def rotated(p, scale):
        c = jnp.tile(cos_ref[...], (1, N_PAIR))
        sa = jnp.tile(sa_ref[...], (1, N_PAIR))
        sb = jnp.tile(sb_ref[...], (1, N_PAIR))
        half = ROPE_DIM // 2
        out = p * c + pltpu.roll(p, half, 1) * sa + pltpu.roll(p, GROUP - half, 1) * sb
        return out if scale == 1.0 else out * scale

    for step in range(N_GROUPS // GROUPS_PER_STEP):
        @pl.when(j == step)
        def _(step=step):
            for gi in range(GROUPS_PER_STEP):
                g = step * GROUPS_PER_STEP + gi
                cols = slice(gi * GROUP, (gi + 1) * GROUP)
                w = w_ref[:, cols]
                p = _dot(xs_ref[...], w)
                if g < 3 or g == G_LORA:
                    p = shifted(p, w, cols)
                elif g == G_Q:
                    p = rotated(p, HEAD ** -0.5)
                elif g == G_KB:
                    p = rotated(p, 1.0)
                o_ref[:, cols] = p


def _inproj(x2d, seq, g, w_re, mu_re, cos_t, sa_t, sb_t):
    t_total = x2d.shape[0]
    tm = min(512, seq)
    tiles_per_seq = seq // tm
    n_tiles = t_total // tm
    last8 = t_total // 8 - 1
    kern = functools.partial(_inproj_kernel, tm=tm, tiles_per_seq=tiles_per_seq)
    return pl.pallas_call(
        kern,
        grid=(n_tiles, N_GROUPS // GROUPS_PER_STEP),
        in_specs=[
            pl.BlockSpec((tm, D_MODEL), lambda i, j: (i, 0)),
            pl.BlockSpec((8, D_MODEL), lambda i, j: (jnp.maximum(i * (tm // 8) - 1, 0), 0)),
            pl.BlockSpec((8, D_MODEL), lambda i, j: (jnp.minimum((i + 1) * (tm // 8), last8), 0)),
            pl.BlockSpec((1, D_MODEL), lambda i, j: (0, 0)),
            pl.BlockSpec((D_MODEL, GROUPS_PER_STEP * GROUP), lambda i, j: (0, j)),
            pl.BlockSpec((1, GROUPS_PER_STEP * GROUP), lambda i, j: (0, j)),
            pl.BlockSpec((tm, 128), lambda i, j: (i % tiles_per_seq, 0)),
            pl.BlockSpec((tm, 128), lambda i, j: (i % tiles_per_seq, 0)),
            pl.BlockSpec((tm, 128), lambda i, j: (i % tiles_per_seq, 0)),
        ],
        out_specs=pl.BlockSpec((tm, GROUPS_PER_STEP * GROUP), lambda i, j: (i, j)),
        out_shape=jax.ShapeDtypeStruct((t_total, N_GROUPS * GROUP), F32),
        scratch_shapes=[pltpu.VMEM((tm, D_MODEL), BF16), pltpu.VMEM((16, D_MODEL), F32)],
        compiler_params=pltpu.CompilerParams(
            dimension_semantics=("arbitrary", "arbitrary"), vmem_limit_bytes=VMEM_LIMIT),
        name="inproj",
    )(x2d, x2d, x2d, g, w_re, mu_re, cos_t, sa_t, sb_t)


QUAD = 256
N_QUAD = D_MODEL // QUAD
HEADS_PER_QUAD = QUAD // HEAD
SCAN_SUB = 4


def _scan_kernel(r_ref, k_ref, v_ref, lw_ref, la_ref, wuh_ref, wul_ref, au_ref,
                 w0_ref, a0_ref, kk_ref, ka_ref, rk_ref, seg_ref, y_ref, s_ref, h_ref, *, n_sub):
    d = pl.program_id(1)
    c = pl.program_id(2)
    C = CHUNK
    bf = lambda x: x.astype(BF16)

    @pl.when(c == 0)
    def _():
        h_ref[...] = jnp.zeros_like(h_ref)

    sgn = 1 - 2 * d
    rr = lax.broadcasted_iota(jnp.int32, (C, C), 0)
    cc = lax.broadcasted_iota(jnp.int32, (C, C), 1)
    tri = jnp.where((rr - cc) * sgn >= 0, 1.0, 0.0).astype(BF16)

    tr = lax.broadcasted_iota(jnp.int32, (C, QUAD), 0)
    tc = lax.broadcasted_iota(jnp.int32, (C, QUAD), 1) % C
    dd = (tr - tc) * sgn
    strict = dd > 0
    incl = dd >= 0
    eye = jnp.where(dd == 0, 1.0, 0.0)
    levels = []
    m = 1
    while m < C:
        levels.append(jnp.logical_and((tr // (2 * m)) == (tc // (2 * m)), (tr // m) != (tc // m)))
        m *= 2
    lane_head = lax.broadcasted_iota(jnp.int32, (1, QUAD), 1) // HEAD
    head_lanes = [lane_head == h for h in range(HEADS_PER_QUAD)]
    same_head = (lax.broadcasted_iota(jnp.int32, (QUAD, QUAD), 0) // HEAD) == (
        lax.broadcasted_iota(jnp.int32, (QUAD, QUAD), 1) // HEAD)
    bd_ones = jnp.where(same_head, 1.0, 0.0).astype(BF16)

    def blockdiag(x):
        return jnp.concatenate([jnp.where(hl, x, 0.0) for hl in head_lanes], axis=0)

    quads = range(N_QUAD)
    sls = [slice(QUAD * q, QUAD * (q + 1)) for q in quads]

    staged = []
    for i in range(n_sub):
        jj = i + d * (n_sub - 1 - 2 * i)
        rows = pl.ds(pl.multiple_of(jj * C, C), C)
        r = r_ref[rows, :]
        k = k_ref[rows, :]
        v = v_ref[rows, :]

        thh, thl = _split(jnp.tanh(lw_ref[rows, :]))
        w_raw = w0_ref[0] + (_dot(thh, wuh_ref[0]) + (_dot(thh, wul_ref[0]) + _dot(thl, wuh_ref[0])))
        logw = -EXP_M05 * _sigmoid(w_raw)
        a = _sigmoid(a0_ref[0] + _dot(bf(la_ref[rows, :]), au_ref[0]))

        l1 = logw.astype(BF16)
        e1 = logw - l1.astype(F32)
        l2 = e1.astype(BF16)
        l3 = (e1 - l2.astype(F32)).astype(BF16)
        cum = _dot(tri, l1) + (_dot(tri, l2) + _dot(tri, l3))
        total = jnp.sum(logw, axis=0, keepdims=True)
        g_incl = jnp.exp(cum)
        g_excl = jnp.exp(cum - logw)
        g_inv = jnp.exp(-cum)
        g_tail = jnp.exp(total - cum)
        g_tot = jnp.exp(total)

        kkv = k * kk_ref[...]
        kd = k * (1.0 + (a - 1.0) * ka_ref[...])
        sq = kkv * kkv

        s_ref[0, rows, :] = _dot(bf(r * kd * rk_ref[...]), seg_ref[...])

        ss = [_dot(bf(sq[:, sl]), bd_ones) for sl in sls]
        kkn = [kkv[:, sl] * lax.rsqrt(jnp.maximum(s, 1e-24)) for sl, s in zip(sls, ss)]
        bvec = [kn * a[:, sl] for sl, kn in zip(sls, kkn)]
        lhs = [bf(jnp.concatenate([-kn * g_excl[:, sl], r[:, sl] * g_incl[:, sl]], axis=0))
               for sl, kn in zip(sls, kkn)]
        rhs = [jnp.concatenate([blockdiag(bf(bv * g_inv[:, sl])), blockdiag(bf(kd[:, sl] * g_inv[:, sl]))], axis=0)
               for sl, bv in zip(sls, bvec)]
        bkh = [bf(jnp.concatenate([bv * g_tail[:, sl], kd[:, sl] * g_tail[:, sl]], axis=0))
               for sl, bv in zip(sls, bvec)]
        vb = [bf(v[:, sl]) for sl in sls]
        v_bd = [blockdiag(x) for x in vb]

        gm = [_dot(lhs[q], rhs[q], NT) for q in quads]
        a_ab = [jnp.where(strict, g[0:C, 0:QUAD], 0.0) for g in gm]
        a_ak = [bf(jnp.where(strict, g[0:C, QUAD:2 * QUAD], 0.0)) for g in gm]
        a_r = [bf(jnp.concatenate([jnp.where(incl, g[C:2 * C, 0:QUAD], 0.0),
                                   jnp.where(incl, g[C:2 * C, QUAD:2 * QUAD], 0.0)], axis=1)) for g in gm]
        akv = [_dot(a_ak[q], v_bd[q]) for q in quads]
        staged.append(dict(rows=rows, lhs=lhs, a_ab=a_ab, a_r=a_r, akv=akv, vb=vb, v_bd=v_bd, bkh=bkh,
                           g_tot=g_tot))

    combos = [(i, q) for i in range(n_sub) for q in quads]
    tinv = {(i, q): eye + jnp.where(levels[0], staged[i]["a_ab"][q], 0.0) for i, q in combos}
    for lvl in levels[1:]:
        tb = {key: bf(t) for key, t in tinv.items()}
        wm = {(i, q): _dot(bf(jnp.where(lvl, staged[i]["a_ab"][q], 0.0)), blockdiag(tb[i, q])) for i, q in combos}
        tinv = {key: tinv[key] + _dot(tb[key], blockdiag(bf(wm[key]))) for key in combos}

    ht = [h_ref[q] for q in quads]
    for i in range(n_sub):
        st = staged[i]
        ar = [_dot(st["lhs"][q], bf(ht[q]), NT) for q in quads]
        rhs_u = [ar[q][0:C] + st["akv"][q] for q in quads]
        u = [bf(_dot(bf(tinv[i, q]), blockdiag(bf(rhs_u[q])))) for q in quads]
        uv_bd = [jnp.concatenate([blockdiag(u[q]), st["v_bd"][q]], axis=0) for q in quads]
        for q in quads:
            y_ref[0, st["rows"], sls[q]] = ar[q][C:2 * C] + _dot(st["a_r"][q], uv_bd[q])
        upd = [_dot(jnp.concatenate([u[q], st["vb"][q]], axis=0), st["bkh"][q], TN) for q in quads]
        ht = [ht[q] * st["g_tot"][:, sls[q]] + jnp.where(same_head, upd[q], 0.0) for q in quads]

    for q in quads:
        h_ref[q] = ht[q]


def _wkv_scan(proj, batch, seq, wuh, wul, au, w0, a0, k_k, k_a, r_k, seg):
    t_total = batch * seq
    n_sub = math.gcd(seq // CHUNK, SCAN_SUB)
    blk = n_sub * CHUNK
    nc = seq // blk

    def rowblk(b, d, c):
        return b * nc + c + d * (nc - 1 - 2 * c)

    def col(jblk):
        return lambda b, d, c: (rowblk(b, d, c), jblk)

    full = lambda b, d, c: (0, 0)
    perdir = lambda b, d, c: (d, 0, 0)
    return pl.pallas_call(
        functools.partial(_scan_kernel, n_sub=n_sub),
        grid=(batch, 2, nc),
        in_specs=[
            pl.BlockSpec((blk, GROUP), col(0)),
            pl.BlockSpec((blk, GROUP), col(1)),
            pl.BlockSpec((blk, GROUP), col(2)),
            pl.BlockSpec((blk, 128), col(LORA_BLK)),
            pl.BlockSpec((blk, 128), col(LORA_BLK + 1)),
            pl.BlockSpec((1, 128, D_MODEL), perdir),
            pl.BlockSpec((1, 128, D_MODEL), perdir),
            pl.BlockSpec((1, 128, D_MODEL), perdir),
            pl.BlockSpec((1, 1, D_MODEL), perdir),
            pl.BlockSpec((1, 1, D_MODEL), perdir),
            pl.BlockSpec((1, D_MODEL), full),
            pl.BlockSpec((1, D_MODEL), full),
            pl.BlockSpec((1, D_MODEL), full),
            pl.BlockSpec((D_MODEL, 128), full),
        ],
        out_specs=[
            pl.BlockSpec((1, blk, D_MODEL), lambda b, d, c: (d, rowblk(b, d, c), 0)),
            pl.BlockSpec((1, blk, 128), lambda b, d, c: (d, rowblk(b, d, c), 0)),
        ],
        out_shape=[
            jax.ShapeDtypeStruct((2, t_total, D_MODEL), F32),
            jax.ShapeDtypeStruct((2, t_total, 128), F32),
        ],
        scratch_shapes=[pltpu.VMEM((N_QUAD, QUAD, QUAD), F32)],
        compiler_params=pltpu.CompilerParams(
            dimension_semantics=("arbitrary", "arbitrary", "arbitrary"), vmem_limit_bytes=VMEM_LIMIT),
        name="wkv_scan",
    )(proj, proj, proj, proj, proj, wuh, wul, au, w0, a0, k_k, k_a, r_k, seg)


V_ROWS = 128 + 16


def _attn_kernel(q_ref, k_ref, v_ref, lq1_ref, lk1_ref, lq2_ref, lk2_ref, sw_ref, o_ref,
                 k0_ref, k1_ref, vt_ref, s_ref, m_ref, p_ref, *, seq, tq, tk, per_trip, nq, lam_init):
    i = pl.program_id(2)
    n_kc = seq // tk
    lane = lax.broadcasted_iota(jnp.int32, (1, 128), 1)
    first = lane < HEAD

    @pl.when(i == 0)
    def _():
        def prep(kc, carry):
            rows = pl.ds(pl.multiple_of(kc * tk, tk), tk)
            kf = k_ref[rows, :]
            k0_ref[rows, :] = jnp.where(first, kf, 0.0).astype(BF16)
            k1_ref[rows, :] = jnp.where(first, 0.0, kf).astype(BF16)
            vt_ref[0:128, rows] = v_ref[rows, :].T.astype(BF16)
            vt_ref[128:V_ROWS, rows] = jnp.ones((V_ROWS - 128, tk), BF16)
            return carry
        lax.fori_loop(0, n_kc, prep, 0)

    def rows_of(kc):
        return pl.ds(pl.multiple_of(kc * tk, tk), tk)

    def fold(op, s):
        return op(s.reshape(tk // 8, 8, tq), axis=0)

    def query_t():
        return (q_ref[...] * LOG2_E).T.astype(BF16)

    neg = jnp.full((8, tq), -jnp.inf, F32)
    za = jnp.zeros((V_ROWS, tq), F32)
    n_trips = n_kc // per_trip

    def finish(acc0, acc1):
        l0 = acc0[128:129, :]
        l1 = acc1[128:129, :]
        lam = (jnp.exp(jnp.sum(lq1_ref[...] * lk1_ref[...], axis=-1, keepdims=True))
               - jnp.exp(jnp.sum(lq2_ref[...] * lk2_ref[...], axis=-1, keepdims=True)) + lam_init)
        o_t = acc0[0:128, :] / l0 - lam * (acc1[0:128, :] / l1)
        o = o_t.T
        o = o * lax.rsqrt(jnp.mean(o * o, axis=-1, keepdims=True) + SUBLN_EPS) * sw_ref[...]
        o_ref[...] = o * (1.0 - lam_init)

    def run(slot_a):
        slot_b = 1 - slot_a
        mine = (i % 2) == slot_a

        def trip(t, q_t, mx, m_prev, acc):
            for c in range(per_trip):
                rows = rows_of(t * per_trip + c)
                if q_t is not None:
                    s0 = _dot(k0_ref[rows, :], q_t)
                    s1 = _dot(k1_ref[rows, :], q_t)
                    s_ref[slot_a, 0, rows, :] = s0
                    s_ref[slot_a, 1, rows, :] = s1
                    mx = (jnp.maximum(mx[0], fold(jnp.max, s0)), jnp.maximum(mx[1], fold(jnp.max, s1)))
                if acc is not None:
                    for comp in range(2):
                        x = s_ref[slot_b, comp, rows, :] - m_prev[comp]
                        p_ref[comp, c * tk:(c + 1) * tk, :] = jnp.exp2(x.astype(BF16))
            if acc is not None:
                span = per_trip * tk
                vt = vt_ref[:, pl.ds(pl.multiple_of(t * span, span), span)]
                acc = tuple(acc[comp] + _dot(vt, p_ref[comp]) for comp in range(2))
            return mx, acc

        def loop(body, init):
            if n_trips == 1:
                return body(0, init)
            return lax.fori_loop(0, n_trips, body, init)

        def prev_max():
            return (jnp.max(m_ref[slot_b, 0], axis=0, keepdims=True),
                    jnp.max(m_ref[slot_b, 1], axis=0, keepdims=True))

        def store_max(mx):
            m_ref[slot_a, 0] = mx[0]
            m_ref[slot_a, 1] = mx[1]

        if slot_a == 0:
            @pl.when(i == 0)
            def _():
                q_t = query_t()
                store_max(loop(lambda t, mx: trip(t, q_t, mx, None, None)[0], (neg, neg)))

        @pl.when(jnp.logical_and(mine, jnp.logical_and(i > 0, i < nq)))
        def _():
            q_t = query_t()
            m_prev = prev_max()
            mx, acc = loop(lambda t, c: trip(t, q_t, c[0], m_prev, c[1]), ((neg, neg), (za, za)))
            store_max(mx)
            finish(*acc)

        if nq % 2 == slot_a:
            @pl.when(i == nq)
            def _():
                m_prev = prev_max()
                finish(*loop(lambda t, acc: trip(t, None, None, m_prev, acc)[1], (za, za)))

    run(0)
    run(1)


def _diff_attn(proj, batch, seq, lq1, lk1, lq2, lk2, subln_w, lam_init):
    t_total = batch * seq
    tq = min(256, seq)
    tk = min(256, seq)
    per_trip = math.gcd(seq // tk, 16)
    nq = seq // tq
    n_heads = N_PAIR
    qb, kb, vb = (G_Q * GROUP // 128, G_KB * GROUP // 128, G_VB * GROUP // 128)
    kern = functools.partial(_attn_kernel, seq=seq, tq=tq, tk=tk, per_trip=per_trip, nq=nq, lam_init=lam_init)
    small = lambda b, h, i: (0, 0)
    return pl.pallas_call(
        kern,
        grid=(batch, n_heads, nq + 1),
        in_specs=[
            pl.BlockSpec((tq, 128), lambda b, h, i: (b * nq + jnp.minimum(i, nq - 1), qb + h)),
            pl.BlockSpec((seq, 128), lambda b, h, i: (b, kb + h)),
            pl.BlockSpec((seq, 128), lambda b, h, i: (b, vb + h)),
            pl.BlockSpec((1, HEAD), small),
            pl.BlockSpec((1, HEAD), small),
            pl.BlockSpec((1, HEAD), small),
            pl.BlockSpec((1, HEAD), small),
            pl.BlockSpec((1, 128), small),
        ],
        out_specs=pl.BlockSpec((tq, 128), lambda b, h, i: (b * nq + jnp.maximum(i - 1, 0), h)),
        out_shape=jax.ShapeDtypeStruct((t_total, D_MODEL), F32),
        scratch_shapes=[pltpu.VMEM((seq, 128), BF16), pltpu.VMEM((seq, 128), BF16),
                        pltpu.VMEM((V_ROWS, seq), BF16), pltpu.VMEM((2, 2, seq, tq), F32),
                        pltpu.VMEM((2, 2, 8, tq), F32), pltpu.VMEM((2, per_trip * tk, tq), BF16)],
        compiler_params=pltpu.CompilerParams(
            dimension_semantics=("arbitrary", "arbitrary", "arbitrary"), vmem_limit_bytes=VMEM_LIMIT),
        name="diff_attn",
    )(proj, proj, proj, lq1, lk1, lq2, lk2, subln_w)


def _mix_kernel(x_ref, yf_ref, yb_ref, sf_ref, sb_ref, v_ref, lg_ref, ga_ref, gb_ref, o_ref,
                lnw_ref, lnb_ref, gup_ref, exp_ref, pa_ref, pb_ref, wo_ref, h_ref, ya_ref):
    y = yf_ref[0] + yb_ref[0]
    rh = lax.broadcasted_iota(jnp.int32, (QUAD, QUAD), 0) // HEAD
    ch = lax.broadcasted_iota(jnp.int32, (QUAD, QUAD), 1) // HEAD
    avg = jnp.where(rh == ch, 1.0 / HEAD, 0.0).astype(BF16)
    for q in range(N_QUAD):
        sl = slice(QUAD * q, QUAD * (q + 1))
        yq = y[:, sl]
        cen = yq - _dot(yq.astype(BF16), avg)
        var = _dot((cen * cen).astype(BF16), avg)
        ya_ref[:, sl] = cen * lax.rsqrt(var + GN_EPS)
    coef = _dot((sf_ref[0] + sb_ref[0]).astype(BF16), exp_ref[...])
    gate = _dot(_sigmoid(lg_ref[...]).astype(BF16), gup_ref[...])
    y_a = (ya_ref[...] * lnw_ref[...] + lnb_ref[...] + coef * v_ref[...]) * gate
    pa = _dot(y_a.astype(BF16), pa_ref[...])
    pb = _dot(o_ref[...].astype(BF16), pb_ref[...])
    merged = _sigmoid(ga_ref[...]) * pa + _sigmoid(gb_ref[...]) * pb
    h_ref[...] = x_ref[...] + _dot(merged.astype(BF16), wo_ref[...])


def _mix(x2d, proj, y_dir, s_dir, o_attn, ln_w, ln_b, g_up, expand, proj_a, proj_b, w_out):
    t_total = x2d.shape[0]
    tm = 256
    row = lambda i: (i, 0)
    full = lambda i: (0, 0)
    wspec = pl.BlockSpec((D_MODEL, D_MODEL), full)
    return pl.pallas_call(
        _mix_kernel,
        grid=(t_total // tm,),
        in_specs=[
            pl.BlockSpec((tm, D_MODEL), row),
            pl.BlockSpec((1, tm, D_MODEL), lambda i: (0, i, 0)),
            pl.BlockSpec((1, tm, D_MODEL), lambda i: (1, i, 0)),
            pl.BlockSpec((1, tm, 128), lambda i: (0, i, 0)),
            pl.BlockSpec((1, tm, 128), lambda i: (1, i, 0)),
            pl.BlockSpec((tm, GROUP), lambda i: (i, 2)),
            pl.BlockSpec((tm, 128), lambda i: (i, LORA_BLK + 2)),
            pl.BlockSpec((tm, GROUP), lambda i: (i, G_GA)),
            pl.BlockSpec((tm, GROUP), lambda i: (i, G_GB)),
            pl.BlockSpec((tm, D_MODEL), row),
            pl.BlockSpec((1, D_MODEL), full),
            pl.BlockSpec((1, D_MODEL), full),
            pl.BlockSpec((GATE_RANK, D_MODEL), full),
            pl.BlockSpec((128, D_MODEL), full),
            wspec, wspec, wspec,
        ],
        out_specs=pl.BlockSpec((tm, D_MODEL), row),
        out_shape=jax.ShapeDtypeStruct((t_total, D_MODEL), F32),
        scratch_shapes=[pltpu.VMEM((tm, D_MODEL), F32)],
        compiler_params=pltpu.CompilerParams(
            dimension_semantics=("arbitrary",), vmem_limit_bytes=VMEM_LIMIT),
        name="mix",
    )(x2d, y_dir, y_dir, s_dir, s_dir, proj, proj, proj, proj, o_attn,
      ln_w, ln_b, g_up, expand, proj_a, proj_b, w_out)


def _mlp_kernel(h_ref, gm_ref, gf_ref, w1_ref, w2_ref, o_ref, *, ff_chunk):
    h = h_ref[...]
    xn = _rms(h, gm_ref[...]).astype(BF16)
    acc = h
    for c in range(D_FF // ff_chunk):
        sl = slice(c * ff_chunk, (c + 1) * ff_chunk)
        hid = jnp.maximum(_dot(xn, w1_ref[:, sl]), 0.0)
        acc = acc + _dot((hid * hid).astype(BF16), w2_ref[sl, :])
    o_ref[...] = _rms(acc, gf_ref[...])


def _mlp(h2d, norm_mlp, norm_final, w1, w2):
    t_total = h2d.shape[0]
    tm = 256
    row = lambda i: (i, 0)
    full = lambda i: (0, 0)
    kern = functools.partial(_mlp_kernel, ff_chunk=1024)
    return pl.pallas_call(
        kern,
        grid=(t_total // tm,),
        in_specs=[
            pl.BlockSpec((tm, D_MODEL), row),
            pl.BlockSpec((1, D_MODEL), full),
            pl.BlockSpec((1, D_MODEL), full),
            pl.BlockSpec((D_MODEL, D_FF), full),
            pl.BlockSpec((D_FF, D_MODEL), full),
        ],
        out_specs=pl.BlockSpec((tm, D_MODEL), row),
        out_shape=jax.ShapeDtypeStruct((t_total, D_MODEL), F32),
        compiler_params=pltpu.CompilerParams(
            dimension_semantics=("arbitrary",), vmem_limit_bytes=VMEM_LIMIT),
        name="mlp",
    )(h2d, norm_mlp, norm_final, w1, w2)


def _rope_tables(seq):
    half = ROPE_DIM // 2
    pos = jnp.arange(seq, dtype=F32)
    inv_freq = ROPE_THETA ** (-jnp.arange(0, ROPE_DIM, 2, dtype=F32) / ROPE_DIM)
    ang = pos[:, None] * inv_freq[None, :]
    cos, sin = jnp.cos(ang), jnp.sin(ang)
    pad = jnp.zeros((seq, HEAD - ROPE_DIM), F32)
    z8 = jnp.zeros((seq, half), F32)
    c64 = jnp.concatenate([cos, cos, pad + 1.0], axis=1)
    sa64 = jnp.concatenate([z8, sin, pad], axis=1)
    sb64 = jnp.concatenate([-sin, z8, pad], axis=1)
    two = lambda t: jnp.concatenate([t, t], axis=1)
    return two(c64), two(sa64), two(sb64)


def _layer(x, l, prm, norm_final):
    batch, seq, _ = x.shape
    x2d = x.reshape(batch * seq, D_MODEL)
    cos_t, sa_t, sb_t = _rope_tables(seq)
    proj = _inproj(x2d, seq, prm["norm_mix"], prm["w_re"], prm["mu_re"], cos_t, sa_t, sb_t)
    y_dir, s_dir = _wkv_scan(proj, batch, seq, prm["wuh"], prm["wul"], prm["au"],
                             prm["w0"], prm["a0"], prm["k_k"], prm["k_a"], prm["r_k"], prm["seg"])
    lam_init = 0.8 - 0.6 * math.exp(-0.3 * l)
    o_attn = _diff_attn(proj, batch, seq, prm["lq1"], prm["lk1"], prm["lq2"], prm["lk2"],
                        prm["subln_w"], lam_init)
    h = _mix(x2d, proj, y_dir, s_dir, o_attn, prm["ln_w"], prm["ln_b"], prm["g_up"], prm["expand"],
             prm["proj_a"], prm["proj_b"], prm["w_out"])
    return h


def _prep_layer(l, w_in, mu_shift, w0, w_lora_up, a0, a_lora_up, g_lora_up, k_k, k_a, r_k, ln_x_w,
                ln_x_b, lambda_q1, lambda_k1, lambda_q2, lambda_k2, subln_w, proj_a, proj_b, w_out,
                norm_mix, norm_mlp, w_mlp_in, w_mlp_out):
    w = w_in[l]
    pad_cols = N_GROUPS * GROUP - w.shape[1]
    w_re = jnp.concatenate([w[:, :RWKV_MAIN], w[:, RWKV_COLS:], w[:, RWKV_MAIN:RWKV_COLS],
                            jnp.zeros((D_MODEL, pad_cols), F32)], axis=1).astype(BF16)
    mu = mu_shift[l]
    mu_re = jnp.concatenate([mu[:RWKV_MAIN], jnp.zeros((5 * GROUP,), F32), mu[RWKV_MAIN:],
                             jnp.zeros((pad_cols,), F32)])[None, :]

    def lora_pad(up):
        z = jnp.zeros_like(up[0])
        return jnp.stack([jnp.concatenate([up[0], z], axis=0), jnp.concatenate([z, up[1]], axis=0)])

    wu = lora_pad(w_lora_up[l])
    au = lora_pad(a_lora_up[l])
    wuh = wu.astype(BF16)
    head_of_lane = jnp.arange(D_MODEL) // HEAD
    seg = (head_of_lane[:, None] == jnp.arange(128)[None, :]).astype(BF16)
    return dict(
        w_re=w_re, mu_re=mu_re, norm_mix=norm_mix[l][None, :],
        wuh=wuh, wul=(wu - wuh.astype(F32)).astype(BF16),
        au=au.astype(BF16),
        w0=w0[l][:, None, :], a0=a0[l][:, None, :],
        k_k=k_k[l][None, :], k_a=k_a[l][None, :], r_k=r_k[l].reshape(1, D_MODEL),
        seg=seg, expand=seg.T,
        lq1=lambda_q1[l][None, :], lk1=lambda_k1[l][None, :],
        lq2=lambda_q2[l][None, :], lk2=lambda_k2[l][None, :],
        subln_w=subln_w[l][None, :],
        ln_w=ln_x_w[l][None, :], ln_b=ln_x_b[l][None, :],
        g_up=g_lora_up[l].astype(BF16),
        proj_a=proj_a[l].astype(BF16), proj_b=proj_b[l].astype(BF16), w_out=w_out[l].astype(BF16),
        norm_mlp=norm_mlp[l][None, :],
        w1=w_mlp_in[l].astype(BF16), w2=w_mlp_out[l].astype(BF16),
    )


def kernel(x_prompt, x_sample, w_in, mu_shift, w0, w_lora_up, a0, a_lora_up, g_lora_up, k_k, k_a, r_k, ln_x_w, ln_x_b, lambda_q1, lambda_k1, lambda_q2, lambda_k2, subln_w, proj_a, proj_b, w_out, norm_mix, norm_mlp, w_mlp_in, w_mlp_out, norm_final):
    depth = w_in.shape[0]
    assert depth == 1, "the final norm is fused into the (single) layer's MLP kernel"
    prm = _prep_layer(0, w_in, mu_shift, w0, w_lora_up, a0, a_lora_up, g_lora_up, k_k, k_a, r_k, ln_x_w,
                      ln_x_b, lambda_q1, lambda_k1, lambda_q2, lambda_k2, subln_w, proj_a, proj_b,
                      w_out, norm_mix, norm_mlp, w_mlp_in, w_mlp_out)
    outs = []
    for x in (x_prompt, x_sample):
        h = _layer(x, 0, prm, norm_final)
        y = _mlp(h, prm["norm_mlp"], norm_final[None, :], prm["w1"], prm["w2"])
        outs.append(y.reshape(x.shape))
    return tuple(outs)
```

```python
import functools
import math

import jax
import jax.numpy as jnp
from jax import lax
from jax.experimental import pallas as pl
from jax.experimental.pallas import tpu as pltpu

F32 = jnp.float32
BF16 = jnp.bfloat16

D_MODEL = 1024
HEAD = 64
N_PAIR = D_MODEL // 128
DECAY_RANK = 64
ICLR_RANK = 64
GATE_RANK = 128
GN_EPS = 64e-5
ROPE_THETA = 500000.0
ROPE_DIM = 16
SUBLN_EPS = 1e-5
NORM_EPS = 1e-6
D_FF = 4 * D_MODEL
RWKV_MAIN = 3 * D_MODEL
RWKV_COLS = RWKV_MAIN + 2 * DECAY_RANK + 2 * ICLR_RANK + GATE_RANK
GROUP = 1024
N_GROUPS = 9
G_Q, G_KB, G_VB, G_GA, G_GB, G_LORA = 3, 4, 5, 6, 7, 8
GROUPS_PER_STEP = 3
LORA_BLK = G_LORA * GROUP // 128
CHUNK = 64
EXP_M05 = math.exp(-0.5)
LOG2_E = math.log2(math.e)
VMEM_LIMIT = 56 * 1024 * 1024

NN = (((1,), (0,)), ((), ()))
NT = (((1,), (1,)), ((), ()))
TN = (((0,), (0,)), ((), ()))


def _dot(a, b, dims=NN):
    return lax.dot_general(a, b, dims, preferred_element_type=F32)


def _split(x):
    hi = x.astype(BF16)
    lo = (x - hi.astype(F32)).astype(BF16)
    return hi, lo


def _dot3(a, b, dims=NN):
    ah, al = _split(a)
    bh, bl = _split(b)
    return _dot(ah, bh, dims) + (_dot(ah, bl, dims) + _dot(al, bh, dims))


def _dot2_exact_rhs(a, b_bf16, dims=NN):
    ah, al = _split(a)
    return _dot(ah, b_bf16, dims) + _dot(al, b_bf16, dims)


def _sigmoid(x):
    return 1.0 / (1.0 + jnp.exp(-x))


def _rms(x, g):
    return x * lax.rsqrt(jnp.mean(x * x, axis=-1, keepdims=True) + NORM_EPS) * g


def _inproj_kernel(x_ref, xp_ref, xn_ref, g_ref, w_ref, mu_ref, cos_ref, sa_ref, sb_ref,
                   o_ref, xs_ref, hs_ref, *, tm, tiles_per_seq):
    i = pl.program_id(0)
    j = pl.program_id(1)

    @pl.when(j == 0)
    def _():
        g = g_ref[...]
        xs_ref[...] = _rms(x_ref[...], g).astype(BF16)
        hs_ref[0:8, :] = _rms(xp_ref[...], g)
        hs_ref[8:16, :] = _rms(xn_ref[...], g)

    def store_shifted(p, w, cols):
        ph = _dot(hs_ref[...].astype(BF16), w)
        t_in_seq = i % tiles_per_seq
        prev = jnp.where(t_in_seq == 0, 0.0, ph[7:8, :])
        nxt = jnp.where(t_in_seq == tiles_per_seq - 1, 0.0, ph[8:9, :])
        c2 = 0.5 * mu_ref[:, cols]
        c1 = 1.0 - mu_ref[:, cols]
        core = p * c1 + (pltpu.roll(p, 1, 0) + pltpu.roll(p, tm - 1, 0)) * c2
        o_ref[:, cols] = core
        o_ref[0:1, cols] = core[0:1, :] + (prev - p[tm - 1:tm, :]) * c2
        o_ref[tm - 1:tm, cols] = core[tm - 1:tm, :] + (nxt - p[0:1, :]) * c2

    def rotated(p, scale):
        c = jnp.tile(cos_ref[...], (1, N_PAIR))
        sa = jnp.tile(sa_ref[...], (1, N_PAIR))
        sb = jnp.tile(sb_ref[...], (1, N_PAIR))
        half = ROPE_DIM // 2
        out = p * c + pltpu.roll(p, half, 1) * sa + pltpu.roll(p, GROUP - half, 1) * sb
        return out if scale == 1.0 else out * scale

    for step in range(N_GROUPS // GROUPS_PER_STEP):
        @pl.when(j == step)
        def _(step=step):
            for gi in range(GROUPS_PER_STEP):
                g = step * GROUPS_PER_STEP + gi
                cols = slice(gi * GROUP, (gi + 1) * GROUP)
                w = w_ref[:, cols]
                p = _dot(xs_ref[...], w)
                if g < 3 or g == G_LORA:
                    store_shifted(p, w, cols)
                elif g == G_Q:
                    o_ref[:, cols] = rotated(p, HEAD ** -0.5)
                elif g == G_KB:
                    o_ref[:, cols] = rotated(p, 1.0)
                else:
                    o_ref[:, cols] = p


def _inproj(x2d, seq, g, w_re, mu_re, cos_t, sa_t, sb_t):
    t_total = x2d.shape[0]
    tm = min(512, seq)
    tiles_per_seq = seq // tm
    n_tiles = t_total // tm
    last8 = t_total // 8 - 1
    kern = functools.partial(_inproj_kernel, tm=tm, tiles_per_seq=tiles_per_seq)
    return pl.pallas_call(
        kern,
        grid=(n_tiles, N_GROUPS // GROUPS_PER_STEP),
        in_specs=[
            pl.BlockSpec((tm, D_MODEL), lambda i, j: (i, 0)),
            pl.BlockSpec((8, D_MODEL), lambda i, j: (jnp.maximum(i * (tm // 8) - 1, 0), 0)),
            pl.BlockSpec((8, D_MODEL), lambda i, j: (jnp.minimum((i + 1) * (tm // 8), last8), 0)),
            pl.BlockSpec((1, D_MODEL), lambda i, j: (0, 0)),
            pl.BlockSpec((D_MODEL, GROUPS_PER_STEP * GROUP), lambda i, j: (0, j)),
            pl.BlockSpec((1, GROUPS_PER_STEP * GROUP), lambda i, j: (0, j)),
            pl.BlockSpec((tm, 128), lambda i, j: (i % tiles_per_seq, 0)),
            pl.BlockSpec((tm, 128), lambda i, j: (i % tiles_per_seq, 0)),
            pl.BlockSpec((tm, 128), lambda i, j: (i % tiles_per_seq, 0)),
        ],
        out_specs=pl.BlockSpec((tm, GROUPS_PER_STEP * GROUP), lambda i, j: (i, j)),
        out_shape=jax.ShapeDtypeStruct((t_total, N_GROUPS * GROUP), F32),
        scratch_shapes=[pltpu.VMEM((tm, D_MODEL), BF16), pltpu.VMEM((16, D_MODEL), F32)],
        compiler_params=pltpu.CompilerParams(
            dimension_semantics=("arbitrary", "arbitrary"), vmem_limit_bytes=VMEM_LIMIT),
        name="inproj",
    )(x2d, x2d, x2d, g, w_re, mu_re, cos_t, sa_t, sb_t)


QUAD = 256
N_QUAD = D_MODEL // QUAD
HEADS_PER_QUAD = QUAD // HEAD
SCAN_SUB = 4


def _scan_kernel(r_ref, k_ref, v_ref, lw_ref, la_ref, wuh_ref, wul_ref, au_ref,
                 w0_ref, a0_ref, kk_ref, ka_ref, rk_ref, seg_ref, y_ref, s_ref, h_ref, *, n_sub):
    d = pl.program_id(1)
    c = pl.program_id(2)
    C = CHUNK
    bf = lambda x: x.astype(BF16)

    @pl.when(c == 0)
    def _():
        h_ref[...] = jnp.zeros_like(h_ref)

    sgn = 1 - 2 * d
    rr = lax.broadcasted_iota(jnp.int32, (C, C), 0)
    cc = lax.broadcasted_iota(jnp.int32, (C, C), 1)
    tri = jnp.where((rr - cc) * sgn >= 0, 1.0, 0.0).astype(BF16)

    tr = lax.broadcasted_iota(jnp.int32, (C, QUAD), 0)
    tc = lax.broadcasted_iota(jnp.int32, (C, QUAD), 1) % C
    dd = (tr - tc) * sgn
    strict = dd > 0
    incl = dd >= 0
    eye = jnp.where(dd == 0, 1.0, 0.0)
    levels = []
    m = 1
    while m < C:
        levels.append(jnp.logical_and((tr // (2 * m)) == (tc // (2 * m)), (tr // m) != (tc // m)))
        m *= 2
    lane_head = lax.broadcasted_iota(jnp.int32, (1, QUAD), 1) // HEAD
    head_lanes = [lane_head == h for h in range(HEADS_PER_QUAD)]
    same_head = (lax.broadcasted_iota(jnp.int32, (QUAD, QUAD), 0) // HEAD) == (
        lax.broadcasted_iota(jnp.int32, (QUAD, QUAD), 1) // HEAD)
    bd_ones = jnp.where(same_head, 1.0, 0.0).astype(BF16)

    def blockdiag(x):
        return jnp.concatenate([jnp.where(hl, x, 0.0) for hl in head_lanes], axis=0)

    quads = range(N_QUAD)
    sls = [slice(QUAD * q, QUAD * (q + 1)) for q in quads]

    staged = []
    for i in range(n_sub):
        jj = i + d * (n_sub - 1 - 2 * i)
        rows = pl.ds(pl.multiple_of(jj * C, C), C)
        r = r_ref[rows, :]
        k = k_ref[rows, :]
        v = v_ref[rows, :]

        thh, thl = _split(jnp.tanh(lw_ref[rows, :]))
        w_raw = w0_ref[0] + (_dot(thh, wuh_ref[0]) + (_dot(thh, wul_ref[0]) + _dot(thl, wuh_ref[0])))
        logw = -EXP_M05 * _sigmoid(w_raw)
        a = _sigmoid(a0_ref[0] + _dot(bf(la_ref[rows, :]), au_ref[0]))

        l1 = logw.astype(BF16)
        e1 = logw - l1.astype(F32)
        l2 = e1.astype(BF16)
        l3 = (e1 - l2.astype(F32)).astype(BF16)
        cum = _dot(tri, l1) + (_dot(tri, l2) + _dot(tri, l3))
        total = jnp.sum(logw, axis=0, keepdims=True)
        g_incl = jnp.exp(cum)
        g_excl = jnp.exp(cum - logw)
        g_inv = jnp.exp(-cum)
        g_tail = jnp.exp(total - cum)
        g_tot = jnp.exp(total)

        kkv = k * kk_ref[...]
        kd = k * (1.0 + (a - 1.0) * ka_ref[...])
        sq = kkv * kkv

        s_ref[0, rows, :] = _dot(bf(r * kd * rk_ref[...]), seg_ref[...])

        ss = [_dot(bf(sq[:, sl]), bd_ones) for sl in sls]
        kkn = [kkv[:, sl] * lax.rsqrt(jnp.maximum(s, 1e-24)) for sl, s in zip(sls, ss)]
        bvec = [kn * a[:, sl] for sl, kn in zip(sls, kkn)]
        lhs = [bf(jnp.concatenate([-kn * g_excl[:, sl], r[:, sl] * g_incl[:, sl]], axis=0))
               for sl, kn in zip(sls, kkn)]
        rhs = [jnp.concatenate([blockdiag(bf(bv * g_inv[:, sl])), blockdiag(bf(kd[:, sl] * g_inv[:, sl]))], axis=0)
               for sl, bv in zip(sls, bvec)]
        bkh = [bf(jnp.concatenate([bv * g_tail[:, sl], kd[:, sl] * g_tail[:, sl]], axis=0))
               for sl, bv in zip(sls, bvec)]
        vb = [bf(v[:, sl]) for sl in sls]
        v_bd = [blockdiag(x) for x in vb]

        gm = [_dot(lhs[q], rhs[q], NT) for q in quads]
        a_ab = [jnp.where(strict, g[0:C, 0:QUAD], 0.0) for g in gm]
        a_ak = [bf(jnp.where(strict, g[0:C, QUAD:2 * QUAD], 0.0)) for g in gm]
        a_r = [bf(jnp.concatenate([jnp.where(incl, g[C:2 * C, 0:QUAD], 0.0),
                                   jnp.where(incl, g[C:2 * C, QUAD:2 * QUAD], 0.0)], axis=1)) for g in gm]
        akv = [_dot(a_ak[q], v_bd[q]) for q in quads]
        staged.append(dict(rows=rows, lhs=lhs, a_ab=a_ab, a_r=a_r, akv=akv, vb=vb, v_bd=v_bd, bkh=bkh,
                           g_tot=g_tot))

    combos = [(i, q) for i in range(n_sub) for q in quads]
    tinv = {(i, q): eye + jnp.where(levels[0], staged[i]["a_ab"][q], 0.0) for i, q in combos}
    for lvl in levels[1:]:
        tb = {key: bf(t) for key, t in tinv.items()}
        wm = {(i, q): _dot(bf(jnp.where(lvl, staged[i]["a_ab"][q], 0.0)), blockdiag(tb[i, q])) for i, q in combos}
        tinv = {key: tinv[key] + _dot(tb[key], blockdiag(bf(wm[key]))) for key in combos}

    ht = [h_ref[q] for q in quads]
    for i in range(n_sub):
        st = staged[i]
        ar = [_dot(st["lhs"][q], bf(ht[q]), NT) for q in quads]
        rhs_u = [ar[q][0:C] + st["akv"][q] for q in quads]
        u = [bf(_dot(bf(tinv[i, q]), blockdiag(bf(rhs_u[q])))) for q in quads]
        uv_bd = [jnp.concatenate([blockdiag(u[q]), st["v_bd"][q]], axis=0) for q in quads]
        for q in quads:
            y_ref[0, st["rows"], sls[q]] = ar[q][C:2 * C] + _dot(st["a_r"][q], uv_bd[q])
        upd = [_dot(jnp.concatenate([u[q], st["vb"][q]], axis=0), st["bkh"][q], TN) for q in quads]
        ht = [ht[q] * st["g_tot"][:, sls[q]] + jnp.where(same_head, upd[q], 0.0) for q in quads]

    for q in quads:
        h_ref[q] = ht[q]


def _wkv_scan(proj, batch, seq, wuh, wul, au, w0, a0, k_k, k_a, r_k, seg):
    t_total = batch * seq
    n_sub = math.gcd(seq // CHUNK, SCAN_SUB)
    blk = n_sub * CHUNK
    nc = seq // blk

    def rowblk(b, d, c):
        return b * nc + c + d * (nc - 1 - 2 * c)

    def col(jblk):
        return lambda b, d, c: (rowblk(b, d, c), jblk)

    full = lambda b, d, c: (0, 0)
    perdir = lambda b, d, c: (d, 0, 0)
    return pl.pallas_call(
        functools.partial(_scan_kernel, n_sub=n_sub),
        grid=(batch, 2, nc),
        in_specs=[
            pl.BlockSpec((blk, GROUP), col(0)),
            pl.BlockSpec((blk, GROUP), col(1)),
            pl.BlockSpec((blk, GROUP), col(2)),
            pl.BlockSpec((blk, 128), col(LORA_BLK)),
            pl.BlockSpec((blk, 128), col(LORA_BLK + 1)),
            pl.BlockSpec((1, 128, D_MODEL), perdir),
            pl.BlockSpec((1, 128, D_MODEL), perdir),
            pl.BlockSpec((1, 128, D_MODEL), perdir),
            pl.BlockSpec((1, 1, D_MODEL), perdir),
            pl.BlockSpec((1, 1, D_MODEL), perdir),
            pl.BlockSpec((1, D_MODEL), full),
            pl.BlockSpec((1, D_MODEL), full),
            pl.BlockSpec((1, D_MODEL), full),
            pl.BlockSpec((D_MODEL, 128), full),
        ],
        out_specs=[
            pl.BlockSpec((1, blk, D_MODEL), lambda b, d, c: (d, rowblk(b, d, c), 0)),
            pl.BlockSpec((1, blk, 128), lambda b, d, c: (d, rowblk(b, d, c), 0)),
        ],
        out_shape=[
            jax.ShapeDtypeStruct((2, t_total, D_MODEL), F32),
            jax.ShapeDtypeStruct((2, t_total, 128), F32),
        ],
        scratch_shapes=[pltpu.VMEM((N_QUAD, QUAD, QUAD), F32)],
        compiler_params=pltpu.CompilerParams(
            dimension_semantics=("arbitrary", "arbitrary", "arbitrary"), vmem_limit_bytes=VMEM_LIMIT),
        name="wkv_scan",
    )(proj, proj, proj, proj, proj, wuh, wul, au, w0, a0, k_k, k_a, r_k, seg)


V_ROWS = 128 + 16
PV_GROUP = 2


def _attn_kernel(q_ref, k_ref, v_ref, lq1_ref, lk1_ref, lq2_ref, lk2_ref, sw_ref, o_ref,
                 k0_ref, k1_ref, vt_ref, s_ref, m_ref, p_ref, *, seq, tq, tk, per_trip, nq, lam_init):
    i = pl.program_id(2)
    n_kc = seq // tk
    lane = lax.broadcasted_iota(jnp.int32, (1, 128), 1)
    first = lane < HEAD

    @pl.when(i == 0)
    def _():
        def prep(kc, carry):
            rows = pl.ds(pl.multiple_of(kc * tk, tk), tk)
            kf = k_ref[rows, :]
            k0_ref[rows, :] = jnp.where(first, kf, 0.0).astype(BF16)
            k1_ref[rows, :] = jnp.where(first, 0.0, kf).astype(BF16)
            vt_ref[0:128, rows] = v_ref[rows, :].T.astype(BF16)
            vt_ref[128:V_ROWS, rows] = jnp.ones((V_ROWS - 128, tk), BF16)
            return carry
        lax.fori_loop(0, n_kc, prep, 0)

    def rows_of(kc):
        return pl.ds(pl.multiple_of(kc * tk, tk), tk)

    def fold(op, s):
        return op(s.reshape(tk // 8, 8, tq), axis=0)

    def query_t():
        return (q_ref[...] * LOG2_E).T.astype(BF16)

    neg = jnp.full((8, tq), -jnp.inf, F32)
    za = jnp.zeros((V_ROWS, tq), F32)
    n_trips = n_kc // per_trip

    def finish(acc0, acc1):
        l0 = acc0[128:129, :]
        l1 = acc1[128:129, :]
        lam = (jnp.exp(jnp.sum(lq1_ref[...] * lk1_ref[...], axis=-1, keepdims=True))
               - jnp.exp(jnp.sum(lq2_ref[...] * lk2_ref[...], axis=-1, keepdims=True)) + lam_init)
        o_t = acc0[0:128, :] / l0 - lam * (acc1[0:128, :] / l1)
        o = o_t.T
        o = o * lax.rsqrt(jnp.mean(o * o, axis=-1, keepdims=True) + SUBLN_EPS) * sw_ref[...]
        o_ref[...] = o * (1.0 - lam_init)

    def run(slot_a):
        slot_b = 1 - slot_a
        mine = (i % 2) == slot_a

        def trip(t, q_t, mx, m_prev, acc):
            for c in range(per_trip):
                rows = rows_of(t * per_trip + c)
                if q_t is not None:
                    s0 = _dot(k0_ref[rows, :], q_t)
                    s1 = _dot(k1_ref[rows, :], q_t)
                    s_ref[slot_a, 0, rows, :] = s0
                    s_ref[slot_a, 1, rows, :] = s1
                    mx = (jnp.maximum(mx[0], fold(jnp.max, s0)), jnp.maximum(mx[1], fold(jnp.max, s1)))
                if acc is not None:
                    for comp in range(2):
                        x = s_ref[slot_b, comp, rows, :] - m_prev[comp]
                        p_ref[comp, c * tk:(c + 1) * tk, :] = jnp.exp2(x.astype(BF16))
                    if (c + 1) % PV_GROUP == 0 or c + 1 == per_trip:
                        c0 = (c // PV_GROUP) * PV_GROUP
                        span = (c + 1 - c0) * tk
                        start = (t * per_trip + c0) * tk
                        vt = vt_ref[:, pl.ds(pl.multiple_of(start, tk), span)]
                        acc = tuple(acc[comp] + _dot(vt, p_ref[comp, c0 * tk:(c + 1) * tk, :])
                                    for comp in range(2))
            return mx, acc

        def loop(body, init):
            if n_trips == 1:
                return body(0, init)
            return lax.fori_loop(0, n_trips, body, init)

        def prev_max():
            return (jnp.max(m_ref[slot_b, 0], axis=0, keepdims=True),
                    jnp.max(m_ref[slot_b, 1], axis=0, keepdims=True))

        def store_max(mx):
            m_ref[slot_a, 0] = mx[0]
            m_ref[slot_a, 1] = mx[1]

        if slot_a == 0:
            @pl.when(i == 0)
            def _():
                q_t = query_t()
                store_max(loop(lambda t, mx: trip(t, q_t, mx, None, None)[0], (neg, neg)))

        @pl.when(jnp.logical_and(mine, jnp.logical_and(i > 0, i < nq)))
        def _():
            q_t = query_t()
            m_prev = prev_max()
            mx, acc = loop(lambda t, c: trip(t, q_t, c[0], m_prev, c[1]), ((neg, neg), (za, za)))
            store_max(mx)
            finish(*acc)

        if nq % 2 == slot_a:
            @pl.when(i == nq)
            def _():
                m_prev = prev_max()
                finish(*loop(lambda t, acc: trip(t, None, None, m_prev, acc)[1], (za, za)))

    run(0)
    run(1)


def _diff_attn(proj, batch, seq, lq1, lk1, lq2, lk2, subln_w, lam_init):
    t_total = batch * seq
    tq = min(512 if seq <= 2048 else 256, seq)
    tk = min(256, seq)
    per_trip = math.gcd(seq // tk, 16)
    nq = seq // tq
    n_heads = N_PAIR
    qb, kb, vb = (G_Q * GROUP // 128, G_KB * GROUP // 128, G_VB * GROUP // 128)
    kern = functools.partial(_attn_kernel, seq=seq, tq=tq, tk=tk, per_trip=per_trip, nq=nq, lam_init=lam_init)
    small = lambda b, h, i: (0, 0)
    return pl.pallas_call(
        kern,
        grid=(batch, n_heads, nq + 1),
        in_specs=[
            pl.BlockSpec((tq, 128), lambda b, h, i: (b * nq + jnp.minimum(i, nq - 1), qb + h)),
            pl.BlockSpec((seq, 128), lambda b, h, i: (b, kb + h)),
            pl.BlockSpec((seq, 128), lambda b, h, i: (b, vb + h)),
            pl.BlockSpec((1, HEAD), small),
            pl.BlockSpec((1, HEAD), small),
            pl.BlockSpec((1, HEAD), small),
            pl.BlockSpec((1, HEAD), small),
            pl.BlockSpec((1, 128), small),
        ],
        out_specs=pl.BlockSpec((tq, 128), lambda b, h, i: (b * nq + jnp.maximum(i - 1, 0), h)),
        out_shape=jax.ShapeDtypeStruct((t_total, D_MODEL), F32),
        scratch_shapes=[pltpu.VMEM((seq, 128), BF16), pltpu.VMEM((seq, 128), BF16),
                        pltpu.VMEM((V_ROWS, seq), BF16), pltpu.VMEM((2, 2, seq, tq), F32),
                        pltpu.VMEM((2, 2, 8, tq), F32), pltpu.VMEM((2, per_trip * tk, tq), BF16)],
        compiler_params=pltpu.CompilerParams(
            dimension_semantics=("arbitrary", "arbitrary", "arbitrary"), vmem_limit_bytes=VMEM_LIMIT),
        name="diff_attn",
    )(proj, proj, proj, lq1, lk1, lq2, lk2, subln_w)


def _mix_kernel(x_ref, yf_ref, yb_ref, sf_ref, sb_ref, v_ref, lg_ref, ga_ref, gb_ref, o_ref,
                lnw_ref, lnb_ref, gup_ref, exp_ref, pa_ref, pb_ref, wo_ref, h_ref, ya_ref):
    y = yf_ref[0] + yb_ref[0]
    rh = lax.broadcasted_iota(jnp.int32, (QUAD, QUAD), 0) // HEAD
    ch = lax.broadcasted_iota(jnp.int32, (QUAD, QUAD), 1) // HEAD
    avg = jnp.where(rh == ch, 1.0 / HEAD, 0.0).astype(BF16)
    for q in range(N_QUAD):
        sl = slice(QUAD * q, QUAD * (q + 1))
        yq = y[:, sl]
        cen = yq - _dot(yq.astype(BF16), avg)
        var = _dot((cen * cen).astype(BF16), avg)
        ya_ref[:, sl] = cen * lax.rsqrt(var + GN_EPS)
    coef = _dot((sf_ref[0] + sb_ref[0]).astype(BF16), exp_ref[...])
    gate = _dot(_sigmoid(lg_ref[...]).astype(BF16), gup_ref[...])
    y_a = (ya_ref[...] * lnw_ref[...] + lnb_ref[...] + coef * v_ref[...]) * gate
    pa = _dot(y_a.astype(BF16), pa_ref[...])
    pb = _dot(o_ref[...].astype(BF16), pb_ref[...])
    merged = _sigmoid(ga_ref[...]) * pa + _sigmoid(gb_ref[...]) * pb
    h_ref[...] = x_ref[...] + _dot(merged.astype(BF16), wo_ref[...])


def _mix(x2d, proj, y_dir, s_dir, o_attn, ln_w, ln_b, g_up, expand, proj_a, proj_b, w_out):
    t_total = x2d.shape[0]
    tm = 256
    row = lambda i: (i, 0)
    full = lambda i: (0, 0)
    wspec = pl.BlockSpec((D_MODEL, D_MODEL), full)
    return pl.pallas_call(
        _mix_kernel,
        grid=(t_total // tm,),
        in_specs=[
            pl.BlockSpec((tm, D_MODEL), row),
            pl.BlockSpec((1, tm, D_MODEL), lambda i: (0, i, 0)),
            pl.BlockSpec((1, tm, D_MODEL), lambda i: (1, i, 0)),
            pl.BlockSpec((1, tm, 128), lambda i: (0, i, 0)),
            pl.BlockSpec((1, tm, 128), lambda i: (1, i, 0)),
            pl.BlockSpec((tm, GROUP), lambda i: (i, 2)),
            pl.BlockSpec((tm, 128), lambda i: (i, LORA_BLK + 2)),
            pl.BlockSpec((tm, GROUP), lambda i: (i, G_GA)),
            pl.BlockSpec((tm, GROUP), lambda i: (i, G_GB)),
            pl.BlockSpec((tm, D_MODEL), row),
            pl.BlockSpec((1, D_MODEL), full),
            pl.BlockSpec((1, D_MODEL), full),
            pl.BlockSpec((GATE_RANK, D_MODEL), full),
            pl.BlockSpec((128, D_MODEL), full),
            wspec, wspec, wspec,
        ],
        out_specs=pl.BlockSpec((tm, D_MODEL), row),
        out_shape=jax.ShapeDtypeStruct((t_total, D_MODEL), F32),
        scratch_shapes=[pltpu.VMEM((tm, D_MODEL), F32)],
        compiler_params=pltpu.CompilerParams(
            dimension_semantics=("arbitrary",), vmem_limit_bytes=VMEM_LIMIT),
        name="mix",
    )(x2d, y_dir, y_dir, s_dir, s_dir, proj, proj, proj, proj, o_attn,
      ln_w, ln_b, g_up, expand, proj_a, proj_b, w_out)


def _mlp_kernel(h_ref, gm_ref, gf_ref, w1_ref, w2_ref, o_ref, *, ff_chunk):
    h = h_ref[...]
    xn = _rms(h, gm_ref[...]).astype(BF16)
    acc = h
    for c in range(D_FF // ff_chunk):
        sl = slice(c * ff_chunk, (c + 1) * ff_chunk)
        hid = jnp.maximum(_dot(xn, w1_ref[:, sl]), 0.0)
        acc = acc + _dot((hid * hid).astype(BF16), w2_ref[sl, :])
    o_ref[...] = _rms(acc, gf_ref[...])


def _mlp(h2d, norm_mlp, norm_final, w1, w2):
    t_total = h2d.shape[0]
    tm = 256
    row = lambda i: (i, 0)
    full = lambda i: (0, 0)
    kern = functools.partial(_mlp_kernel, ff_chunk=1024)
    return pl.pallas_call(
        kern,
        grid=(t_total // tm,),
        in_specs=[
            pl.BlockSpec((tm, D_MODEL), row),
            pl.BlockSpec((1, D_MODEL), full),
            pl.BlockSpec((1, D_MODEL), full),
            pl.BlockSpec((D_MODEL, D_FF), full),
            pl.BlockSpec((D_FF, D_MODEL), full),
        ],
        out_specs=pl.BlockSpec((tm, D_MODEL), row),
        out_shape=jax.ShapeDtypeStruct((t_total, D_MODEL), F32),
        compiler_params=pltpu.CompilerParams(
            dimension_semantics=("arbitrary",), vmem_limit_bytes=VMEM_LIMIT),
        name="mlp",
    )(h2d, norm_mlp, norm_final, w1, w2)


def _rope_tables(seq):
    half = ROPE_DIM // 2
    pos = jnp.arange(seq, dtype=F32)
    inv_freq = ROPE_THETA ** (-jnp.arange(0, ROPE_DIM, 2, dtype=F32) / ROPE_DIM)
    ang = pos[:, None] * inv_freq[None, :]
    cos, sin = jnp.cos(ang), jnp.sin(ang)
    pad = jnp.zeros((seq, HEAD - ROPE_DIM), F32)
    z8 = jnp.zeros((seq, half), F32)
    c64 = jnp.concatenate([cos, cos, pad + 1.0], axis=1)
    sa64 = jnp.concatenate([z8, sin, pad], axis=1)
    sb64 = jnp.concatenate([-sin, z8, pad], axis=1)
    two = lambda t: jnp.concatenate([t, t], axis=1)
    return two(c64), two(sa64), two(sb64)


def _layer(x, l, prm, norm_final):
    batch, seq, _ = x.shape
    x2d = x.reshape(batch * seq, D_MODEL)
    cos_t, sa_t, sb_t = _rope_tables(seq)
    proj = _inproj(x2d, seq, prm["norm_mix"], prm["w_re"], prm["mu_re"], cos_t, sa_t, sb_t)
    y_dir, s_dir = _wkv_scan(proj, batch, seq, prm["wuh"], prm["wul"], prm["au"],
                             prm["w0"], prm["a0"], prm["k_k"], prm["k_a"], prm["r_k"], prm["seg"])
    lam_init = 0.8 - 0.6 * math.exp(-0.3 * l)
    o_attn = _diff_attn(proj, batch, seq, prm["lq1"], prm["lk1"], prm["lq2"], prm["lk2"],
                        prm["subln_w"], lam_init)
    h = _mix(x2d, proj, y_dir, s_dir, o_attn, prm["ln_w"], prm["ln_b"], prm["g_up"], prm["expand"],
             prm["proj_a"], prm["proj_b"], prm["w_out"])
    return h


def _prep_layer(l, w_in, mu_shift, w0, w_lora_up, a0, a_lora_up, g_lora_up, k_k, k_a, r_k, ln_x_w,
                ln_x_b, lambda_q1, lambda_k1, lambda_q2, lambda_k2, subln_w, proj_a, proj_b, w_out,
                norm_mix, norm_mlp, w_mlp_in, w_mlp_out):
    w = w_in[l]
    pad_cols = N_GROUPS * GROUP - w.shape[1]
    w_re = jnp.concatenate([w[:, :RWKV_MAIN], w[:, RWKV_COLS:], w[:, RWKV_MAIN:RWKV_COLS],
                            jnp.zeros((D_MODEL, pad_cols), F32)], axis=1).astype(BF16)
    mu = mu_shift[l]
    mu_re = jnp.concatenate([mu[:RWKV_MAIN], jnp.zeros((5 * GROUP,), F32), mu[RWKV_MAIN:],
                             jnp.zeros((pad_cols,), F32)])[None, :]

    def lora_pad(up):
        z = jnp.zeros_like(up[0])
        return jnp.stack([jnp.concatenate([up[0], z], axis=0), jnp.concatenate([z, up[1]], axis=0)])

    wu = lora_pad(w_lora_up[l])
    au = lora_pad(a_lora_up[l])
    wuh = wu.astype(BF16)
    head_of_lane = jnp.arange(D_MODEL) // HEAD
    seg = (head_of_lane[:, None] == jnp.arange(128)[None, :]).astype(BF16)
    return dict(
        w_re=w_re, mu_re=mu_re, norm_mix=norm_mix[l][None, :],
        wuh=wuh, wul=(wu - wuh.astype(F32)).astype(BF16),
        au=au.astype(BF16),
        w0=w0[l][:, None, :], a0=a0[l][:, None, :],
        k_k=k_k[l][None, :], k_a=k_a[l][None, :], r_k=r_k[l].reshape(1, D_MODEL),
        seg=seg, expand=seg.T,
        lq1=lambda_q1[l][None, :], lk1=lambda_k1[l][None, :],
        lq2=lambda_q2[l][None, :], lk2=lambda_k2[l][None, :],
        subln_w=subln_w[l][None, :],
        ln_w=ln_x_w[l][None, :], ln_b=ln_x_b[l][None, :],
        g_up=g_lora_up[l].astype(BF16),
        proj_a=proj_a[l].astype(BF16), proj_b=proj_b[l].astype(BF16), w_out=w_out[l].astype(BF16),
        norm_mlp=norm_mlp[l][None, :],
        w1=w_mlp_in[l].astype(BF16), w2=w_mlp_out[l].astype(BF16),
    )


def kernel(x_prompt, x_sample, w_in, mu_shift, w0, w_lora_up, a0, a_lora_up, g_lora_up, k_k, k_a, r_k, ln_x_w, ln_x_b, lambda_q1, lambda_k1, lambda_q2, lambda_k2, subln_w, proj_a, proj_b, w_out, norm_mix, norm_mlp, w_mlp_in, w_mlp_out, norm_final):
    depth = w_in.shape[0]
    assert depth == 1, "the final norm is fused into the (single) layer's MLP kernel"
    prm = _prep_layer(0, w_in, mu_shift, w0, w_lora_up, a0, a_lora_up, g_lora_up, k_k, k_a, r_k, ln_x_w,
                      ln_x_b, lambda_q1, lambda_k1, lambda_q2, lambda_k2, subln_w, proj_a, proj_b,
                      w_out, norm_mix, norm_mlp, w_mlp_in, w_mlp_out)
    outs = []
    for x in (x_prompt, x_sample):
        h = _layer(x, 0, prm, norm_final)
        y = _mlp(h, prm["norm_mlp"], norm_final[None, :], prm["w1"], prm["w2"])
        outs.append(y.reshape(x.shape))
    return tuple(outs)
```

```python
import functools
import math

import jax
import jax.numpy as jnp
from jax import lax
from jax.experimental import pallas as pl
from jax.experimental.pallas import tpu as pltpu

F32 = jnp.float32
BF16 = jnp.bfloat16

D_MODEL = 1024
HEAD = 64
N_PAIR = D_MODEL // 128
DECAY_RANK = 64
ICLR_RANK = 64
GATE_RANK = 128
GN_EPS = 64e-5
ROPE_THETA = 500000.0
ROPE_DIM = 16
SUBLN_EPS = 1e-5
NORM_EPS = 1e-6
D_FF = 4 * D_MODEL
RWKV_MAIN = 3 * D_MODEL
RWKV_COLS = RWKV_MAIN + 2 * DECAY_RANK + 2 * ICLR_RANK + GATE_RANK
GROUP = 1024
N_GROUPS = 9
G_Q, G_KB, G_VB, G_GA, G_GB, G_LORA = 3, 4, 5, 6, 7, 8
GROUPS_PER_STEP = 3
LORA_BLK = G_LORA * GROUP // 128
CHUNK = 64
EXP_M05 = math.exp(-0.5)
LOG2_E = math.log2(math.e)
VMEM_LIMIT = 56 * 1024 * 1024

NN = (((1,), (0,)), ((), ()))
NT = (((1,), (1,)), ((), ()))
TN = (((0,), (0,)), ((), ()))


def _dot(a, b, dims=NN):
    return lax.dot_general(a, b, dims, preferred_element_type=F32)


def _split(x):
    hi = x.astype(BF16)
    lo = (x - hi.astype(F32)).astype(BF16)
    return hi, lo


def _dot3(a, b, dims=NN):
    ah, al = _split(a)
    bh, bl = _split(b)
    return _dot(ah, bh, dims) + (_dot(ah, bl, dims) + _dot(al, bh, dims))


def _dot2_exact_rhs(a, b_bf16, dims=NN):
    ah, al = _split(a)
    return _dot(ah, b_bf16, dims) + _dot(al, b_bf16, dims)


def _sigmoid(x):
    return 1.0 / (1.0 + jnp.exp(-x))


def _rms(x, g):
    return x * lax.rsqrt(jnp.mean(x * x, axis=-1, keepdims=True) + NORM_EPS) * g


def _inproj_kernel(x_ref, xp_ref, xn_ref, g_ref, w_ref, mu_ref, cos_ref, sa_ref, sb_ref,
                   o_ref, xs_ref, hs_ref, *, tm, tiles_per_seq):
    i = pl.program_id(0)
    j = pl.program_id(1)

    @pl.when(j == 0)
    def _():
        g = g_ref[...]
        xs_ref[...] = _rms(x_ref[...], g).astype(BF16)
        hs_ref[0:8, :] = _rms(xp_ref[...], g)
        hs_ref[8:16, :] = _rms(xn_ref[...], g)

    def store_shifted(p, w, cols):
        ph = _dot(hs_ref[...].astype(BF16), w)
        t_in_seq = i % tiles_per_seq
        prev = jnp.where(t_in_seq == 0, 0.0, ph[7:8, :])
        nxt = jnp.where(t_in_seq == tiles_per_seq - 1, 0.0, ph[8:9, :])
        c2 = 0.5 * mu_ref[:, cols]
        c1 = 1.0 - mu_ref[:, cols]
        core = p * c1 + (pltpu.roll(p, 1, 0) + pltpu.roll(p, tm - 1, 0)) * c2
        o_ref[:, cols] = core
        o_ref[0:1, cols] = core[0:1, :] + (prev - p[tm - 1:tm, :]) * c2
        o_ref[tm - 1:tm, cols] = core[tm - 1:tm, :] + (nxt - p[0:1, :]) * c2

    def rotated(p, scale):
        c = jnp.tile(cos_ref[...], (1, N_PAIR))
        sa = jnp.tile(sa_ref[...], (1, N_PAIR))
        sb = jnp.tile(sb_ref[...], (1, N_PAIR))
        half = ROPE_DIM // 2
        out = p * c + pltpu.roll(p, half, 1) * sa + pltpu.roll(p, GROUP - half, 1) * sb
        return out if scale == 1.0 else out * scale

    for step in range(N_GROUPS // GROUPS_PER_STEP):
        @pl.when(j == step)
        def _(step=step):
            for gi in range(GROUPS_PER_STEP):
                g = step * GROUPS_PER_STEP + gi
                cols = slice(gi * GROUP, (gi + 1) * GROUP)
                w = w_ref[:, cols]
                p = _dot(xs_ref[...], w)
                if g < 3 or g == G_LORA:
                    store_shifted(p, w, cols)
                elif g == G_Q:
                    o_ref[:, cols] = rotated(p, HEAD ** -0.5)
                elif g == G_KB:
                    o_ref[:, cols] = rotated(p, 1.0)
                else:
                    o_ref[:, cols] = p


def _inproj(x2d, seq, g, w_re, mu_re, cos_t, sa_t, sb_t):
    t_total = x2d.shape[0]
    tm = min(512, seq)
    tiles_per_seq = seq // tm
    n_tiles = t_total // tm
    last8 = t_total // 8 - 1
    kern = functools.partial(_inproj_kernel, tm=tm, tiles_per_seq=tiles_per_seq)
    return pl.pallas_call(
        kern,
        grid=(n_tiles, N_GROUPS // GROUPS_PER_STEP),
        in_specs=[
            pl.BlockSpec((tm, D_MODEL), lambda i, j: (i, 0)),
            pl.BlockSpec((8, D_MODEL), lambda i, j: (jnp.maximum(i * (tm // 8) - 1, 0), 0)),
            pl.BlockSpec((8, D_MODEL), lambda i, j: (jnp.minimum((i + 1) * (tm // 8), last8), 0)),
            pl.BlockSpec((1, D_MODEL), lambda i, j: (0, 0)),
            pl.BlockSpec((D_MODEL, GROUPS_PER_STEP * GROUP), lambda i, j: (0, j)),
            pl.BlockSpec((1, GROUPS_PER_STEP * GROUP), lambda i, j: (0, j)),
            pl.BlockSpec((tm, 128), lambda i, j: (i % tiles_per_seq, 0)),
            pl.BlockSpec((tm, 128), lambda i, j: (i % tiles_per_seq, 0)),
            pl.BlockSpec((tm, 128), lambda i, j: (i % tiles_per_seq, 0)),
        ],
        out_specs=pl.BlockSpec((tm, GROUPS_PER_STEP * GROUP), lambda i, j: (i, j)),
        out_shape=jax.ShapeDtypeStruct((t_total, N_GROUPS * GROUP), F32),
        scratch_shapes=[pltpu.VMEM((tm, D_MODEL), BF16), pltpu.VMEM((16, D_MODEL), F32)],
        compiler_params=pltpu.CompilerParams(
            dimension_semantics=("arbitrary", "arbitrary"), vmem_limit_bytes=VMEM_LIMIT),
        name="inproj",
    )(x2d, x2d, x2d, g, w_re, mu_re, cos_t, sa_t, sb_t)


QUAD = 256
N_QUAD = D_MODEL // QUAD
HEADS_PER_QUAD = QUAD // HEAD
SCAN_SUB = 8


def _scan_kernel(r_ref, k_ref, v_ref, lw_ref, la_ref, wuh_ref, wul_ref, au_ref,
                 w0_ref, a0_ref, kk_ref, ka_ref, rk_ref, seg_ref, y_ref, s_ref, h_ref, *, n_sub):
    d = pl.program_id(1)
    c = pl.program_id(2)
    C = CHUNK
    bf = lambda x: x.astype(BF16)

    @pl.when(c == 0)
    def _():
        h_ref[...] = jnp.zeros_like(h_ref)

    sgn = 1 - 2 * d
    rr = lax.broadcasted_iota(jnp.int32, (C, C), 0)
    cc = lax.broadcasted_iota(jnp.int32, (C, C), 1)
    tri = jnp.where((rr - cc) * sgn >= 0, 1.0, 0.0).astype(BF16)

    tr = lax.broadcasted_iota(jnp.int32, (C, QUAD), 0)
    tc = lax.broadcasted_iota(jnp.int32, (C, QUAD), 1) % C
    dd = (tr - tc) * sgn
    strict = dd > 0
    incl = dd >= 0
    eye = jnp.where(dd == 0, 1.0, 0.0)
    levels = []
    m = 1
    while m < C:
        levels.append(jnp.logical_and((tr // (2 * m)) == (tc // (2 * m)), (tr // m) != (tc // m)))
        m *= 2
    lane_head = lax.broadcasted_iota(jnp.int32, (1, QUAD), 1) // HEAD
    head_lanes = [lane_head == h for h in range(HEADS_PER_QUAD)]
    same_head = (lax.broadcasted_iota(jnp.int32, (QUAD, QUAD), 0) // HEAD) == (
        lax.broadcasted_iota(jnp.int32, (QUAD, QUAD), 1) // HEAD)
    bd_ones = jnp.where(same_head, 1.0, 0.0).astype(BF16)

    def blockdiag(x):
        return jnp.concatenate([jnp.where(hl, x, 0.0) for hl in head_lanes], axis=0)

    quads = range(N_QUAD)
    sls = [slice(QUAD * q, QUAD * (q + 1)) for q in quads]

    staged = []
    for i in range(n_sub):
        jj = i + d * (n_sub - 1 - 2 * i)
        rows = pl.ds(pl.multiple_of(jj * C, C), C)
        r = r_ref[rows, :]
        k = k_ref[rows, :]
        v = v_ref[rows, :]

        thh, thl = _split(jnp.tanh(lw_ref[rows, :]))
        w_raw = w0_ref[0] + (_dot(thh, wuh_ref[0]) + (_dot(thh, wul_ref[0]) + _dot(thl, wuh_ref[0])))
        logw = -EXP_M05 * _sigmoid(w_raw)
        a = _sigmoid(a0_ref[0] + _dot(bf(la_ref[rows, :]), au_ref[0]))

        l1 = logw.astype(BF16)
        e1 = logw - l1.astype(F32)
        l2 = e1.astype(BF16)
        l3 = (e1 - l2.astype(F32)).astype(BF16)
        cum = _dot(tri, l1) + (_dot(tri, l2) + _dot(tri, l3))
        total = jnp.sum(logw, axis=0, keepdims=True)
        g_incl = jnp.exp(cum)
        g_excl = jnp.exp(cum - logw)
        g_inv = jnp.exp(-cum)
        g_tail = jnp.exp(total - cum)
        g_tot = jnp.exp(total)

        kkv = k * kk_ref[...]
        kd = k * (1.0 + (a - 1.0) * ka_ref[...])
        sq = kkv * kkv

        s_ref[0, rows, :] = _dot(bf(r * kd * rk_ref[...]), seg_ref[...])

        ss = [_dot(bf(sq[:, sl]), bd_ones) for sl in sls]
        kkn = [kkv[:, sl] * lax.rsqrt(jnp.maximum(s, 1e-24)) for sl, s in zip(sls, ss)]
        bvec = [kn * a[:, sl] for sl, kn in zip(sls, kkn)]
        lhs = [bf(jnp.concatenate([-kn * g_excl[:, sl], r[:, sl] * g_incl[:, sl]], axis=0))
               for sl, kn in zip(sls, kkn)]
        rhs = [jnp.concatenate([blockdiag(bf(bv * g_inv[:, sl])), blockdiag(bf(kd[:, sl] * g_inv[:, sl]))], axis=0)
               for sl, bv in zip(sls, bvec)]
        bkh = [bf(jnp.concatenate([bv * g_tail[:, sl], kd[:, sl] * g_tail[:, sl]], axis=0))
               for sl, bv in zip(sls, bvec)]
        vb = [bf(v[:, sl]) for sl in sls]
        v_bd = [blockdiag(x) for x in vb]

        gm = [_dot(lhs[q], rhs[q], NT) for q in quads]
        a_ab = [jnp.where(strict, g[0:C, 0:QUAD], 0.0) for g in gm]
        a_ak = [bf(jnp.where(strict, g[0:C, QUAD:2 * QUAD], 0.0)) for g in gm]
        a_r = [bf(jnp.concatenate([jnp.where(incl, g[C:2 * C, 0:QUAD], 0.0),
                                   jnp.where(incl, g[C:2 * C, QUAD:2 * QUAD], 0.0)], axis=1)) for g in gm]
        akv = [_dot(a_ak[q], v_bd[q]) for q in quads]
        staged.append(dict(rows=rows, lhs=lhs, a_ab=a_ab, a_r=a_r, akv=akv, vb=vb, v_bd=v_bd, bkh=bkh,
                           g_tot=g_tot))

    combos = [(i, q) for i in range(n_sub) for q in quads]
    tinv = {(i, q): eye + jnp.where(levels[0], staged[i]["a_ab"][q], 0.0) for i, q in combos}
    for lvl in levels[1:]:
        tb = {key: bf(t) for key, t in tinv.items()}
        wm = {(i, q): _dot(bf(jnp.where(lvl, staged[i]["a_ab"][q], 0.0)), blockdiag(tb[i, q])) for i, q in combos}
        tinv = {key: tinv[key] + _dot(tb[key], blockdiag(bf(wm[key]))) for key in combos}

    ht = [h_ref[q] for q in quads]
    for i in range(n_sub):
        st = staged[i]
        ar = [_dot(st["lhs"][q], bf(ht[q]), NT) for q in quads]
        rhs_u = [ar[q][0:C] + st["akv"][q] for q in quads]
        u = [bf(_dot(bf(tinv[i, q]), blockdiag(bf(rhs_u[q])))) for q in quads]
        uv_bd = [jnp.concatenate([blockdiag(u[q]), st["v_bd"][q]], axis=0) for q in quads]
        for q in quads:
            y_ref[0, st["rows"], sls[q]] = ar[q][C:2 * C] + _dot(st["a_r"][q], uv_bd[q])
        upd = [_dot(jnp.concatenate([u[q], st["vb"][q]], axis=0), st["bkh"][q], TN) for q in quads]
        ht = [ht[q] * st["g_tot"][:, sls[q]] + jnp.where(same_head, upd[q], 0.0) for q in quads]

    for q in quads:
        h_ref[q] = ht[q]


def _wkv_scan(proj, batch, seq, wuh, wul, au, w0, a0, k_k, k_a, r_k, seg):
    t_total = batch * seq
    n_sub = math.gcd(seq // CHUNK, SCAN_SUB)
    blk = n_sub * CHUNK
    nc = seq // blk

    def rowblk(b, d, c):
        return b * nc + c + d * (nc - 1 - 2 * c)

    def col(jblk):
        return lambda b, d, c: (rowblk(b, d, c), jblk)

    full = lambda b, d, c: (0, 0)
    perdir = lambda b, d, c: (d, 0, 0)
    return pl.pallas_call(
        functools.partial(_scan_kernel, n_sub=n_sub),
        grid=(batch, 2, nc),
        in_specs=[
            pl.BlockSpec((blk, GROUP), col(0)),
            pl.BlockSpec((blk, GROUP), col(1)),
            pl.BlockSpec((blk, GROUP), col(2)),
            pl.BlockSpec((blk, 128), col(LORA_BLK)),
            pl.BlockSpec((blk, 128), col(LORA_BLK + 1)),
            pl.BlockSpec((1, 128, D_MODEL), perdir),
            pl.BlockSpec((1, 128, D_MODEL), perdir),
            pl.BlockSpec((1, 128, D_MODEL), perdir),
            pl.BlockSpec((1, 1, D_MODEL), perdir),
            pl.BlockSpec((1, 1, D_MODEL), perdir),
            pl.BlockSpec((1, D_MODEL), full),
            pl.BlockSpec((1, D_MODEL), full),
            pl.BlockSpec((1, D_MODEL), full),
            pl.BlockSpec((D_MODEL, 128), full),
        ],
        out_specs=[
            pl.BlockSpec((1, blk, D_MODEL), lambda b, d, c: (d, rowblk(b, d, c), 0)),
            pl.BlockSpec((1, blk, 128), lambda b, d, c: (d, rowblk(b, d, c), 0)),
        ],
        out_shape=[
            jax.ShapeDtypeStruct((2, t_total, D_MODEL), F32),
            jax.ShapeDtypeStruct((2, t_total, 128), F32),
        ],
        scratch_shapes=[pltpu.VMEM((N_QUAD, QUAD, QUAD), F32)],
        compiler_params=pltpu.CompilerParams(
            dimension_semantics=("arbitrary", "arbitrary", "arbitrary"), vmem_limit_bytes=VMEM_LIMIT),
        name="wkv_scan",
    )(proj, proj, proj, proj, proj, wuh, wul, au, w0, a0, k_k, k_a, r_k, seg)


V_ROWS = 128 + 16
PV_GROUP = 2


def _attn_kernel(q_ref, k_ref, v_ref, lq1_ref, lk1_ref, lq2_ref, lk2_ref, sw_ref, o_ref,
                 k0_ref, k1_ref, vt_ref, s_ref, m_ref, p_ref, *, seq, tq, tk, per_trip, nq, lam_init):
    i = pl.program_id(2)
    n_kc = seq // tk
    lane = lax.broadcasted_iota(jnp.int32, (1, 128), 1)
    first = lane < HEAD

    @pl.when(i == 0)
    def _():
        def prep(kc, carry):
            rows = pl.ds(pl.multiple_of(kc * tk, tk), tk)
            kf = k_ref[rows, :]
            k0_ref[rows, :] = jnp.where(first, kf, 0.0).astype(BF16)
            k1_ref[rows, :] = jnp.where(first, 0.0, kf).astype(BF16)
            vt_ref[0:128, rows] = v_ref[rows, :].T.astype(BF16)
            vt_ref[128:V_ROWS, rows] = jnp.ones((V_ROWS - 128, tk), BF16)
            return carry
        lax.fori_loop(0, n_kc, prep, 0)

    def rows_of(kc):
        return pl.ds(pl.multiple_of(kc * tk, tk), tk)

    def fold(op, s):
        return op(s.reshape(tk // 8, 8, tq), axis=0)

    def query_t():
        return (q_ref[...] * LOG2_E).T.astype(BF16)

    neg = jnp.full((8, tq), -jnp.inf, F32)
    za = jnp.zeros((V_ROWS, tq), F32)
    n_trips = n_kc // per_trip

    def finish(acc0, acc1):
        l0 = acc0[128:129, :]
        l1 = acc1[128:129, :]
        lam = (jnp.exp(jnp.sum(lq1_ref[...] * lk1_ref[...], axis=-1, keepdims=True))
               - jnp.exp(jnp.sum(lq2_ref[...] * lk2_ref[...], axis=-1, keepdims=True)) + lam_init)
        o_t = acc0[0:128, :] / l0 - lam * (acc1[0:128, :] / l1)
        o = o_t.T
        o = o * lax.rsqrt(jnp.mean(o * o, axis=-1, keepdims=True) + SUBLN_EPS) * sw_ref[...]
        o_ref[...] = o * (1.0 - lam_init)

    def run(slot_a):
        slot_b = 1 - slot_a
        mine = (i % 2) == slot_a

        def trip(t, q_t, mx, m_prev, acc):
            for c in range(per_trip):
                rows = rows_of(t * per_trip + c)
                if q_t is not None:
                    s0 = _dot(k0_ref[rows, :], q_t)
                    s1 = _dot(k1_ref[rows, :], q_t)
                    s_ref[slot_a, 0, rows, :] = s0
                    s_ref[slot_a, 1, rows, :] = s1
                    mx = (jnp.maximum(mx[0], fold(jnp.max, s0)), jnp.maximum(mx[1], fold(jnp.max, s1)))
                if acc is not None:
                    for comp in range(2):
                        x = s_ref[slot_b, comp, rows, :] - m_prev[comp]
                        p_ref[comp, c * tk:(c + 1) * tk, :] = jnp.exp2(x.astype(BF16))
                    if (c + 1) % PV_GROUP == 0 or c + 1 == per_trip:
                        c0 = (c // PV_GROUP) * PV_GROUP
                        span = (c + 1 - c0) * tk
                        start = (t * per_trip + c0) * tk
                        vt = vt_ref[:, pl.ds(pl.multiple_of(start, tk), span)]
                        acc = tuple(acc[comp] + _dot(vt, p_ref[comp, c0 * tk:(c + 1) * tk, :])
                                    for comp in range(2))
            return mx, acc

        def loop(body, init):
            if n_trips == 1:
                return body(0, init)
            return lax.fori_loop(0, n_trips, body, init)

        def prev_max():
            return (jnp.max(m_ref[slot_b, 0], axis=0, keepdims=True),
                    jnp.max(m_ref[slot_b, 1], axis=0, keepdims=True))

        def store_max(mx):
            m_ref[slot_a, 0] = mx[0]
            m_ref[slot_a, 1] = mx[1]

        if slot_a == 0:
            @pl.when(i == 0)
            def _():
                q_t = query_t()
                store_max(loop(lambda t, mx: trip(t, q_t, mx, None, None)[0], (neg, neg)))

        @pl.when(jnp.logical_and(mine, jnp.logical_and(i > 0, i < nq)))
        def _():
            q_t = query_t()
            m_prev = prev_max()
            mx, acc = loop(lambda t, c: trip(t, q_t, c[0], m_prev, c[1]), ((neg, neg), (za, za)))
            store_max(mx)
            finish(*acc)

        if nq % 2 == slot_a:
            @pl.when(i == nq)
            def _():
                m_prev = prev_max()
                finish(*loop(lambda t, acc: trip(t, None, None, m_prev, acc)[1], (za, za)))

    run(0)
    run(1)


def _diff_attn(proj, batch, seq, lq1, lk1, lq2, lk2, subln_w, lam_init):
    t_total = batch * seq
    tq = min(512, seq)
    tk = min(256, seq)
    per_trip = math.gcd(seq // tk, 16)
    nq = seq // tq
    n_heads = N_PAIR
    qb, kb, vb = (G_Q * GROUP // 128, G_KB * GROUP // 128, G_VB * GROUP // 128)
    kern = functools.partial(_attn_kernel, seq=seq, tq=tq, tk=tk, per_trip=per_trip, nq=nq, lam_init=lam_init)
    small = lambda b, h, i: (0, 0)
    return pl.pallas_call(
        kern,
        grid=(batch, n_heads, nq + 1),
        in_specs=[
            pl.BlockSpec((tq, 128), lambda b, h, i: (b * nq + jnp.minimum(i, nq - 1), qb + h)),
            pl.BlockSpec((seq, 128), lambda b, h, i: (b, kb + h)),
            pl.BlockSpec((seq, 128), lambda b, h, i: (b, vb + h)),
            pl.BlockSpec((1, HEAD), small),
            pl.BlockSpec((1, HEAD), small),
            pl.BlockSpec((1, HEAD), small),
            pl.BlockSpec((1, HEAD), small),
            pl.BlockSpec((1, 128), small),
        ],
        out_specs=pl.BlockSpec((tq, 128), lambda b, h, i: (b * nq + jnp.maximum(i - 1, 0), h)),
        out_shape=jax.ShapeDtypeStruct((t_total, D_MODEL), F32),
        scratch_shapes=[pltpu.VMEM((seq, 128), BF16), pltpu.VMEM((seq, 128), BF16),
                        pltpu.VMEM((V_ROWS, seq), BF16), pltpu.VMEM((2, 2, seq, tq), F32),
                        pltpu.VMEM((2, 2, 8, tq), F32), pltpu.VMEM((2, per_trip * tk, tq), BF16)],
        compiler_params=pltpu.CompilerParams(
            dimension_semantics=("arbitrary", "arbitrary", "arbitrary"), vmem_limit_bytes=VMEM_LIMIT),
        name="diff_attn",
    )(proj, proj, proj, lq1, lk1, lq2, lk2, subln_w)


def _mix_kernel(x_ref, yf_ref, yb_ref, sf_ref, sb_ref, v_ref, lg_ref, ga_ref, gb_ref, o_ref,
                lnw_ref, lnb_ref, gup_ref, exp_ref, pa_ref, pb_ref, wo_ref, h_ref, ya_ref):
    y = yf_ref[0] + yb_ref[0]
    rh = lax.broadcasted_iota(jnp.int32, (QUAD, QUAD), 0) // HEAD
    ch = lax.broadcasted_iota(jnp.int32, (QUAD, QUAD), 1) // HEAD
    avg = jnp.where(rh == ch, 1.0 / HEAD, 0.0).astype(BF16)
    for q in range(N_QUAD):
        sl = slice(QUAD * q, QUAD * (q + 1))
        yq = y[:, sl]
        cen = yq - _dot(yq.astype(BF16), avg)
        var = _dot((cen * cen).astype(BF16), avg)
        ya_ref[:, sl] = cen * lax.rsqrt(var + GN_EPS)
    coef = _dot((sf_ref[0] + sb_ref[0]).astype(BF16), exp_ref[...])
    gate = _dot(_sigmoid(lg_ref[...]).astype(BF16), gup_ref[...])
    y_a = (ya_ref[...] * lnw_ref[...] + lnb_ref[...] + coef * v_ref[...]) * gate
    pa = _dot(y_a.astype(BF16), pa_ref[...])
    pb = _dot(o_ref[...].astype(BF16), pb_ref[...])
    merged = _sigmoid(ga_ref[...]) * pa + _sigmoid(gb_ref[...]) * pb
    h_ref[...] = x_ref[...] + _dot(merged.astype(BF16), wo_ref[...])


def _mix(x2d, proj, y_dir, s_dir, o_attn, ln_w, ln_b, g_up, expand, proj_a, proj_b, w_out):
    t_total = x2d.shape[0]
    tm = 256
    row = lambda i: (i, 0)
    full = lambda i: (0, 0)
    wspec = pl.BlockSpec((D_MODEL, D_MODEL), full)
    return pl.pallas_call(
        _mix_kernel,
        grid=(t_total // tm,),
        in_specs=[
            pl.BlockSpec((tm, D_MODEL), row),
            pl.BlockSpec((1, tm, D_MODEL), lambda i: (0, i, 0)),
            pl.BlockSpec((1, tm, D_MODEL), lambda i: (1, i, 0)),
            pl.BlockSpec((1, tm, 128), lambda i: (0, i, 0)),
            pl.BlockSpec((1, tm, 128), lambda i: (1, i, 0)),
            pl.BlockSpec((tm, GROUP), lambda i: (i, 2)),
            pl.BlockSpec((tm, 128), lambda i: (i, LORA_BLK + 2)),
            pl.BlockSpec((tm, GROUP), lambda i: (i, G_GA)),
            pl.BlockSpec((tm, GROUP), lambda i: (i, G_GB)),
            pl.BlockSpec((tm, D_MODEL), row),
            pl.BlockSpec((1, D_MODEL), full),
            pl.BlockSpec((1, D_MODEL), full),
            pl.BlockSpec((GATE_RANK, D_MODEL), full),
            pl.BlockSpec((128, D_MODEL), full),
            wspec, wspec, wspec,
        ],
        out_specs=pl.BlockSpec((tm, D_MODEL), row),
        out_shape=jax.ShapeDtypeStruct((t_total, D_MODEL), F32),
        scratch_shapes=[pltpu.VMEM((tm, D_MODEL), F32)],
        compiler_params=pltpu.CompilerParams(
            dimension_semantics=("arbitrary",), vmem_limit_bytes=VMEM_LIMIT),
        name="mix",
    )(x2d, y_dir, y_dir, s_dir, s_dir, proj, proj, proj, proj, o_attn,
      ln_w, ln_b, g_up, expand, proj_a, proj_b, w_out)


def _mlp_kernel(h_ref, gm_ref, gf_ref, w1_ref, w2_ref, o_ref, *, ff_chunk):
    h = h_ref[...]
    xn = _rms(h, gm_ref[...]).astype(BF16)
    acc = h
    for c in range(D_FF // ff_chunk):
        sl = slice(c * ff_chunk, (c + 1) * ff_chunk)
        hid = jnp.maximum(_dot(xn, w1_ref[:, sl]), 0.0)
        acc = acc + _dot((hid * hid).astype(BF16), w2_ref[sl, :])
    o_ref[...] = _rms(acc, gf_ref[...])


def _mlp(h2d, norm_mlp, norm_final, w1, w2):
    t_total = h2d.shape[0]
    tm = 256
    row = lambda i: (i, 0)
    full = lambda i: (0, 0)
    kern = functools.partial(_mlp_kernel, ff_chunk=1024)
    return pl.pallas_call(
        kern,
        grid=(t_total // tm,),
        in_specs=[
            pl.BlockSpec((tm, D_MODEL), row),
            pl.BlockSpec((1, D_MODEL), full),
            pl.BlockSpec((1, D_MODEL), full),
            pl.BlockSpec((D_MODEL, D_FF), full),
            pl.BlockSpec((D_FF, D_MODEL), full),
        ],
        out_specs=pl.BlockSpec((tm, D_MODEL), row),
        out_shape=jax.ShapeDtypeStruct((t_total, D_MODEL), F32),
        compiler_params=pltpu.CompilerParams(
            dimension_semantics=("arbitrary",), vmem_limit_bytes=VMEM_LIMIT),
        name="mlp",
    )(h2d, norm_mlp, norm_final, w1, w2)


def _rope_tables(seq):
    half = ROPE_DIM // 2
    pos = jnp.arange(seq, dtype=F32)
    inv_freq = ROPE_THETA ** (-jnp.arange(0, ROPE_DIM, 2, dtype=F32) / ROPE_DIM)
    ang = pos[:, None] * inv_freq[None, :]
    cos, sin = jnp.cos(ang), jnp.sin(ang)
    pad = jnp.zeros((seq, HEAD - ROPE_DIM), F32)
    z8 = jnp.zeros((seq, half), F32)
    c64 = jnp.concatenate([cos, cos, pad + 1.0], axis=1)
    sa64 = jnp.concatenate([z8, sin, pad], axis=1)
    sb64 = jnp.concatenate([-sin, z8, pad], axis=1)
    two = lambda t: jnp.concatenate([t, t], axis=1)
    return two(c64), two(sa64), two(sb64)


def _layer(x, l, prm, norm_final):
    batch, seq, _ = x.shape
    x2d = x.reshape(batch * seq, D_MODEL)
    cos_t, sa_t, sb_t = _rope_tables(seq)
    proj = _inproj(x2d, seq, prm["norm_mix"], prm["w_re"], prm["mu_re"], cos_t, sa_t, sb_t)
    y_dir, s_dir = _wkv_scan(proj, batch, seq, prm["wuh"], prm["wul"], prm["au"],
                             prm["w0"], prm["a0"], prm["k_k"], prm["k_a"], prm["r_k"], prm["seg"])
    lam_init = 0.8 - 0.6 * math.exp(-0.3 * l)
    o_attn = _diff_attn(proj, batch, seq, prm["lq1"], prm["lk1"], prm["lq2"], prm["lk2"],
                        prm["subln_w"], lam_init)
    h = _mix(x2d, proj, y_dir, s_dir, o_attn, prm["ln_w"], prm["ln_b"], prm["g_up"], prm["expand"],
             prm["proj_a"], prm["proj_b"], prm["w_out"])
    return h


def _prep_layer(l, w_in, mu_shift, w0, w_lora_up, a0, a_lora_up, g_lora_up, k_k, k_a, r_k, ln_x_w,
                ln_x_b, lambda_q1, lambda_k1, lambda_q2, lambda_k2, subln_w, proj_a, proj_b, w_out,
                norm_mix, norm_mlp, w_mlp_in, w_mlp_out):
    w = w_in[l]
    pad_cols = N_GROUPS * GROUP - w.shape[1]
    w_re = jnp.concatenate([w[:, :RWKV_MAIN], w[:, RWKV_COLS:], w[:, RWKV_MAIN:RWKV_COLS],
                            jnp.zeros((D_MODEL, pad_cols), F32)], axis=1).astype(BF16)
    mu = mu_shift[l]
    mu_re = jnp.concatenate([mu[:RWKV_MAIN], jnp.zeros((5 * GROUP,), F32), mu[RWKV_MAIN:],
                             jnp.zeros((pad_cols,), F32)])[None, :]

    def lora_pad(up):
        z = jnp.zeros_like(up[0])
        return jnp.stack([jnp.concatenate([up[0], z], axis=0), jnp.concatenate([z, up[1]], axis=0)])

    wu = lora_pad(w_lora_up[l])
    au = lora_pad(a_lora_up[l])
    wuh = wu.astype(BF16)
    head_of_lane = jnp.arange(D_MODEL) // HEAD
    seg = (head_of_lane[:, None] == jnp.arange(128)[None, :]).astype(BF16)
    return dict(
        w_re=w_re, mu_re=mu_re, norm_mix=norm_mix[l][None, :],
        wuh=wuh, wul=(wu - wuh.astype(F32)).astype(BF16),
        au=au.astype(BF16),
        w0=w0[l][:, None, :], a0=a0[l][:, None, :],
        k_k=k_k[l][None, :], k_a=k_a[l][None, :], r_k=r_k[l].reshape(1, D_MODEL),
        seg=seg, expand=seg.T,
        lq1=lambda_q1[l][None, :], lk1=lambda_k1[l][None, :],
        lq2=lambda_q2[l][None, :], lk2=lambda_k2[l][None, :],
        subln_w=subln_w[l][None, :],
        ln_w=ln_x_w[l][None, :], ln_b=ln_x_b[l][None, :],
        g_up=g_lora_up[l].astype(BF16),
        proj_a=proj_a[l].astype(BF16), proj_b=proj_b[l].astype(BF16), w_out=w_out[l].astype(BF16),
        norm_mlp=norm_mlp[l][None, :],
        w1=w_mlp_in[l].astype(BF16), w2=w_mlp_out[l].astype(BF16),
    )


def kernel(x_prompt, x_sample, w_in, mu_shift, w0, w_lora_up, a0, a_lora_up, g_lora_up, k_k, k_a, r_k, ln_x_w, ln_x_b, lambda_q1, lambda_k1, lambda_q2, lambda_k2, subln_w, proj_a, proj_b, w_out, norm_mix, norm_mlp, w_mlp_in, w_mlp_out, norm_final):
    depth = w_in.shape[0]
    assert depth == 1, "the final norm is fused into the (single) layer's MLP kernel"
    prm = _prep_layer(0, w_in, mu_shift, w0, w_lora_up, a0, a_lora_up, g_lora_up, k_k, k_a, r_k, ln_x_w,
                      ln_x_b, lambda_q1, lambda_k1, lambda_q2, lambda_k2, subln_w, proj_a, proj_b,
                      w_out, norm_mix, norm_mlp, w_mlp_in, w_mlp_out)
    outs = []
    for x in (x_prompt, x_sample):
        h = _layer(x, 0, prm, norm_final)
        y = _mlp(h, prm["norm_mlp"], norm_final[None, :], prm["w1"], prm["w2"])
        outs.append(y.reshape(x.shape))
    return tuple(outs)
```

```python
import functools
import math

import jax
import jax.numpy as jnp
from jax import lax
from jax.experimental import pallas as pl
from jax.experimental.pallas import tpu as pltpu

F32 = jnp.float32
BF16 = jnp.bfloat16

D_MODEL = 1024
HEAD = 64
N_PAIR = D_MODEL // 128
DECAY_RANK = 64
ICLR_RANK = 64
GATE_RANK = 128
GN_EPS = 64e-5
ROPE_THETA = 500000.0
ROPE_DIM = 16
SUBLN_EPS = 1e-5
NORM_EPS = 1e-6
D_FF = 4 * D_MODEL
RWKV_MAIN = 3 * D_MODEL
RWKV_COLS = RWKV_MAIN + 2 * DECAY_RANK + 2 * ICLR_RANK + GATE_RANK
GROUP = 1024
N_GROUPS = 9
G_Q, G_KB, G_VB, G_GA, G_GB, G_LORA = 3, 4, 5, 6, 7, 8
GROUPS_PER_STEP = 3
LORA_BLK = G_LORA * GROUP // 128
CHUNK = 64
EXP_M05 = math.exp(-0.5)
LOG2_E = math.log2(math.e)
VMEM_LIMIT = 56 * 1024 * 1024

NN = (((1,), (0,)), ((), ()))
NT = (((1,), (1,)), ((), ()))
TN = (((0,), (0,)), ((), ()))


def _dot(a, b, dims=NN):
    return lax.dot_general(a, b, dims, preferred_element_type=F32)


def _split(x):
    hi = x.astype(BF16)
    lo = (x - hi.astype(F32)).astype(BF16)
    return hi, lo


def _dot3(a, b, dims=NN):
    ah, al = _split(a)
    bh, bl = _split(b)
    return _dot(ah, bh, dims) + (_dot(ah, bl, dims) + _dot(al, bh, dims))


def _dot2_exact_rhs(a, b_bf16, dims=NN):
    ah, al = _split(a)
    return _dot(ah, b_bf16, dims) + _dot(al, b_bf16, dims)


def _sigmoid(x):
    return 1.0 / (1.0 + jnp.exp(-x))


def _rms(x, g):
    return x * lax.rsqrt(jnp.mean(x * x, axis=-1, keepdims=True) + NORM_EPS) * g


def _inproj_kernel(x_ref, xp_ref, xn_ref, g_ref, w_ref, mu_ref, cos_ref, sa_ref, sb_ref,
                   o_ref, xs_ref, hs_ref, *, tm, tiles_per_seq):
    j = pl.program_id(0)
    i = pl.program_id(1)
    g = g_ref[...]
    xs_ref[...] = _rms(x_ref[...], g).astype(BF16)
    hs_ref[0:8, :] = _rms(xp_ref[...], g)
    hs_ref[8:16, :] = _rms(xn_ref[...], g)

    def store_shifted(p, w, cols):
        ph = _dot(hs_ref[...].astype(BF16), w)
        t_in_seq = i % tiles_per_seq
        prev = jnp.where(t_in_seq == 0, 0.0, ph[7:8, :])
        nxt = jnp.where(t_in_seq == tiles_per_seq - 1, 0.0, ph[8:9, :])
        c2 = 0.5 * mu_ref[:, cols]
        c1 = 1.0 - mu_ref[:, cols]
        core = p * c1 + (pltpu.roll(p, 1, 0) + pltpu.roll(p, tm - 1, 0)) * c2
        o_ref[:, cols] = core
        o_ref[0:1, cols] = core[0:1, :] + (prev - p[tm - 1:tm, :]) * c2
        o_ref[tm - 1:tm, cols] = core[tm - 1:tm, :] + (nxt - p[0:1, :]) * c2

    def rotated(p, scale):
        c = jnp.tile(cos_ref[...], (1, N_PAIR))
        sa = jnp.tile(sa_ref[...], (1, N_PAIR))
        sb = jnp.tile(sb_ref[...], (1, N_PAIR))
        half = ROPE_DIM // 2
        out = p * c + pltpu.roll(p, half, 1) * sa + pltpu.roll(p, GROUP - half, 1) * sb
        return out if scale == 1.0 else out * scale

    for step in range(N_GROUPS // GROUPS_PER_STEP):
        @pl.when(j == step)
        def _(step=step):
            for gi in range(GROUPS_PER_STEP):
                g = step * GROUPS_PER_STEP + gi
                cols = slice(gi * GROUP, (gi + 1) * GROUP)
                w = w_ref[:, cols]
                p = _dot(xs_ref[...], w)
                if g < 3 or g == G_LORA:
                    store_shifted(p, w, cols)
                elif g == G_Q:
                    o_ref[:, cols] = rotated(p, HEAD ** -0.5)
                elif g == G_KB:
                    o_ref[:, cols] = rotated(p, 1.0)
                else:
                    o_ref[:, cols] = p


def _inproj(x2d, seq, g, w_re, mu_re, cos_t, sa_t, sb_t):
    t_total = x2d.shape[0]
    tm = min(512, seq)
    tiles_per_seq = seq // tm
    n_tiles = t_total // tm
    last8 = t_total // 8 - 1
    kern = functools.partial(_inproj_kernel, tm=tm, tiles_per_seq=tiles_per_seq)
    return pl.pallas_call(
        kern,
        grid=(N_GROUPS // GROUPS_PER_STEP, n_tiles),
        in_specs=[
            pl.BlockSpec((tm, D_MODEL), lambda j, i: (i, 0)),
            pl.BlockSpec((8, D_MODEL), lambda j, i: (jnp.maximum(i * (tm // 8) - 1, 0), 0)),
            pl.BlockSpec((8, D_MODEL), lambda j, i: (jnp.minimum((i + 1) * (tm // 8), last8), 0)),
            pl.BlockSpec((1, D_MODEL), lambda j, i: (0, 0)),
            pl.BlockSpec((D_MODEL, GROUPS_PER_STEP * GROUP), lambda j, i: (0, j)),
            pl.BlockSpec((1, GROUPS_PER_STEP * GROUP), lambda j, i: (0, j)),
            pl.BlockSpec((tm, 128), lambda j, i: (i % tiles_per_seq, 0)),
            pl.BlockSpec((tm, 128), lambda j, i: (i % tiles_per_seq, 0)),
            pl.BlockSpec((tm, 128), lambda j, i: (i % tiles_per_seq, 0)),
        ],
        out_specs=pl.BlockSpec((tm, GROUPS_PER_STEP * GROUP), lambda j, i: (i, j)),
        out_shape=jax.ShapeDtypeStruct((t_total, N_GROUPS * GROUP), F32),
        scratch_shapes=[pltpu.VMEM((tm, D_MODEL), BF16), pltpu.VMEM((16, D_MODEL), F32)],
        compiler_params=pltpu.CompilerParams(
            dimension_semantics=("arbitrary", "arbitrary"), vmem_limit_bytes=VMEM_LIMIT),
        name="inproj",
    )(x2d, x2d, x2d, g, w_re, mu_re, cos_t, sa_t, sb_t)


QUAD = 256
N_QUAD = D_MODEL // QUAD
HEADS_PER_QUAD = QUAD // HEAD
SCAN_SUB = 8


def _scan_kernel(r_ref, k_ref, v_ref, lw_ref, la_ref, wuh_ref, wul_ref, au_ref,
                 w0_ref, a0_ref, kk_ref, ka_ref, rk_ref, seg_ref, y_ref, s_ref, h_ref, *, n_sub):
    d = pl.program_id(1)
    c = pl.program_id(2)
    C = CHUNK
    bf = lambda x: x.astype(BF16)

    @pl.when(c == 0)
    def _():
        h_ref[...] = jnp.zeros_like(h_ref)

    sgn = 1 - 2 * d
    rr = lax.broadcasted_iota(jnp.int32, (C, C), 0)
    cc = lax.broadcasted_iota(jnp.int32, (C, C), 1)
    tri = jnp.where((rr - cc) * sgn >= 0, 1.0, 0.0).astype(BF16)

    tr = lax.broadcasted_iota(jnp.int32, (C, QUAD), 0)
    tc = lax.broadcasted_iota(jnp.int32, (C, QUAD), 1) % C
    dd = (tr - tc) * sgn
    strict = dd > 0
    incl = dd >= 0
    eye = jnp.where(dd == 0, 1.0, 0.0)
    levels = []
    m = 1
    while m < C:
        levels.append(jnp.logical_and((tr // (2 * m)) == (tc // (2 * m)), (tr // m) != (tc // m)))
        m *= 2
    lane_head = lax.broadcasted_iota(jnp.int32, (1, QUAD), 1) // HEAD
    head_lanes = [lane_head == h for h in range(HEADS_PER_QUAD)]
    same_head = (lax.broadcasted_iota(jnp.int32, (QUAD, QUAD), 0) // HEAD) == (
        lax.broadcasted_iota(jnp.int32, (QUAD, QUAD), 1) // HEAD)
    bd_ones = jnp.where(same_head, 1.0, 0.0).astype(BF16)

    def blockdiag(x):
        return jnp.concatenate([jnp.where(hl, x, 0.0) for hl in head_lanes], axis=0)

    quads = range(N_QUAD)
    sls = [slice(QUAD * q, QUAD * (q + 1)) for q in quads]

    staged = []
    for i in range(n_sub):
        jj = i + d * (n_sub - 1 - 2 * i)
        rows = pl.ds(pl.multiple_of(jj * C, C), C)
        r = r_ref[rows, :]
        k = k_ref[rows, :]
        v = v_ref[rows, :]

        thh, thl = _split(jnp.tanh(lw_ref[rows, :]))
        w_raw = w0_ref[0] + (_dot(thh, wuh_ref[0]) + (_dot(thh, wul_ref[0]) + _dot(thl, wuh_ref[0])))
        logw = -EXP_M05 * _sigmoid(w_raw)
        a = _sigmoid(a0_ref[0] + _dot(bf(la_ref[rows, :]), au_ref[0]))

        l1 = logw.astype(BF16)
        e1 = logw - l1.astype(F32)
        l2 = e1.astype(BF16)
        l3 = (e1 - l2.astype(F32)).astype(BF16)
        cum = _dot(tri, l1) + (_dot(tri, l2) + _dot(tri, l3))
        total = jnp.sum(logw, axis=0, keepdims=True)
        g_incl = jnp.exp(cum)
        g_excl = jnp.exp(cum - logw)
        g_inv = jnp.exp(-cum)
        g_tail = jnp.exp(total - cum)
        g_tot = jnp.exp(total)

        kkv = k * kk_ref[...]
        kd = k * (1.0 + (a - 1.0) * ka_ref[...])
        sq = kkv * kkv

        s_ref[0, rows, :] = _dot(bf(r * kd * rk_ref[...]), seg_ref[...])

        ss = [_dot(bf(sq[:, sl]), bd_ones) for sl in sls]
        kkn = [kkv[:, sl] * lax.rsqrt(jnp.maximum(s, 1e-24)) for sl, s in zip(sls, ss)]
        bvec = [kn * a[:, sl] for sl, kn in zip(sls, kkn)]
        lhs = [bf(jnp.concatenate([-kn * g_excl[:, sl], r[:, sl] * g_incl[:, sl]], axis=0))
               for sl, kn in zip(sls, kkn)]
        rhs = [jnp.concatenate([blockdiag(bf(bv * g_inv[:, sl])), blockdiag(bf(kd[:, sl] * g_inv[:, sl]))], axis=0)
               for sl, bv in zip(sls, bvec)]
        bkh = [bf(jnp.concatenate([bv * g_tail[:, sl], kd[:, sl] * g_tail[:, sl]], axis=0))
               for sl, bv in zip(sls, bvec)]
        vb = [bf(v[:, sl]) for sl in sls]
        v_bd = [blockdiag(x) for x in vb]

        gm = [_dot(lhs[q], rhs[q], NT) for q in quads]
        a_ab = [jnp.where(strict, g[0:C, 0:QUAD], 0.0) for g in gm]
        a_ak = [bf(jnp.where(strict, g[0:C, QUAD:2 * QUAD], 0.0)) for g in gm]
        a_r = [bf(jnp.concatenate([jnp.where(incl, g[C:2 * C, 0:QUAD], 0.0),
                                   jnp.where(incl, g[C:2 * C, QUAD:2 * QUAD], 0.0)], axis=1)) for g in gm]
        akv = [_dot(a_ak[q], v_bd[q]) for q in quads]
        staged.append(dict(rows=rows, lhs=lhs, a_ab=a_ab, a_r=a_r, akv=akv, vb=vb, v_bd=v_bd, bkh=bkh,
                           g_tot=g_tot))

    combos = [(i, q) for i in range(n_sub) for q in quads]
    tinv = {(i, q): eye + jnp.where(levels[0], staged[i]["a_ab"][q], 0.0) for i, q in combos}
    for lvl in levels[1:]:
        tb = {key: bf(t) for key, t in tinv.items()}
        wm = {(i, q): _dot(bf(jnp.where(lvl, staged[i]["a_ab"][q], 0.0)), blockdiag(tb[i, q])) for i, q in combos}
        tinv = {key: tinv[key] + _dot(tb[key], blockdiag(bf(wm[key]))) for key in combos}

    ht = [h_ref[q] for q in quads]
    for i in range(n_sub):
        st = staged[i]
        ar = [_dot(st["lhs"][q], bf(ht[q]), NT) for q in quads]
        rhs_u = [ar[q][0:C] + st["akv"][q] for q in quads]
        u = [bf(_dot(bf(tinv[i, q]), blockdiag(bf(rhs_u[q])))) for q in quads]
        uv_bd = [jnp.concatenate([blockdiag(u[q]), st["v_bd"][q]], axis=0) for q in quads]
        for q in quads:
            y_ref[0, st["rows"], sls[q]] = (ar[q][C:2 * C] + _dot(st["a_r"][q], uv_bd[q])).astype(y_ref.dtype)
        upd = [_dot(jnp.concatenate([u[q], st["vb"][q]], axis=0), st["bkh"][q], TN) for q in quads]
        ht = [ht[q] * st["g_tot"][:, sls[q]] + jnp.where(same_head, upd[q], 0.0) for q in quads]

    for q in quads:
        h_ref[q] = ht[q]


def _wkv_scan(proj, batch, seq, wuh, wul, au, w0, a0, k_k, k_a, r_k, seg):
    t_total = batch * seq
    n_sub = math.gcd(seq // CHUNK, SCAN_SUB)
    blk = n_sub * CHUNK
    nc = seq // blk

    def rowblk(b, d, c):
        return b * nc + c + d * (nc - 1 - 2 * c)

    def col(jblk):
        return lambda b, d, c: (rowblk(b, d, c), jblk)

    full = lambda b, d, c: (0, 0)
    perdir = lambda b, d, c: (d, 0, 0)
    return pl.pallas_call(
        functools.partial(_scan_kernel, n_sub=n_sub),
        grid=(batch, 2, nc),
        in_specs=[
            pl.BlockSpec((blk, GROUP), col(0)),
            pl.BlockSpec((blk, GROUP), col(1)),
            pl.BlockSpec((blk, GROUP), col(2)),
            pl.BlockSpec((blk, 128), col(LORA_BLK)),
            pl.BlockSpec((blk, 128), col(LORA_BLK + 1)),
            pl.BlockSpec((1, 128, D_MODEL), perdir),
            pl.BlockSpec((1, 128, D_MODEL), perdir),
            pl.BlockSpec((1, 128, D_MODEL), perdir),
            pl.BlockSpec((1, 1, D_MODEL), perdir),
            pl.BlockSpec((1, 1, D_MODEL), perdir),
            pl.BlockSpec((1, D_MODEL), full),
            pl.BlockSpec((1, D_MODEL), full),
            pl.BlockSpec((1, D_MODEL), full),
            pl.BlockSpec((D_MODEL, 128), full),
        ],
        out_specs=[
            pl.BlockSpec((1, blk, D_MODEL), lambda b, d, c: (d, rowblk(b, d, c), 0)),
            pl.BlockSpec((1, blk, 128), lambda b, d, c: (d, rowblk(b, d, c), 0)),
        ],
        out_shape=[
            jax.ShapeDtypeStruct((2, t_total, D_MODEL), BF16),
            jax.ShapeDtypeStruct((2, t_total, 128), F32),
        ],
        scratch_shapes=[pltpu.VMEM((N_QUAD, QUAD, QUAD), F32)],
        compiler_params=pltpu.CompilerParams(
            dimension_semantics=("arbitrary", "arbitrary", "arbitrary"), vmem_limit_bytes=VMEM_LIMIT),
        name="wkv_scan",
    )(proj, proj, proj, proj, proj, wuh, wul, au, w0, a0, k_k, k_a, r_k, seg)


V_ROWS = 128 + 16
PV_GROUP = 2


def _attn_kernel(q_ref, k_ref, v_ref, lq1_ref, lk1_ref, lq2_ref, lk2_ref, sw_ref, o_ref,
                 k0_ref, k1_ref, vt_ref, s_ref, m_ref, p_ref, *, seq, tq, tk, per_trip, nq, lam_init):
    i = pl.program_id(2)
    n_kc = seq // tk
    lane = lax.broadcasted_iota(jnp.int32, (1, 128), 1)
    first = lane < HEAD

    @pl.when(i == 0)
    def _():
        def prep(kc, carry):
            rows = pl.ds(pl.multiple_of(kc * tk, tk), tk)
            kf = k_ref[rows, :]
            k0_ref[rows, :] = jnp.where(first, kf, 0.0).astype(BF16)
            k1_ref[rows, :] = jnp.where(first, 0.0, kf).astype(BF16)
            vt_ref[0:128, rows] = v_ref[rows, :].T.astype(BF16)
            vt_ref[128:V_ROWS, rows] = jnp.ones((V_ROWS - 128, tk), BF16)
            return carry
        lax.fori_loop(0, n_kc, prep, 0)

    def rows_of(kc):
        return pl.ds(pl.multiple_of(kc * tk, tk), tk)

    def fold(op, s):
        return op(s.reshape(tk // 8, 8, tq), axis=0)

    def query_t():
        return (q_ref[...] * LOG2_E).T.astype(BF16)

    neg = jnp.full((8, tq), -jnp.inf, F32)
    za = jnp.zeros((V_ROWS, tq), F32)
    n_trips = n_kc // per_trip

    def finish(acc0, acc1):
        l0 = acc0[128:129, :]
        l1 = acc1[128:129, :]
        lam = (jnp.exp(jnp.sum(lq1_ref[...] * lk1_ref[...], axis=-1, keepdims=True))
               - jnp.exp(jnp.sum(lq2_ref[...] * lk2_ref[...], axis=-1, keepdims=True)) + lam_init)
        o_t = acc0[0:128, :] / l0 - lam * (acc1[0:128, :] / l1)
        o = o_t.T
        o = o * lax.rsqrt(jnp.mean(o * o, axis=-1, keepdims=True) + SUBLN_EPS) * sw_ref[...]
        o_ref[...] = (o * (1.0 - lam_init)).astype(o_ref.dtype)

    def run(slot_a):
        slot_b = 1 - slot_a
        mine = (i % 2) == slot_a

        def trip(t, q_t, mx, m_prev, acc):
            for c in range(per_trip):
                rows = rows_of(t * per_trip + c)
                if q_t is not None:
                    s0 = _dot(k0_ref[rows, :], q_t)
                    s1 = _dot(k1_ref[rows, :], q_t)
                    s_ref[slot_a, 0, rows, :] = s0
                    s_ref[slot_a, 1, rows, :] = s1
                    mx = (jnp.maximum(mx[0], fold(jnp.max, s0)), jnp.maximum(mx[1], fold(jnp.max, s1)))
                if acc is not None:
                    for comp in range(2):
                        x = s_ref[slot_b, comp, rows, :] - m_prev[comp]
                        p_ref[comp, c * tk:(c + 1) * tk, :] = jnp.exp2(x.astype(BF16))
                    if (c + 1) % PV_GROUP == 0 or c + 1 == per_trip:
                        c0 = (c // PV_GROUP) * PV_GROUP
                        span = (c + 1 - c0) * tk
                        start = (t * per_trip + c0) * tk
                        vt = vt_ref[:, pl.ds(pl.multiple_of(start, tk), span)]
                        acc = tuple(acc[comp] + _dot(vt, p_ref[comp, c0 * tk:(c + 1) * tk, :])
                                    for comp in range(2))
            return mx, acc

        def loop(body, init):
            if n_trips == 1:
                return body(0, init)
            return lax.fori_loop(0, n_trips, body, init)

        def prev_max():
            return (jnp.max(m_ref[slot_b, 0], axis=0, keepdims=True),
                    jnp.max(m_ref[slot_b, 1], axis=0, keepdims=True))

        def store_max(mx):
            m_ref[slot_a, 0] = mx[0]
            m_ref[slot_a, 1] = mx[1]

        if slot_a == 0:
            @pl.when(i == 0)
            def _():
                q_t = query_t()
                store_max(loop(lambda t, mx: trip(t, q_t, mx, None, None)[0], (neg, neg)))

        @pl.when(jnp.logical_and(mine, jnp.logical_and(i > 0, i < nq)))
        def _():
            q_t = query_t()
            m_prev = prev_max()
            mx, acc = loop(lambda t, c: trip(t, q_t, c[0], m_prev, c[1]), ((neg, neg), (za, za)))
            store_max(mx)
            finish(*acc)

        if nq % 2 == slot_a:
            @pl.when(i == nq)
            def _():
                m_prev = prev_max()
                finish(*loop(lambda t, acc: trip(t, None, None, m_prev, acc)[1], (za, za)))

    run(0)
    run(1)


def _diff_attn(proj, batch, seq, lq1, lk1, lq2, lk2, subln_w, lam_init):
    t_total = batch * seq
    tq = min(512, seq)
    tk = min(256, seq)
    per_trip = math.gcd(seq // tk, 16)
    nq = seq // tq
    n_heads = N_PAIR
    qb, kb, vb = (G_Q * GROUP // 128, G_KB * GROUP // 128, G_VB * GROUP // 128)
    kern = functools.partial(_attn_kernel, seq=seq, tq=tq, tk=tk, per_trip=per_trip, nq=nq, lam_init=lam_init)
    small = lambda b, h, i: (0, 0)
    return pl.pallas_call(
        kern,
        grid=(batch, n_heads, nq + 1),
        in_specs=[
            pl.BlockSpec((tq, 128), lambda b, h, i: (b * nq + jnp.minimum(i, nq - 1), qb + h)),
            pl.BlockSpec((seq, 128), lambda b, h, i: (b, kb + h)),
            pl.BlockSpec((seq, 128), lambda b, h, i: (b, vb + h)),
            pl.BlockSpec((1, HEAD), small),
            pl.BlockSpec((1, HEAD), small),
            pl.BlockSpec((1, HEAD), small),
            pl.BlockSpec((1, HEAD), small),
            pl.BlockSpec((1, 128), small),
        ],
        out_specs=pl.BlockSpec((tq, 128), lambda b, h, i: (b * nq + jnp.maximum(i - 1, 0), h)),
        out_shape=jax.ShapeDtypeStruct((t_total, D_MODEL), BF16),
        scratch_shapes=[pltpu.VMEM((seq, 128), BF16), pltpu.VMEM((seq, 128), BF16),
                        pltpu.VMEM((V_ROWS, seq), BF16), pltpu.VMEM((2, 2, seq, tq), F32),
                        pltpu.VMEM((2, 2, 8, tq), F32), pltpu.VMEM((2, per_trip * tk, tq), BF16)],
        compiler_params=pltpu.CompilerParams(
            dimension_semantics=("arbitrary", "arbitrary", "arbitrary"), vmem_limit_bytes=VMEM_LIMIT),
        name="diff_attn",
    )(proj, proj, proj, lq1, lk1, lq2, lk2, subln_w)


def _mix_kernel(x_ref, yf_ref, yb_ref, sf_ref, sb_ref, v_ref, lg_ref, ga_ref, gb_ref, o_ref,
                lnw_ref, lnb_ref, gup_ref, exp_ref, pa_ref, pb_ref, wo_ref, h_ref, ya_ref):
    y = yf_ref[0].astype(F32) + yb_ref[0].astype(F32)
    rh = lax.broadcasted_iota(jnp.int32, (QUAD, QUAD), 0) // HEAD
    ch = lax.broadcasted_iota(jnp.int32, (QUAD, QUAD), 1) // HEAD
    avg = jnp.where(rh == ch, 1.0 / HEAD, 0.0).astype(BF16)
    for q in range(N_QUAD):
        sl = slice(QUAD * q, QUAD * (q + 1))
        yq = y[:, sl]
        cen = yq - _dot(yq.astype(BF16), avg)
        var = _dot((cen * cen).astype(BF16), avg)
        ya_ref[:, sl] = cen * lax.rsqrt(var + GN_EPS)
    coef = _dot((sf_ref[0] + sb_ref[0]).astype(BF16), exp_ref[...])
    gate = _dot(_sigmoid(lg_ref[...]).astype(BF16), gup_ref[...])
    y_a = (ya_ref[...] * lnw_ref[...] + lnb_ref[...] + coef * v_ref[...]) * gate
    pa = _dot(y_a.astype(BF16), pa_ref[...])
    pb = _dot(o_ref[...], pb_ref[...])
    merged = _sigmoid(ga_ref[...]) * pa + _sigmoid(gb_ref[...]) * pb
    h_ref[...] = x_ref[...] + _dot(merged.astype(BF16), wo_ref[...])


def _mix(x2d, proj, y_dir, s_dir, o_attn, ln_w, ln_b, g_up, expand, proj_a, proj_b, w_out):
    t_total = x2d.shape[0]
    tm = 256
    row = lambda i: (i, 0)
    full = lambda i: (0, 0)
    wspec = pl.BlockSpec((D_MODEL, D_MODEL), full)
    return pl.pallas_call(
        _mix_kernel,
        grid=(t_total // tm,),
        in_specs=[
            pl.BlockSpec((tm, D_MODEL), row),
            pl.BlockSpec((1, tm, D_MODEL), lambda i: (0, i, 0)),
            pl.BlockSpec((1, tm, D_MODEL), lambda i: (1, i, 0)),
            pl.BlockSpec((1, tm, 128), lambda i: (0, i, 0)),
            pl.BlockSpec((1, tm, 128), lambda i: (1, i, 0)),
            pl.BlockSpec((tm, GROUP), lambda i: (i, 2)),
            pl.BlockSpec((tm, 128), lambda i: (i, LORA_BLK + 2)),
            pl.BlockSpec((tm, GROUP), lambda i: (i, G_GA)),
            pl.BlockSpec((tm, GROUP), lambda i: (i, G_GB)),
            pl.BlockSpec((tm, D_MODEL), row),
            pl.BlockSpec((1, D_MODEL), full),
            pl.BlockSpec((1, D_MODEL), full),
            pl.BlockSpec((GATE_RANK, D_MODEL), full),
            pl.BlockSpec((128, D_MODEL), full),
            wspec, wspec, wspec,
        ],
        out_specs=pl.BlockSpec((tm, D_MODEL), row),
        out_shape=jax.ShapeDtypeStruct((t_total, D_MODEL), F32),
        scratch_shapes=[pltpu.VMEM((tm, D_MODEL), F32)],
        compiler_params=pltpu.CompilerParams(
            dimension_semantics=("arbitrary",), vmem_limit_bytes=VMEM_LIMIT),
        name="mix",
    )(x2d, y_dir, y_dir, s_dir, s_dir, proj, proj, proj, proj, o_attn,
      ln_w, ln_b, g_up, expand, proj_a, proj_b, w_out)


def _mlp_kernel(h_ref, gm_ref, gf_ref, w1_ref, w2_ref, o_ref, *, ff_chunk):
    h = h_ref[...]
    xn = _rms(h, gm_ref[...]).astype(BF16)
    acc = h
    for c in range(D_FF // ff_chunk):
        sl = slice(c * ff_chunk, (c + 1) * ff_chunk)
        hid = jnp.maximum(_dot(xn, w1_ref[:, sl]), 0.0)
        acc = acc + _dot((hid * hid).astype(BF16), w2_ref[sl, :])
    o_ref[...] = _rms(acc, gf_ref[...])


def _mlp(h2d, norm_mlp, norm_final, w1, w2):
    t_total = h2d.shape[0]
    tm = 256
    row = lambda i: (i, 0)
    full = lambda i: (0, 0)
    kern = functools.partial(_mlp_kernel, ff_chunk=1024)
    return pl.pallas_call(
        kern,
        grid=(t_total // tm,),
        in_specs=[
            pl.BlockSpec((tm, D_MODEL), row),
            pl.BlockSpec((1, D_MODEL), full),
            pl.BlockSpec((1, D_MODEL), full),
            pl.BlockSpec((D_MODEL, D_FF), full),
            pl.BlockSpec((D_FF, D_MODEL), full),
        ],
        out_specs=pl.BlockSpec((tm, D_MODEL), row),
        out_shape=jax.ShapeDtypeStruct((t_total, D_MODEL), F32),
        compiler_params=pltpu.CompilerParams(
            dimension_semantics=("arbitrary",), vmem_limit_bytes=VMEM_LIMIT),
        name="mlp",
    )(h2d, norm_mlp, norm_final, w1, w2)


def _rope_tables(seq):
    half = ROPE_DIM // 2
    pos = jnp.arange(seq, dtype=F32)
    inv_freq = ROPE_THETA ** (-jnp.arange(0, ROPE_DIM, 2, dtype=F32) / ROPE_DIM)
    ang = pos[:, None] * inv_freq[None, :]
    cos, sin = jnp.cos(ang), jnp.sin(ang)
    pad = jnp.zeros((seq, HEAD - ROPE_DIM), F32)
    z8 = jnp.zeros((seq, half), F32)
    c64 = jnp.concatenate([cos, cos, pad + 1.0], axis=1)
    sa64 = jnp.concatenate([z8, sin, pad], axis=1)
    sb64 = jnp.concatenate([-sin, z8, pad], axis=1)
    two = lambda t: jnp.concatenate([t, t], axis=1)
    return two(c64), two(sa64), two(sb64)


def _layer(x, l, prm, norm_final):
    batch, seq, _ = x.shape
    x2d = x.reshape(batch * seq, D_MODEL)
    cos_t, sa_t, sb_t = _rope_tables(seq)
    proj = _inproj(x2d, seq, prm["norm_mix"], prm["w_re"], prm["mu_re"], cos_t, sa_t, sb_t)
    y_dir, s_dir = _wkv_scan(proj, batch, seq, prm["wuh"], prm["wul"], prm["au"],
                             prm["w0"], prm["a0"], prm["k_k"], prm["k_a"], prm["r_k"], prm["seg"])
    lam_init = 0.8 - 0.6 * math.exp(-0.3 * l)
    o_attn = _diff_attn(proj, batch, seq, prm["lq1"], prm["lk1"], prm["lq2"], prm["lk2"],
                        prm["subln_w"], lam_init)
    h = _mix(x2d, proj, y_dir, s_dir, o_attn, prm["ln_w"], prm["ln_b"], prm["g_up"], prm["expand"],
             prm["proj_a"], prm["proj_b"], prm["w_out"])
    return h


def _prep_layer(l, w_in, mu_shift, w0, w_lora_up, a0, a_lora_up, g_lora_up, k_k, k_a, r_k, ln_x_w,
                ln_x_b, lambda_q1, lambda_k1, lambda_q2, lambda_k2, subln_w, proj_a, proj_b, w_out,
                norm_mix, norm_mlp, w_mlp_in, w_mlp_out):
    w = w_in[l]
    pad_cols = N_GROUPS * GROUP - w.shape[1]
    w_re = jnp.concatenate([w[:, :RWKV_MAIN], w[:, RWKV_COLS:], w[:, RWKV_MAIN:RWKV_COLS],
                            jnp.zeros((D_MODEL, pad_cols), F32)], axis=1).astype(BF16)
    mu = mu_shift[l]
    mu_re = jnp.concatenate([mu[:RWKV_MAIN], jnp.zeros((5 * GROUP,), F32), mu[RWKV_MAIN:],
                             jnp.zeros((pad_cols,), F32)])[None, :]

    def lora_pad(up):
        z = jnp.zeros_like(up[0])
        return jnp.stack([jnp.concatenate([up[0], z], axis=0), jnp.concatenate([z, up[1]], axis=0)])

    wu = lora_pad(w_lora_up[l])
    au = lora_pad(a_lora_up[l])
    wuh = wu.astype(BF16)
    head_of_lane = jnp.arange(D_MODEL) // HEAD
    seg = (head_of_lane[:, None] == jnp.arange(128)[None, :]).astype(BF16)
    return dict(
        w_re=w_re, mu_re=mu_re, norm_mix=norm_mix[l][None, :],
        wuh=wuh, wul=(wu - wuh.astype(F32)).astype(BF16),
        au=au.astype(BF16),
        w0=w0[l][:, None, :], a0=a0[l][:, None, :],
        k_k=k_k[l][None, :], k_a=k_a[l][None, :], r_k=r_k[l].reshape(1, D_MODEL),
        seg=seg, expand=seg.T,
        lq1=lambda_q1[l][None, :], lk1=lambda_k1[l][None, :],
        lq2=lambda_q2[l][None, :], lk2=lambda_k2[l][None, :],
        subln_w=subln_w[l][None, :],
        ln_w=ln_x_w[l][None, :], ln_b=ln_x_b[l][None, :],
        g_up=g_lora_up[l].astype(BF16),
        proj_a=proj_a[l].astype(BF16), proj_b=proj_b[l].astype(BF16), w_out=w_out[l].astype(BF16),
        norm_mlp=norm_mlp[l][None, :],
        w1=w_mlp_in[l].astype(BF16), w2=w_mlp_out[l].astype(BF16),
    )


def kernel(x_prompt, x_sample, w_in, mu_shift, w0, w_lora_up, a0, a_lora_up, g_lora_up, k_k, k_a, r_k, ln_x_w, ln_x_b, lambda_q1, lambda_k1, lambda_q2, lambda_k2, subln_w, proj_a, proj_b, w_out, norm_mix, norm_mlp, w_mlp_in, w_mlp_out, norm_final):
    depth = w_in.shape[0]
    assert depth == 1, "the final norm is fused into the (single) layer's MLP kernel"
    prm = _prep_layer(0, w_in, mu_shift, w0, w_lora_up, a0, a_lora_up, g_lora_up, k_k, k_a, r_k, ln_x_w,
                      ln_x_b, lambda_q1, lambda_k1, lambda_q2, lambda_k2, subln_w, proj_a, proj_b,
                      w_out, norm_mix, norm_mlp, w_mlp_in, w_mlp_out)
    outs = []
    for x in (x_prompt, x_sample):
        h = _layer(x, 0, prm, norm_final)
        y = _mlp(h, prm["norm_mlp"], norm_final[None, :], prm["w1"], prm["w2"])
        outs.append(y.reshape(x.shape))
    return tuple(outs)
```

```python
import functools
import math

import jax
import jax.numpy as jnp
from jax import lax
from jax.experimental import pallas as pl
from jax.experimental.pallas import tpu as pltpu

F32 = jnp.float32
BF16 = jnp.bfloat16

D_MODEL = 1024
HEAD = 64
N_PAIR = D_MODEL // 128
DECAY_RANK = 64
ICLR_RANK = 64
GATE_RANK = 128
GN_EPS = 64e-5
ROPE_THETA = 500000.0
ROPE_DIM = 16
SUBLN_EPS = 1e-5
NORM_EPS = 1e-6
D_FF = 4 * D_MODEL
RWKV_MAIN = 3 * D_MODEL
RWKV_COLS = RWKV_MAIN + 2 * DECAY_RANK + 2 * ICLR_RANK + GATE_RANK
GROUP = 1024
N_GROUPS = 9
G_Q, G_KB, G_VB, G_GA, G_GB, G_LORA = 3, 4, 5, 6, 7, 8
GROUPS_PER_STEP = 3
LORA_BLK = G_LORA * GROUP // 128
CHUNK = 64
EXP_M05 = math.exp(-0.5)
LOG2_E = math.log2(math.e)
VMEM_LIMIT = 56 * 1024 * 1024

NN = (((1,), (0,)), ((), ()))
NT = (((1,), (1,)), ((), ()))
TN = (((0,), (0,)), ((), ()))


def _dot(a, b, dims=NN):
    return lax.dot_general(a, b, dims, preferred_element_type=F32)


def _split(x):
    hi = x.astype(BF16)
    lo = (x - hi.astype(F32)).astype(BF16)
    return hi, lo


def _dot3(a, b, dims=NN):
    ah, al = _split(a)
    bh, bl = _split(b)
    return _dot(ah, bh, dims) + (_dot(ah, bl, dims) + _dot(al, bh, dims))


def _dot2_exact_rhs(a, b_bf16, dims=NN):
    ah, al = _split(a)
    return _dot(ah, b_bf16, dims) + _dot(al, b_bf16, dims)


def _sigmoid(x):
    return 1.0 / (1.0 + jnp.exp(-x))


def _rms(x, g):
    return x * lax.rsqrt(jnp.mean(x * x, axis=-1, keepdims=True) + NORM_EPS) * g


def _inproj_kernel(x_ref, xp_ref, xn_ref, g_ref, w_ref, mu_ref, cos_ref, sa_ref, sb_ref,
                   o_ref, xs_ref, hs_ref, *, tm, tiles_per_seq):
    j = pl.program_id(0)
    i = pl.program_id(1)
    g = g_ref[...]
    xs_ref[...] = _rms(x_ref[...], g).astype(BF16)
    hs_ref[0:8, :] = _rms(xp_ref[...], g)
    hs_ref[8:16, :] = _rms(xn_ref[...], g)

    def store_shifted(p, w, cols):
        ph = _dot(hs_ref[...].astype(BF16), w)
        t_in_seq = i % tiles_per_seq
        prev = jnp.where(t_in_seq == 0, 0.0, ph[7:8, :])
        nxt = jnp.where(t_in_seq == tiles_per_seq - 1, 0.0, ph[8:9, :])
        c2 = 0.5 * mu_ref[:, cols]
        c1 = 1.0 - mu_ref[:, cols]
        core = p * c1 + (pltpu.roll(p, 1, 0) + pltpu.roll(p, tm - 1, 0)) * c2
        o_ref[:, cols] = core
        o_ref[0:1, cols] = core[0:1, :] + (prev - p[tm - 1:tm, :]) * c2
        o_ref[tm - 1:tm, cols] = core[tm - 1:tm, :] + (nxt - p[0:1, :]) * c2

    def rotated(p, scale):
        c = jnp.tile(cos_ref[...], (1, N_PAIR))
        sa = jnp.tile(sa_ref[...], (1, N_PAIR))
        sb = jnp.tile(sb_ref[...], (1, N_PAIR))
        half = ROPE_DIM // 2
        out = p * c + pltpu.roll(p, half, 1) * sa + pltpu.roll(p, GROUP - half, 1) * sb
        return out if scale == 1.0 else out * scale

    for step in range(N_GROUPS // GROUPS_PER_STEP):
        @pl.when(j == step)
        def _(step=step):
            for gi in range(GROUPS_PER_STEP):
                g = step * GROUPS_PER_STEP + gi
                cols = slice(gi * GROUP, (gi + 1) * GROUP)
                w = w_ref[:, cols]
                p = _dot(xs_ref[...], w)
                if g < 3 or g == G_LORA:
                    store_shifted(p, w, cols)
                elif g == G_Q:
                    o_ref[:, cols] = rotated(p, HEAD ** -0.5)
                elif g == G_KB:
                    o_ref[:, cols] = rotated(p, 1.0)
                else:
                    o_ref[:, cols] = p


def _inproj(x2d, seq, g, w_re, mu_re, cos_t, sa_t, sb_t):
    t_total = x2d.shape[0]
    tm = min(512, seq)
    tiles_per_seq = seq // tm
    n_tiles = t_total // tm
    last8 = t_total // 8 - 1
    kern = functools.partial(_inproj_kernel, tm=tm, tiles_per_seq=tiles_per_seq)
    return pl.pallas_call(
        kern,
        grid=(N_GROUPS // GROUPS_PER_STEP, n_tiles),
        in_specs=[
            pl.BlockSpec((tm, D_MODEL), lambda j, i: (i, 0)),
            pl.BlockSpec((8, D_MODEL), lambda j, i: (jnp.maximum(i * (tm // 8) - 1, 0), 0)),
            pl.BlockSpec((8, D_MODEL), lambda j, i: (jnp.minimum((i + 1) * (tm // 8), last8), 0)),
            pl.BlockSpec((1, D_MODEL), lambda j, i: (0, 0)),
            pl.BlockSpec((D_MODEL, GROUPS_PER_STEP * GROUP), lambda j, i: (0, j)),
            pl.BlockSpec((1, GROUPS_PER_STEP * GROUP), lambda j, i: (0, j)),
            pl.BlockSpec((tm, 128), lambda j, i: (i % tiles_per_seq, 0)),
            pl.BlockSpec((tm, 128), lambda j, i: (i % tiles_per_seq, 0)),
            pl.BlockSpec((tm, 128), lambda j, i: (i % tiles_per_seq, 0)),
        ],
        out_specs=pl.BlockSpec((tm, GROUPS_PER_STEP * GROUP), lambda j, i: (i, j)),
        out_shape=jax.ShapeDtypeStruct((t_total, N_GROUPS * GROUP), F32),
        scratch_shapes=[pltpu.VMEM((tm, D_MODEL), BF16), pltpu.VMEM((16, D_MODEL), F32)],
        compiler_params=pltpu.CompilerParams(
            dimension_semantics=("arbitrary", "arbitrary"), vmem_limit_bytes=VMEM_LIMIT),
        name="inproj",
    )(x2d, x2d, x2d, g, w_re, mu_re, cos_t, sa_t, sb_t)


QUAD = 256
N_QUAD = D_MODEL // QUAD
HEADS_PER_QUAD = QUAD // HEAD
SCAN_SUB = 8


def _scan_kernel(r_ref, k_ref, v_ref, lw_ref, la_ref, wuh_ref, wul_ref, au_ref,
                 w0_ref, a0_ref, kk_ref, ka_ref, rk_ref, seg_ref, y_ref, s_ref, h_ref, *, n_sub):
    d = pl.program_id(1)
    c = pl.program_id(2)
    C = CHUNK
    bf = lambda x: x.astype(BF16)

    @pl.when(c == 0)
    def _():
        h_ref[...] = jnp.zeros_like(h_ref)

    sgn = 1 - 2 * d
    rr = lax.broadcasted_iota(jnp.int32, (C, C), 0)
    cc = lax.broadcasted_iota(jnp.int32, (C, C), 1)
    tri = jnp.where((rr - cc) * sgn >= 0, 1.0, 0.0).astype(BF16)

    tr = lax.broadcasted_iota(jnp.int32, (C, QUAD), 0)
    tc = lax.broadcasted_iota(jnp.int32, (C, QUAD), 1) % C
    dd = (tr - tc) * sgn
    strict = dd > 0
    incl = dd >= 0
    eye = jnp.where(dd == 0, 1.0, 0.0)
    levels = []
    m = 1
    while m < C:
        levels.append(jnp.logical_and((tr // (2 * m)) == (tc // (2 * m)), (tr // m) != (tc // m)))
        m *= 2
    lane_head = lax.broadcasted_iota(jnp.int32, (1, QUAD), 1) // HEAD
    head_lanes = [lane_head == h for h in range(HEADS_PER_QUAD)]
    same_head = (lax.broadcasted_iota(jnp.int32, (QUAD, QUAD), 0) // HEAD) == (
        lax.broadcasted_iota(jnp.int32, (QUAD, QUAD), 1) // HEAD)
    bd_ones = jnp.where(same_head, 1.0, 0.0).astype(BF16)

    def blockdiag(x):
        return jnp.concatenate([jnp.where(hl, x, 0.0) for hl in head_lanes], axis=0)

    quads = range(N_QUAD)
    sls = [slice(QUAD * q, QUAD * (q + 1)) for q in quads]

    staged = []
    for i in range(n_sub):
        jj = i + d * (n_sub - 1 - 2 * i)
        rows = pl.ds(pl.multiple_of(jj * C, C), C)
        r = r_ref[rows, :]
        k = k_ref[rows, :]
        v = v_ref[rows, :]

        thh, thl = _split(jnp.tanh(lw_ref[rows, :]))
        w_raw = w0_ref[0] + (_dot(thh, wuh_ref[0]) + (_dot(thh, wul_ref[0]) + _dot(thl, wuh_ref[0])))
        logw = (-EXP_M05 * LOG2_E) * _sigmoid(w_raw)
        a = _sigmoid(a0_ref[0] + _dot(bf(la_ref[rows, :]), au_ref[0]))

        l1, l2 = _split(logw)
        cum = _dot(tri, l1) + _dot(tri, l2)
        total = jnp.sum(logw, axis=0, keepdims=True)
        g_incl = jnp.exp2(cum)
        g_excl = jnp.exp2(cum - logw)
        g_inv = jnp.exp2(-cum)
        g_tail = jnp.exp2(total - cum)
        g_tot = jnp.exp2(total)

        kkv = k * kk_ref[...]
        kd = k * (1.0 + (a - 1.0) * ka_ref[...])
        sq = kkv * kkv

        s_ref[0, rows, :] = _dot(bf(r * kd * rk_ref[...]), seg_ref[...])

        ss = [_dot(bf(sq[:, sl]), bd_ones) for sl in sls]
        kkn = [kkv[:, sl] * lax.rsqrt(jnp.maximum(s, 1e-24)) for sl, s in zip(sls, ss)]
        bvec = [kn * a[:, sl] for sl, kn in zip(sls, kkn)]
        lhs = [bf(jnp.concatenate([-kn * g_excl[:, sl], r[:, sl] * g_incl[:, sl]], axis=0))
               for sl, kn in zip(sls, kkn)]
        rhs = [jnp.concatenate([blockdiag(bf(bv * g_inv[:, sl])), blockdiag(bf(kd[:, sl] * g_inv[:, sl]))], axis=0)
               for sl, bv in zip(sls, bvec)]
        bkh = [bf(jnp.concatenate([bv * g_tail[:, sl], kd[:, sl] * g_tail[:, sl]], axis=0))
               for sl, bv in zip(sls, bvec)]
        vb = [bf(v[:, sl]) for sl in sls]
        v_bd = [blockdiag(x) for x in vb]

        gm = [_dot(lhs[q], rhs[q], NT) for q in quads]
        gb = [bf(g) for g in gm]
        a_ab = [jnp.where(strict, g[0:C, 0:QUAD], 0.0) for g in gm]
        a_abb = [jnp.where(strict, g[0:C, 0:QUAD], 0.0) for g in gb]
        a_ak = [jnp.where(strict, g[0:C, QUAD:2 * QUAD], 0.0) for g in gb]
        a_r = [jnp.concatenate([jnp.where(incl, g[C:2 * C, 0:QUAD], 0.0),
                                jnp.where(incl, g[C:2 * C, QUAD:2 * QUAD], 0.0)], axis=1) for g in gb]
        akv = [_dot(a_ak[q], v_bd[q]) for q in quads]
        staged.append(dict(rows=rows, lhs=lhs, a_ab=a_ab, a_abb=a_abb, a_r=a_r, akv=akv, vb=vb, v_bd=v_bd,
                           bkh=bkh, g_tot=g_tot))

    combos = [(i, q) for i in range(n_sub) for q in quads]
    tinv = {(i, q): eye + jnp.where(levels[0], staged[i]["a_ab"][q], 0.0) for i, q in combos}
    for lvl in levels[1:]:
        tb = {key: bf(t) for key, t in tinv.items()}
        wm = {(i, q): _dot(jnp.where(lvl, staged[i]["a_abb"][q], 0.0), blockdiag(tb[i, q])) for i, q in combos}
        tinv = {key: tinv[key] + _dot(tb[key], blockdiag(bf(wm[key]))) for key in combos}

    ht = [h_ref[q] for q in quads]
    for i in range(n_sub):
        st = staged[i]
        ar = [_dot(st["lhs"][q], bf(ht[q]), NT) for q in quads]
        rhs_u = [ar[q][0:C] + st["akv"][q] for q in quads]
        u = [bf(_dot(bf(tinv[i, q]), blockdiag(bf(rhs_u[q])))) for q in quads]
        uv_bd = [jnp.concatenate([blockdiag(u[q]), st["v_bd"][q]], axis=0) for q in quads]
        for q in quads:
            y_ref[0, st["rows"], sls[q]] = (ar[q][C:2 * C] + _dot(st["a_r"][q], uv_bd[q])).astype(y_ref.dtype)
        upd = [_dot(jnp.concatenate([u[q], st["vb"][q]], axis=0), st["bkh"][q], TN) for q in quads]
        ht = [ht[q] * st["g_tot"][:, sls[q]] + jnp.where(same_head, upd[q], 0.0) for q in quads]

    for q in quads:
        h_ref[q] = ht[q]


def _wkv_scan(proj, batch, seq, wuh, wul, au, w0, a0, k_k, k_a, r_k, seg):
    t_total = batch * seq
    n_sub = math.gcd(seq // CHUNK, SCAN_SUB)
    blk = n_sub * CHUNK
    nc = seq // blk

    def rowblk(b, d, c):
        return b * nc + c + d * (nc - 1 - 2 * c)

    def col(jblk):
        return lambda b, d, c: (rowblk(b, d, c), jblk)

    full = lambda b, d, c: (0, 0)
    perdir = lambda b, d, c: (d, 0, 0)
    return pl.pallas_call(
        functools.partial(_scan_kernel, n_sub=n_sub),
        grid=(batch, 2, nc),
        in_specs=[
            pl.BlockSpec((blk, GROUP), col(0)),
            pl.BlockSpec((blk, GROUP), col(1)),
            pl.BlockSpec((blk, GROUP), col(2)),
            pl.BlockSpec((blk, 128), col(LORA_BLK)),
            pl.BlockSpec((blk, 128), col(LORA_BLK + 1)),
            pl.BlockSpec((1, 128, D_MODEL), perdir),
            pl.BlockSpec((1, 128, D_MODEL), perdir),
            pl.BlockSpec((1, 128, D_MODEL), perdir),
            pl.BlockSpec((1, 1, D_MODEL), perdir),
            pl.BlockSpec((1, 1, D_MODEL), perdir),
            pl.BlockSpec((1, D_MODEL), full),
            pl.BlockSpec((1, D_MODEL), full),
            pl.BlockSpec((1, D_MODEL), full),
            pl.BlockSpec((D_MODEL, 128), full),
        ],
        out_specs=[
            pl.BlockSpec((1, blk, D_MODEL), lambda b, d, c: (d, rowblk(b, d, c), 0)),
            pl.BlockSpec((1, blk, 128), lambda b, d, c: (d, rowblk(b, d, c), 0)),
        ],
        out_shape=[
            jax.ShapeDtypeStruct((2, t_total, D_MODEL), BF16),
            jax.ShapeDtypeStruct((2, t_total, 128), F32),
        ],
        scratch_shapes=[pltpu.VMEM((N_QUAD, QUAD, QUAD), F32)],
        compiler_params=pltpu.CompilerParams(
            dimension_semantics=("arbitrary", "arbitrary", "arbitrary"), vmem_limit_bytes=VMEM_LIMIT),
        name="wkv_scan",
    )(proj, proj, proj, proj, proj, wuh, wul, au, w0, a0, k_k, k_a, r_k, seg)


V_ROWS = 128 + 16
PV_GROUP = 2


def _attn_kernel(q_ref, k_ref, v_ref, lq1_ref, lk1_ref, lq2_ref, lk2_ref, sw_ref, o_ref,
                 k0_ref, k1_ref, vt_ref, s_ref, m_ref, p_ref, *, seq, tq, tk, per_trip, nq, lam_init):
    i = pl.program_id(2)
    n_kc = seq // tk
    lane = lax.broadcasted_iota(jnp.int32, (1, 128), 1)
    first = lane < HEAD

    @pl.when(i == 0)
    def _():
        def prep(kc, carry):
            rows = pl.ds(pl.multiple_of(kc * tk, tk), tk)
            kf = k_ref[rows, :]
            k0_ref[rows, :] = jnp.where(first, kf, 0.0).astype(BF16)
            k1_ref[rows, :] = jnp.where(first, 0.0, kf).astype(BF16)
            vt_ref[0:128, rows] = v_ref[rows, :].T.astype(BF16)
            vt_ref[128:V_ROWS, rows] = jnp.ones((V_ROWS - 128, tk), BF16)
            return carry
        lax.fori_loop(0, n_kc, prep, 0)

    def rows_of(kc):
        return pl.ds(pl.multiple_of(kc * tk, tk), tk)

    def fold(op, s):
        return op(s.reshape(tk // 8, 8, tq), axis=0)

    def query_t():
        return (q_ref[...] * LOG2_E).T.astype(BF16)

    neg = jnp.full((8, tq), -jnp.inf, F32)
    za = jnp.zeros((V_ROWS, tq), F32)
    n_trips = n_kc // per_trip

    def finish(acc0, acc1):
        l0 = acc0[128:129, :]
        l1 = acc1[128:129, :]
        lam = (jnp.exp(jnp.sum(lq1_ref[...] * lk1_ref[...], axis=-1, keepdims=True))
               - jnp.exp(jnp.sum(lq2_ref[...] * lk2_ref[...], axis=-1, keepdims=True)) + lam_init)
        o_t = acc0[0:128, :] / l0 - lam * (acc1[0:128, :] / l1)
        o = o_t.T
        o = o * lax.rsqrt(jnp.mean(o * o, axis=-1, keepdims=True) + SUBLN_EPS) * sw_ref[...]
        o_ref[...] = (o * (1.0 - lam_init)).astype(o_ref.dtype)

    def run(slot_a):
        slot_b = 1 - slot_a
        mine = (i % 2) == slot_a

        def trip(t, q_t, mx, m_prev, acc):
            for c in range(per_trip):
                rows = rows_of(t * per_trip + c)
                if q_t is not None:
                    s0 = _dot(k0_ref[rows, :], q_t)
                    s1 = _dot(k1_ref[rows, :], q_t)
                    s_ref[slot_a, 0, rows, :] = s0
                    s_ref[slot_a, 1, rows, :] = s1
                    mx = (jnp.maximum(mx[0], fold(jnp.max, s0)), jnp.maximum(mx[1], fold(jnp.max, s1)))
                if acc is not None:
                    for comp in range(2):
                        x = s_ref[slot_b, comp, rows, :] - m_prev[comp]
                        p_ref[comp, c * tk:(c + 1) * tk, :] = jnp.exp2(x.astype(BF16))
                    if (c + 1) % PV_GROUP == 0 or c + 1 == per_trip:
                        c0 = (c // PV_GROUP) * PV_GROUP
                        span = (c + 1 - c0) * tk
                        start = (t * per_trip + c0) * tk
                        vt = vt_ref[:, pl.ds(pl.multiple_of(start, tk), span)]
                        acc = tuple(acc[comp] + _dot(vt, p_ref[comp, c0 * tk:(c + 1) * tk, :])
                                    for comp in range(2))
            return mx, acc

        def loop(body, init):
            if n_trips == 1:
                return body(0, init)
            return lax.fori_loop(0, n_trips, body, init)

        def prev_max():
            return (jnp.max(m_ref[slot_b, 0], axis=0, keepdims=True),
                    jnp.max(m_ref[slot_b, 1], axis=0, keepdims=True))

        def store_max(mx):
            m_ref[slot_a, 0] = mx[0]
            m_ref[slot_a, 1] = mx[1]

        if slot_a == 0:
            @pl.when(i == 0)
            def _():
                q_t = query_t()
                store_max(loop(lambda t, mx: trip(t, q_t, mx, None, None)[0], (neg, neg)))

        @pl.when(jnp.logical_and(mine, jnp.logical_and(i > 0, i < nq)))
        def _():
            q_t = query_t()
            m_prev = prev_max()
            mx, acc = loop(lambda t, c: trip(t, q_t, c[0], m_prev, c[1]), ((neg, neg), (za, za)))
            store_max(mx)
            finish(*acc)

        if nq % 2 == slot_a:
            @pl.when(i == nq)
            def _():
                m_prev = prev_max()
                finish(*loop(lambda t, acc: trip(t, None, None, m_prev, acc)[1], (za, za)))

    run(0)
    run(1)


def _diff_attn(proj, batch, seq, lq1, lk1, lq2, lk2, subln_w, lam_init):
    t_total = batch * seq
    tq = min(512, seq)
    tk = min(256, seq)
    per_trip = math.gcd(seq // tk, 16)
    nq = seq // tq
    n_heads = N_PAIR
    qb, kb, vb = (G_Q * GROUP // 128, G_KB * GROUP // 128, G_VB * GROUP // 128)
    kern = functools.partial(_attn_kernel, seq=seq, tq=tq, tk=tk, per_trip=per_trip, nq=nq, lam_init=lam_init)
    small = lambda b, h, i: (0, 0)
    return pl.pallas_call(
        kern,
        grid=(batch, n_heads, nq + 1),
        in_specs=[
            pl.BlockSpec((tq, 128), lambda b, h, i: (b * nq + jnp.minimum(i, nq - 1), qb + h)),
            pl.BlockSpec((seq, 128), lambda b, h, i: (b, kb + h)),
            pl.BlockSpec((seq, 128), lambda b, h, i: (b, vb + h)),
            pl.BlockSpec((1, HEAD), small),
            pl.BlockSpec((1, HEAD), small),
            pl.BlockSpec((1, HEAD), small),
            pl.BlockSpec((1, HEAD), small),
            pl.BlockSpec((1, 128), small),
        ],
        out_specs=pl.BlockSpec((tq, 128), lambda b, h, i: (b * nq + jnp.maximum(i - 1, 0), h)),
        out_shape=jax.ShapeDtypeStruct((t_total, D_MODEL), BF16),
        scratch_shapes=[pltpu.VMEM((seq, 128), BF16), pltpu.VMEM((seq, 128), BF16),
                        pltpu.VMEM((V_ROWS, seq), BF16), pltpu.VMEM((2, 2, seq, tq), F32),
                        pltpu.VMEM((2, 2, 8, tq), F32), pltpu.VMEM((2, per_trip * tk, tq), BF16)],
        compiler_params=pltpu.CompilerParams(
            dimension_semantics=("arbitrary", "arbitrary", "arbitrary"), vmem_limit_bytes=VMEM_LIMIT),
        name="diff_attn",
    )(proj, proj, proj, lq1, lk1, lq2, lk2, subln_w)


def _mix_kernel(x_ref, yf_ref, yb_ref, sf_ref, sb_ref, v_ref, lg_ref, ga_ref, gb_ref, o_ref,
                lnw_ref, lnb_ref, gup_ref, exp_ref, pa_ref, pb_ref, wo_ref, h_ref, ya_ref):
    y = yf_ref[0].astype(F32) + yb_ref[0].astype(F32)
    rh = lax.broadcasted_iota(jnp.int32, (QUAD, QUAD), 0) // HEAD
    ch = lax.broadcasted_iota(jnp.int32, (QUAD, QUAD), 1) // HEAD
    avg = jnp.where(rh == ch, 1.0 / HEAD, 0.0).astype(BF16)
    for q in range(N_QUAD):
        sl = slice(QUAD * q, QUAD * (q + 1))
        yq = y[:, sl]
        cen = yq - _dot(yq.astype(BF16), avg)
        var = _dot((cen * cen).astype(BF16), avg)
        ya_ref[:, sl] = cen * lax.rsqrt(var + GN_EPS)
    coef = _dot((sf_ref[0] + sb_ref[0]).astype(BF16), exp_ref[...])
    gate = _dot(_sigmoid(lg_ref[...]).astype(BF16), gup_ref[...])
    y_a = (ya_ref[...] * lnw_ref[...] + lnb_ref[...] + coef * v_ref[...]) * gate
    pa = _dot(y_a.astype(BF16), pa_ref[...])
    pb = _dot(o_ref[...], pb_ref[...])
    merged = _sigmoid(ga_ref[...]) * pa + _sigmoid(gb_ref[...]) * pb
    h_ref[...] = x_ref[...] + _dot(merged.astype(BF16), wo_ref[...])


def _mix(x2d, proj, y_dir, s_dir, o_attn, ln_w, ln_b, g_up, expand, proj_a, proj_b, w_out):
    t_total = x2d.shape[0]
    tm = 256
    row = lambda i: (i, 0)
    full = lambda i: (0, 0)
    wspec = pl.BlockSpec((D_MODEL, D_MODEL), full)
    return pl.pallas_call(
        _mix_kernel,
        grid=(t_total // tm,),
        in_specs=[
            pl.BlockSpec((tm, D_MODEL), row),
            pl.BlockSpec((1, tm, D_MODEL), lambda i: (0, i, 0)),
            pl.BlockSpec((1, tm, D_MODEL), lambda i: (1, i, 0)),
            pl.BlockSpec((1, tm, 128), lambda i: (0, i, 0)),
            pl.BlockSpec((1, tm, 128), lambda i: (1, i, 0)),
            pl.BlockSpec((tm, GROUP), lambda i: (i, 2)),
            pl.BlockSpec((tm, 128), lambda i: (i, LORA_BLK + 2)),
            pl.BlockSpec((tm, GROUP), lambda i: (i, G_GA)),
            pl.BlockSpec((tm, GROUP), lambda i: (i, G_GB)),
            pl.BlockSpec((tm, D_MODEL), row),
            pl.BlockSpec((1, D_MODEL), full),
            pl.BlockSpec((1, D_MODEL), full),
            pl.BlockSpec((GATE_RANK, D_MODEL), full),
            pl.BlockSpec((128, D_MODEL), full),
            wspec, wspec, wspec,
        ],
        out_specs=pl.BlockSpec((tm, D_MODEL), row),
        out_shape=jax.ShapeDtypeStruct((t_total, D_MODEL), F32),
        scratch_shapes=[pltpu.VMEM((tm, D_MODEL), F32)],
        compiler_params=pltpu.CompilerParams(
            dimension_semantics=("arbitrary",), vmem_limit_bytes=VMEM_LIMIT),
        name="mix",
    )(x2d, y_dir, y_dir, s_dir, s_dir, proj, proj, proj, proj, o_attn,
      ln_w, ln_b, g_up, expand, proj_a, proj_b, w_out)


def _mlp_kernel(h_ref, gm_ref, gf_ref, w1_ref, w2_ref, o_ref, *, ff_chunk):
    h = h_ref[...]
    xn = _rms(h, gm_ref[...]).astype(BF16)
    acc = h
    for c in range(D_FF // ff_chunk):
        sl = slice(c * ff_chunk, (c + 1) * ff_chunk)
        hid = jnp.maximum(_dot(xn, w1_ref[:, sl]), 0.0)
        acc = acc + _dot((hid * hid).astype(BF16), w2_ref[sl, :])
    o_ref[...] = _rms(acc, gf_ref[...])


def _mlp(h2d, norm_mlp, norm_final, w1, w2):
    t_total = h2d.shape[0]
    tm = 256
    row = lambda i: (i, 0)
    full = lambda i: (0, 0)
    kern = functools.partial(_mlp_kernel, ff_chunk=1024)
    return pl.pallas_call(
        kern,
        grid=(t_total // tm,),
        in_specs=[
            pl.BlockSpec((tm, D_MODEL), row),
            pl.BlockSpec((1, D_MODEL), full),
            pl.BlockSpec((1, D_MODEL), full),
            pl.BlockSpec((D_MODEL, D_FF), full),
            pl.BlockSpec((D_FF, D_MODEL), full),
        ],
        out_specs=pl.BlockSpec((tm, D_MODEL), row),
        out_shape=jax.ShapeDtypeStruct((t_total, D_MODEL), F32),
        compiler_params=pltpu.CompilerParams(
            dimension_semantics=("arbitrary",), vmem_limit_bytes=VMEM_LIMIT),
        name="mlp",
    )(h2d, norm_mlp, norm_final, w1, w2)


def _rope_tables(seq):
    half = ROPE_DIM // 2
    pos = jnp.arange(seq, dtype=F32)
    inv_freq = ROPE_THETA ** (-jnp.arange(0, ROPE_DIM, 2, dtype=F32) / ROPE_DIM)
    ang = pos[:, None] * inv_freq[None, :]
    cos, sin = jnp.cos(ang), jnp.sin(ang)
    pad = jnp.zeros((seq, HEAD - ROPE_DIM), F32)
    z8 = jnp.zeros((seq, half), F32)
    c64 = jnp.concatenate([cos, cos, pad + 1.0], axis=1)
    sa64 = jnp.concatenate([z8, sin, pad], axis=1)
    sb64 = jnp.concatenate([-sin, z8, pad], axis=1)
    two = lambda t: jnp.concatenate([t, t], axis=1)
    return two(c64), two(sa64), two(sb64)


def _layer(x, l, prm, norm_final):
    batch, seq, _ = x.shape
    x2d = x.reshape(batch * seq, D_MODEL)
    cos_t, sa_t, sb_t = _rope_tables(seq)
    proj = _inproj(x2d, seq, prm["norm_mix"], prm["w_re"], prm["mu_re"], cos_t, sa_t, sb_t)
    y_dir, s_dir = _wkv_scan(proj, batch, seq, prm["wuh"], prm["wul"], prm["au"],
                             prm["w0"], prm["a0"], prm["k_k"], prm["k_a"], prm["r_k"], prm["seg"])
    lam_init = 0.8 - 0.6 * math.exp(-0.3 * l)
    o_attn = _diff_attn(proj, batch, seq, prm["lq1"], prm["lk1"], prm["lq2"], prm["lk2"],
                        prm["subln_w"], lam_init)
    h = _mix(x2d, proj, y_dir, s_dir, o_attn, prm["ln_w"], prm["ln_b"], prm["g_up"], prm["expand"],
             prm["proj_a"], prm["proj_b"], prm["w_out"])
    return h


def _prep_layer(l, w_in, mu_shift, w0, w_lora_up, a0, a_lora_up, g_lora_up, k_k, k_a, r_k, ln_x_w,
                ln_x_b, lambda_q1, lambda_k1, lambda_q2, lambda_k2, subln_w, proj_a, proj_b, w_out,
                norm_mix, norm_mlp, w_mlp_in, w_mlp_out):
    w = w_in[l]
    pad_cols = N_GROUPS * GROUP - w.shape[1]
    w_re = jnp.concatenate([w[:, :RWKV_MAIN], w[:, RWKV_COLS:], w[:, RWKV_MAIN:RWKV_COLS],
                            jnp.zeros((D_MODEL, pad_cols), F32)], axis=1).astype(BF16)
    mu = mu_shift[l]
    mu_re = jnp.concatenate([mu[:RWKV_MAIN], jnp.zeros((5 * GROUP,), F32), mu[RWKV_MAIN:],
                             jnp.zeros((pad_cols,), F32)])[None, :]

    def lora_pad(up):
        z = jnp.zeros_like(up[0])
        return jnp.stack([jnp.concatenate([up[0], z], axis=0), jnp.concatenate([z, up[1]], axis=0)])

    wu = lora_pad(w_lora_up[l])
    au = lora_pad(a_lora_up[l])
    wuh = wu.astype(BF16)
    head_of_lane = jnp.arange(D_MODEL) // HEAD
    seg = (head_of_lane[:, None] == jnp.arange(128)[None, :]).astype(BF16)
    return dict(
        w_re=w_re, mu_re=mu_re, norm_mix=norm_mix[l][None, :],
        wuh=wuh, wul=(wu - wuh.astype(F32)).astype(BF16),
        au=au.astype(BF16),
        w0=w0[l][:, None, :], a0=a0[l][:, None, :],
        k_k=k_k[l][None, :], k_a=k_a[l][None, :], r_k=r_k[l].reshape(1, D_MODEL),
        seg=seg, expand=seg.T,
        lq1=lambda_q1[l][None, :], lk1=lambda_k1[l][None, :],
        lq2=lambda_q2[l][None, :], lk2=lambda_k2[l][None, :],
        subln_w=subln_w[l][None, :],
        ln_w=ln_x_w[l][None, :], ln_b=ln_x_b[l][None, :],
        g_up=g_lora_up[l].astype(BF16),
        proj_a=proj_a[l].astype(BF16), proj_b=proj_b[l].astype(BF16), w_out=w_out[l].astype(BF16),
        norm_mlp=norm_mlp[l][None, :],
        w1=w_mlp_in[l].astype(BF16), w2=w_mlp_out[l].astype(BF16),
    )


def kernel(x_prompt, x_sample, w_in, mu_shift, w0, w_lora_up, a0, a_lora_up, g_lora_up, k_k, k_a, r_k, ln_x_w, ln_x_b, lambda_q1, lambda_k1, lambda_q2, lambda_k2, subln_w, proj_a, proj_b, w_out, norm_mix, norm_mlp, w_mlp_in, w_mlp_out, norm_final):
    depth = w_in.shape[0]
    assert depth == 1, "the final norm is fused into the (single) layer's MLP kernel"
    prm = _prep_layer(0, w_in, mu_shift, w0, w_lora_up, a0, a_lora_up, g_lora_up, k_k, k_a, r_k, ln_x_w,
                      ln_x_b, lambda_q1, lambda_k1, lambda_q2, lambda_k2, subln_w, proj_a, proj_b,
                      w_out, norm_mix, norm_mlp, w_mlp_in, w_mlp_out)
    outs = []
    for x in (x_prompt, x_sample):
        h = _layer(x, 0, prm, norm_final)
        y = _mlp(h, prm["norm_mlp"], norm_final[None, :], prm["w1"], prm["w2"])
        outs.append(y.reshape(x.shape))
    return tuple(outs)
```

```python
import functools
import math

import jax
import jax.numpy as jnp
from jax import lax
from jax.experimental import pallas as pl
from jax.experimental.pallas import tpu as pltpu

F32 = jnp.float32
BF16 = jnp.bfloat16

D_MODEL = 1024
HEAD = 64
N_PAIR = D_MODEL // 128
DECAY_RANK = 64
ICLR_RANK = 64
GATE_RANK = 128
GN_EPS = 64e-5
ROPE_THETA = 500000.0
ROPE_DIM = 16
SUBLN_EPS = 1e-5
NORM_EPS = 1e-6
D_FF = 4 * D_MODEL
RWKV_MAIN = 3 * D_MODEL
RWKV_COLS = RWKV_MAIN + 2 * DECAY_RANK + 2 * ICLR_RANK + GATE_RANK
GROUP = 1024
N_GROUPS = 9
G_Q, G_KB, G_VB, G_GA, G_GB, G_LORA = 3, 4, 5, 6, 7, 8
GROUPS_PER_STEP = 3
LORA_BLK = G_LORA * GROUP // 128
CHUNK = 64
EXP_M05 = math.exp(-0.5)
LOG2_E = math.log2(math.e)
VMEM_LIMIT = 56 * 1024 * 1024

NN = (((1,), (0,)), ((), ()))
NT = (((1,), (1,)), ((), ()))
TN = (((0,), (0,)), ((), ()))


def _dot(a, b, dims=NN):
    return lax.dot_general(a, b, dims, preferred_element_type=F32)


def _split(x):
    hi = x.astype(BF16)
    lo = (x - hi.astype(F32)).astype(BF16)
    return hi, lo


def _dot3(a, b, dims=NN):
    ah, al = _split(a)
    bh, bl = _split(b)
    return _dot(ah, bh, dims) + (_dot(ah, bl, dims) + _dot(al, bh, dims))


def _dot2_exact_rhs(a, b_bf16, dims=NN):
    ah, al = _split(a)
    return _dot(ah, b_bf16, dims) + _dot(al, b_bf16, dims)


def _sigmoid(x):
    return 1.0 / (1.0 + jnp.exp(-x))


def _rms(x, g):
    return x * lax.rsqrt(jnp.mean(x * x, axis=-1, keepdims=True) + NORM_EPS) * g


def _inproj_kernel(x_ref, xp_ref, xn_ref, g_ref, w_ref, mu_ref, cos_ref, sa_ref, sb_ref,
                   o_ref, xs_ref, hs_ref, *, tm, tiles_per_seq):
    j = pl.program_id(0)
    i = pl.program_id(1)
    g = g_ref[...]
    xs_ref[...] = _rms(x_ref[...], g).astype(BF16)
    hs_ref[0:8, :] = _rms(xp_ref[...], g)
    hs_ref[8:16, :] = _rms(xn_ref[...], g)

    def store_shifted(p, w, cols):
        ph = _dot(hs_ref[...].astype(BF16), w)
        t_in_seq = i % tiles_per_seq
        prev = jnp.where(t_in_seq == 0, 0.0, ph[7:8, :])
        nxt = jnp.where(t_in_seq == tiles_per_seq - 1, 0.0, ph[8:9, :])
        c2 = 0.5 * mu_ref[:, cols]
        c1 = 1.0 - mu_ref[:, cols]
        core = p * c1 + (pltpu.roll(p, 1, 0) + pltpu.roll(p, tm - 1, 0)) * c2
        o_ref[:, cols] = core
        o_ref[0:1, cols] = core[0:1, :] + (prev - p[tm - 1:tm, :]) * c2
        o_ref[tm - 1:tm, cols] = core[tm - 1:tm, :] + (nxt - p[0:1, :]) * c2

    def rotated(p, scale):
        c = jnp.tile(cos_ref[...], (1, N_PAIR))
        sa = jnp.tile(sa_ref[...], (1, N_PAIR))
        sb = jnp.tile(sb_ref[...], (1, N_PAIR))
        half = ROPE_DIM // 2
        out = p * c + pltpu.roll(p, half, 1) * sa + pltpu.roll(p, GROUP - half, 1) * sb
        return out if scale == 1.0 else out * scale

    for step in range(N_GROUPS // GROUPS_PER_STEP):
        @pl.when(j == step)
        def _(step=step):
            for gi in range(GROUPS_PER_STEP):
                g = step * GROUPS_PER_STEP + gi
                cols = slice(gi * GROUP, (gi + 1) * GROUP)
                w = w_ref[:, cols]
                p = _dot(xs_ref[...], w)
                if g < 3 or g == G_LORA:
                    store_shifted(p, w, cols)
                elif g == G_Q:
                    o_ref[:, cols] = rotated(p, HEAD ** -0.5)
                elif g == G_KB:
                    o_ref[:, cols] = rotated(p, 1.0)
                else:
                    o_ref[:, cols] = p


def _inproj(x2d, seq, g, w_re, mu_re, cos_t, sa_t, sb_t):
    t_total = x2d.shape[0]
    tm = min(512, seq)
    tiles_per_seq = seq // tm
    n_tiles = t_total // tm
    last8 = t_total // 8 - 1
    kern = functools.partial(_inproj_kernel, tm=tm, tiles_per_seq=tiles_per_seq)
    return pl.pallas_call(
        kern,
        grid=(N_GROUPS // GROUPS_PER_STEP, n_tiles),
        in_specs=[
            pl.BlockSpec((tm, D_MODEL), lambda j, i: (i, 0)),
            pl.BlockSpec((8, D_MODEL), lambda j, i: (jnp.maximum(i * (tm // 8) - 1, 0), 0)),
            pl.BlockSpec((8, D_MODEL), lambda j, i: (jnp.minimum((i + 1) * (tm // 8), last8), 0)),
            pl.BlockSpec((1, D_MODEL), lambda j, i: (0, 0)),
            pl.BlockSpec((D_MODEL, GROUPS_PER_STEP * GROUP), lambda j, i: (0, j)),
            pl.BlockSpec((1, GROUPS_PER_STEP * GROUP), lambda j, i: (0, j)),
            pl.BlockSpec((tm, 128), lambda j, i: (i % tiles_per_seq, 0)),
            pl.BlockSpec((tm, 128), lambda j, i: (i % tiles_per_seq, 0)),
            pl.BlockSpec((tm, 128), lambda j, i: (i % tiles_per_seq, 0)),
        ],
        out_specs=pl.BlockSpec((tm, GROUPS_PER_STEP * GROUP), lambda j, i: (i, j)),
        out_shape=jax.ShapeDtypeStruct((t_total, N_GROUPS * GROUP), F32),
        scratch_shapes=[pltpu.VMEM((tm, D_MODEL), BF16), pltpu.VMEM((16, D_MODEL), F32)],
        compiler_params=pltpu.CompilerParams(
            dimension_semantics=("arbitrary", "arbitrary"), vmem_limit_bytes=VMEM_LIMIT),
        name="inproj",
    )(x2d, x2d, x2d, g, w_re, mu_re, cos_t, sa_t, sb_t)


QUAD = 256
N_QUAD = D_MODEL // QUAD
HEADS_PER_QUAD = QUAD // HEAD
SCAN_SUB = 8


def _scan_kernel(r_ref, k_ref, v_ref, lw_ref, la_ref, wuh_ref, wul_ref, au_ref,
                 w0_ref, a0_ref, kk_ref, ka_ref, rk_ref, seg_ref, y_ref, s_ref, h_ref, *, n_sub):
    d = pl.program_id(1)
    c = pl.program_id(2)
    C = CHUNK
    bf = lambda x: x.astype(BF16)

    @pl.when(c == 0)
    def _():
        h_ref[...] = jnp.zeros_like(h_ref)

    sgn = 1 - 2 * d
    rr = lax.broadcasted_iota(jnp.int32, (C, C), 0)
    cc = lax.broadcasted_iota(jnp.int32, (C, C), 1)
    tri = jnp.where((rr - cc) * sgn >= 0, 1.0, 0.0).astype(BF16)

    tr = lax.broadcasted_iota(jnp.int32, (C, QUAD), 0)
    tc = lax.broadcasted_iota(jnp.int32, (C, QUAD), 1) % C
    dd = (tr - tc) * sgn
    strict = dd > 0
    incl = dd >= 0
    eye = jnp.where(dd == 0, 1.0, 0.0)
    levels = []
    m = 1
    while m < C:
        levels.append(jnp.logical_and((tr // (2 * m)) == (tc // (2 * m)), (tr // m) != (tc // m)))
        m *= 2
    lane_head = lax.broadcasted_iota(jnp.int32, (1, QUAD), 1) // HEAD
    head_lanes = [lane_head == h for h in range(HEADS_PER_QUAD)]
    same_head = (lax.broadcasted_iota(jnp.int32, (QUAD, QUAD), 0) // HEAD) == (
        lax.broadcasted_iota(jnp.int32, (QUAD, QUAD), 1) // HEAD)
    bd_ones = jnp.where(same_head, 1.0, 0.0).astype(BF16)

    def blockdiag(x):
        return jnp.concatenate([jnp.where(hl, x, 0.0) for hl in head_lanes], axis=0)

    quads = range(N_QUAD)
    sls = [slice(QUAD * q, QUAD * (q + 1)) for q in quads]

    staged = []
    for i in range(n_sub):
        jj = i + d * (n_sub - 1 - 2 * i)
        rows = pl.ds(pl.multiple_of(jj * C, C), C)
        r = r_ref[rows, :]
        k = k_ref[rows, :]
        v = v_ref[rows, :]

        thh, thl = _split(jnp.tanh(lw_ref[rows, :]))
        w_raw = w0_ref[0] + (_dot(thh, wuh_ref[0]) + (_dot(thh, wul_ref[0]) + _dot(thl, wuh_ref[0])))
        logw = (-EXP_M05 * LOG2_E) * _sigmoid(w_raw)
        a = _sigmoid(a0_ref[0] + _dot(bf(la_ref[rows, :]), au_ref[0]))

        l1, l2 = _split(logw)
        cum = _dot(tri, l1) + _dot(tri, l2)
        total = jnp.sum(logw, axis=0, keepdims=True)
        g_incl = jnp.exp2(cum)
        g_excl = jnp.exp2(cum - logw)
        g_inv = jnp.exp2(-cum)
        g_tail = jnp.exp2(total - cum)
        g_tot = jnp.exp2(total)

        kkv = k * kk_ref[...]
        kd = k * (1.0 + (a - 1.0) * ka_ref[...])
        sq = kkv * kkv

        s_ref[0, rows, :] = _dot(bf(r * kd * rk_ref[...]), seg_ref[...])

        ss = [_dot(bf(sq[:, sl]), bd_ones) for sl in sls]
        kkn = [kkv[:, sl] * lax.rsqrt(jnp.maximum(s, 1e-24)) for sl, s in zip(sls, ss)]
        bvec = [kn * a[:, sl] for sl, kn in zip(sls, kkn)]
        lhs = [bf(jnp.concatenate([-kn * g_excl[:, sl], r[:, sl] * g_incl[:, sl]], axis=0))
               for sl, kn in zip(sls, kkn)]
        rhs = [jnp.concatenate([blockdiag(bf(bv * g_inv[:, sl])), blockdiag(bf(kd[:, sl] * g_inv[:, sl]))], axis=0)
               for sl, bv in zip(sls, bvec)]
        bkh = [bf(jnp.concatenate([bv * g_tail[:, sl], kd[:, sl] * g_tail[:, sl]], axis=0))
               for sl, bv in zip(sls, bvec)]
        vb = [bf(v[:, sl]) for sl in sls]
        v_bd = [blockdiag(x) for x in vb]

        gm = [_dot(lhs[q], rhs[q], NT) for q in quads]
        gb = [bf(g) for g in gm]
        a_ab = [jnp.where(strict, g[0:C, 0:QUAD], 0.0) for g in gm]
        a_abb = [jnp.where(strict, g[0:C, 0:QUAD], 0.0) for g in gb]
        a_ak = [jnp.where(strict, g[0:C, QUAD:2 * QUAD], 0.0) for g in gb]
        a_r = [jnp.concatenate([jnp.where(incl, g[C:2 * C, 0:QUAD], 0.0),
                                jnp.where(incl, g[C:2 * C, QUAD:2 * QUAD], 0.0)], axis=1) for g in gb]
        akv = [_dot(a_ak[q], v_bd[q]) for q in quads]
        staged.append(dict(rows=rows, lhs=lhs, a_ab=a_ab, a_abb=a_abb, a_r=a_r, akv=akv, vb=vb, v_bd=v_bd,
                           bkh=bkh, g_tot=g_tot))

    combos = [(i, q) for i in range(n_sub) for q in quads]
    tinv = {(i, q): eye + jnp.where(levels[0], staged[i]["a_ab"][q], 0.0) for i, q in combos}
    for lvl in levels[1:]:
        tb = {key: bf(t) for key, t in tinv.items()}
        wm = {(i, q): _dot(jnp.where(lvl, staged[i]["a_abb"][q], 0.0), blockdiag(tb[i, q])) for i, q in combos}
        tinv = {key: tinv[key] + _dot(tb[key], blockdiag(bf(wm[key]))) for key in combos}

    ht = [h_ref[q] for q in quads]
    for i in range(n_sub):
        st = staged[i]
        ar = [_dot(st["lhs"][q], bf(ht[q]), NT) for q in quads]
        rhs_u = [ar[q][0:C] + st["akv"][q] for q in quads]
        u = [bf(_dot(bf(tinv[i, q]), blockdiag(bf(rhs_u[q])))) for q in quads]
        uv_bd = [jnp.concatenate([blockdiag(u[q]), st["v_bd"][q]], axis=0) for q in quads]
        for q in quads:
            y_ref[0, st["rows"], sls[q]] = (ar[q][C:2 * C] + _dot(st["a_r"][q], uv_bd[q])).astype(y_ref.dtype)
        upd = [_dot(jnp.concatenate([u[q], st["vb"][q]], axis=0), st["bkh"][q], TN) for q in quads]
        ht = [ht[q] * st["g_tot"][:, sls[q]] + jnp.where(same_head, upd[q], 0.0) for q in quads]

    for q in quads:
        h_ref[q] = ht[q]


def _wkv_scan(proj, batch, seq, wuh, wul, au, w0, a0, k_k, k_a, r_k, seg):
    t_total = batch * seq
    n_sub = math.gcd(seq // CHUNK, SCAN_SUB)
    blk = n_sub * CHUNK
    nc = seq // blk

    def rowblk(b, d, c):
        return b * nc + c + d * (nc - 1 - 2 * c)

    def col(jblk):
        return lambda b, d, c: (rowblk(b, d, c), jblk)

    full = lambda b, d, c: (0, 0)
    perdir = lambda b, d, c: (d, 0, 0)
    return pl.pallas_call(
        functools.partial(_scan_kernel, n_sub=n_sub),
        grid=(batch, 2, nc),
        in_specs=[
            pl.BlockSpec((blk, GROUP), col(0)),
            pl.BlockSpec((blk, GROUP), col(1)),
            pl.BlockSpec((blk, GROUP), col(2)),
            pl.BlockSpec((blk, 128), col(LORA_BLK)),
            pl.BlockSpec((blk, 128), col(LORA_BLK + 1)),
            pl.BlockSpec((1, 128, D_MODEL), perdir),
            pl.BlockSpec((1, 128, D_MODEL), perdir),
            pl.BlockSpec((1, 128, D_MODEL), perdir),
            pl.BlockSpec((1, 1, D_MODEL), perdir),
            pl.BlockSpec((1, 1, D_MODEL), perdir),
            pl.BlockSpec((1, D_MODEL), full),
            pl.BlockSpec((1, D_MODEL), full),
            pl.BlockSpec((1, D_MODEL), full),
            pl.BlockSpec((D_MODEL, 128), full),
        ],
        out_specs=[
            pl.BlockSpec((1, blk, D_MODEL), lambda b, d, c: (d, rowblk(b, d, c), 0)),
            pl.BlockSpec((1, blk, 128), lambda b, d, c: (d, rowblk(b, d, c), 0)),
        ],
        out_shape=[
            jax.ShapeDtypeStruct((2, t_total, D_MODEL), BF16),
            jax.ShapeDtypeStruct((2, t_total, 128), F32),
        ],
        scratch_shapes=[pltpu.VMEM((N_QUAD, QUAD, QUAD), F32)],
        compiler_params=pltpu.CompilerParams(
            dimension_semantics=("arbitrary", "arbitrary", "arbitrary"), vmem_limit_bytes=VMEM_LIMIT),
        name="wkv_scan",
    )(proj, proj, proj, proj, proj, wuh, wul, au, w0, a0, k_k, k_a, r_k, seg)


V_ROWS = 128 + 16
PV_GROUP = 2


def _attn_kernel(q_ref, k_ref, v_ref, lq1_ref, lk1_ref, lq2_ref, lk2_ref, sw_ref, o_ref,
                 k0_ref, k1_ref, vt_ref, s_ref, m_ref, p_ref, *, seq, tq, tk, per_trip, nq, lam_init):
    i = pl.program_id(2)
    n_kc = seq // tk
    lane = lax.broadcasted_iota(jnp.int32, (1, 128), 1)
    first = lane < HEAD

    def prepare(rows):
        kf = k_ref[rows, :]
        k0_ref[rows, :] = jnp.where(first, kf, 0.0).astype(BF16)
        k1_ref[rows, :] = jnp.where(first, 0.0, kf).astype(BF16)
        vt_ref[0:128, rows] = v_ref[rows, :].T.astype(BF16)
        vt_ref[128:V_ROWS, rows] = jnp.ones((V_ROWS - 128, tk), BF16)

    def rows_of(kc):
        return pl.ds(pl.multiple_of(kc * tk, tk), tk)

    def fold(op, s):
        return op(s.reshape(tk // 8, 8, tq), axis=0)

    def query_t():
        return (q_ref[...] * LOG2_E).T.astype(BF16)

    neg = jnp.full((8, tq), -jnp.inf, F32)
    za = jnp.zeros((V_ROWS, tq), F32)
    n_trips = n_kc // per_trip

    def finish(acc0, acc1):
        l0 = acc0[128:129, :]
        l1 = acc1[128:129, :]
        lam = (jnp.exp(jnp.sum(lq1_ref[...] * lk1_ref[...], axis=-1, keepdims=True))
               - jnp.exp(jnp.sum(lq2_ref[...] * lk2_ref[...], axis=-1, keepdims=True)) + lam_init)
        o_t = acc0[0:128, :] / l0 - lam * (acc1[0:128, :] / l1)
        o = o_t.T
        o = o * lax.rsqrt(jnp.mean(o * o, axis=-1, keepdims=True) + SUBLN_EPS) * sw_ref[...]
        o_ref[...] = (o * (1.0 - lam_init)).astype(o_ref.dtype)

    def run(slot_a):
        slot_b = 1 - slot_a
        mine = (i % 2) == slot_a

        def trip(t, q_t, mx, m_prev, acc, first_tile=False):
            for c in range(per_trip):
                rows = rows_of(t * per_trip + c)
                if first_tile:
                    prepare(rows)
                if q_t is not None:
                    s0 = _dot(k0_ref[rows, :], q_t)
                    s1 = _dot(k1_ref[rows, :], q_t)
                    s_ref[slot_a, 0, rows, :] = s0
                    s_ref[slot_a, 1, rows, :] = s1
                    mx = (jnp.maximum(mx[0], fold(jnp.max, s0)), jnp.maximum(mx[1], fold(jnp.max, s1)))
                if acc is not None:
                    for comp in range(2):
                        x = s_ref[slot_b, comp, rows, :] - m_prev[comp]
                        p_ref[comp, c * tk:(c + 1) * tk, :] = jnp.exp2(x.astype(BF16))
                    if (c + 1) % PV_GROUP == 0 or c + 1 == per_trip:
                        c0 = (c // PV_GROUP) * PV_GROUP
                        span = (c + 1 - c0) * tk
                        start = (t * per_trip + c0) * tk
                        vt = vt_ref[:, pl.ds(pl.multiple_of(start, tk), span)]
                        acc = tuple(acc[comp] + _dot(vt, p_ref[comp, c0 * tk:(c + 1) * tk, :])
                                    for comp in range(2))
            return mx, acc

        def loop(body, init):
            if n_trips == 1:
                return body(0, init)
            return lax.fori_loop(0, n_trips, body, init)

        def prev_max():
            return (jnp.max(m_ref[slot_b, 0], axis=0, keepdims=True),
                    jnp.max(m_ref[slot_b, 1], axis=0, keepdims=True))

        def store_max(mx):
            m_ref[slot_a, 0] = mx[0]
            m_ref[slot_a, 1] = mx[1]

        if slot_a == 0:
            @pl.when(i == 0)
            def _():
                q_t = query_t()
                store_max(loop(lambda t, mx: trip(t, q_t, mx, None, None, first_tile=True)[0], (neg, neg)))

        @pl.when(jnp.logical_and(mine, jnp.logical_and(i > 0, i < nq)))
        def _():
            q_t = query_t()
            m_prev = prev_max()
            mx, acc = loop(lambda t, c: trip(t, q_t, c[0], m_prev, c[1]), ((neg, neg), (za, za)))
            store_max(mx)
            finish(*acc)

        if nq % 2 == slot_a:
            @pl.when(i == nq)
            def _():
                m_prev = prev_max()
                finish(*loop(lambda t, acc: trip(t, None, None, m_prev, acc)[1], (za, za)))

    run(0)
    run(1)


def _diff_attn(proj, batch, seq, lq1, lk1, lq2, lk2, subln_w, lam_init):
    t_total = batch * seq
    tq = min(512, seq)
    tk = min(256, seq)
    per_trip = math.gcd(seq // tk, 16)
    nq = seq // tq
    n_heads = N_PAIR
    qb, kb, vb = (G_Q * GROUP // 128, G_KB * GROUP // 128, G_VB * GROUP // 128)
    kern = functools.partial(_attn_kernel, seq=seq, tq=tq, tk=tk, per_trip=per_trip, nq=nq, lam_init=lam_init)
    small = lambda b, h, i: (0, 0)
    return pl.pallas_call(
        kern,
        grid=(batch, n_heads, nq + 1),
        in_specs=[
            pl.BlockSpec((tq, 128), lambda b, h, i: (b * nq + jnp.minimum(i, nq - 1), qb + h)),
            pl.BlockSpec((seq, 128), lambda b, h, i: (b, kb + h)),
            pl.BlockSpec((seq, 128), lambda b, h, i: (b, vb + h)),
            pl.BlockSpec((1, HEAD), small),
            pl.BlockSpec((1, HEAD), small),
            pl.BlockSpec((1, HEAD), small),
            pl.BlockSpec((1, HEAD), small),
            pl.BlockSpec((1, 128), small),
        ],
        out_specs=pl.BlockSpec((tq, 128), lambda b, h, i: (b * nq + jnp.maximum(i - 1, 0), h)),
        out_shape=jax.ShapeDtypeStruct((t_total, D_MODEL), BF16),
        scratch_shapes=[pltpu.VMEM((seq, 128), BF16), pltpu.VMEM((seq, 128), BF16),
                        pltpu.VMEM((V_ROWS, seq), BF16), pltpu.VMEM((2, 2, seq, tq), F32),
                        pltpu.VMEM((2, 2, 8, tq), F32), pltpu.VMEM((2, per_trip * tk, tq), BF16)],
        compiler_params=pltpu.CompilerParams(
            dimension_semantics=("arbitrary", "arbitrary", "arbitrary"), vmem_limit_bytes=VMEM_LIMIT),
        name="diff_attn",
    )(proj, proj, proj, lq1, lk1, lq2, lk2, subln_w)


def _mix_kernel(x_ref, yf_ref, yb_ref, sf_ref, sb_ref, v_ref, lg_ref, ga_ref, gb_ref, o_ref,
                lnw_ref, lnb_ref, gup_ref, exp_ref, pa_ref, pb_ref, wo_ref, h_ref, ya_ref):
    y = yf_ref[0].astype(F32) + yb_ref[0].astype(F32)
    rh = lax.broadcasted_iota(jnp.int32, (QUAD, QUAD), 0) // HEAD
    ch = lax.broadcasted_iota(jnp.int32, (QUAD, QUAD), 1) // HEAD
    avg = jnp.where(rh == ch, 1.0 / HEAD, 0.0).astype(BF16)
    for q in range(N_QUAD):
        sl = slice(QUAD * q, QUAD * (q + 1))
        yq = y[:, sl]
        cen = yq - _dot(yq.astype(BF16), avg)
        var = _dot((cen * cen).astype(BF16), avg)
        ya_ref[:, sl] = cen * lax.rsqrt(var + GN_EPS)
    coef = _dot((sf_ref[0] + sb_ref[0]).astype(BF16), exp_ref[...])
    gate = _dot(_sigmoid(lg_ref[...]).astype(BF16), gup_ref[...])
    y_a = (ya_ref[...] * lnw_ref[...] + lnb_ref[...] + coef * v_ref[...]) * gate
    pa = _dot(y_a.astype(BF16), pa_ref[...])
    pb = _dot(o_ref[...], pb_ref[...])
    merged = _sigmoid(ga_ref[...]) * pa + _sigmoid(gb_ref[...]) * pb
    h_ref[...] = x_ref[...] + _dot(merged.astype(BF16), wo_ref[...])


def _mix(x2d, proj, y_dir, s_dir, o_attn, ln_w, ln_b, g_up, expand, proj_a, proj_b, w_out):
    t_total = x2d.shape[0]
    tm = 256
    row = lambda i: (i, 0)
    full = lambda i: (0, 0)
    wspec = pl.BlockSpec((D_MODEL, D_MODEL), full)
    return pl.pallas_call(
        _mix_kernel,
        grid=(t_total // tm,),
        in_specs=[
            pl.BlockSpec((tm, D_MODEL), row),
            pl.BlockSpec((1, tm, D_MODEL), lambda i: (0, i, 0)),
            pl.BlockSpec((1, tm, D_MODEL), lambda i: (1, i, 0)),
            pl.BlockSpec((1, tm, 128), lambda i: (0, i, 0)),
            pl.BlockSpec((1, tm, 128), lambda i: (1, i, 0)),
            pl.BlockSpec((tm, GROUP), lambda i: (i, 2)),
            pl.BlockSpec((tm, 128), lambda i: (i, LORA_BLK + 2)),
            pl.BlockSpec((tm, GROUP), lambda i: (i, G_GA)),
            pl.BlockSpec((tm, GROUP), lambda i: (i, G_GB)),
            pl.BlockSpec((tm, D_MODEL), row),
            pl.BlockSpec((1, D_MODEL), full),
            pl.BlockSpec((1, D_MODEL), full),
            pl.BlockSpec((GATE_RANK, D_MODEL), full),
            pl.BlockSpec((128, D_MODEL), full),
            wspec, wspec, wspec,
        ],
        out_specs=pl.BlockSpec((tm, D_MODEL), row),
        out_shape=jax.ShapeDtypeStruct((t_total, D_MODEL), F32),
        scratch_shapes=[pltpu.VMEM((tm, D_MODEL), F32)],
        compiler_params=pltpu.CompilerParams(
            dimension_semantics=("arbitrary",), vmem_limit_bytes=VMEM_LIMIT),
        name="mix",
    )(x2d, y_dir, y_dir, s_dir, s_dir, proj, proj, proj, proj, o_attn,
      ln_w, ln_b, g_up, expand, proj_a, proj_b, w_out)


def _mlp_kernel(h_ref, gm_ref, gf_ref, w1_ref, w2_ref, o_ref, *, ff_chunk):
    h = h_ref[...]
    xn = _rms(h, gm_ref[...]).astype(BF16)
    acc = h
    for c in range(D_FF // ff_chunk):
        sl = slice(c * ff_chunk, (c + 1) * ff_chunk)
        hid = jnp.maximum(_dot(xn, w1_ref[:, sl]), 0.0)
        acc = acc + _dot((hid * hid).astype(BF16), w2_ref[sl, :])
    o_ref[...] = _rms(acc, gf_ref[...])


def _mlp(h2d, norm_mlp, norm_final, w1, w2):
    t_total = h2d.shape[0]
    tm = 256
    row = lambda i: (i, 0)
    full = lambda i: (0, 0)
    kern = functools.partial(_mlp_kernel, ff_chunk=1024)
    return pl.pallas_call(
        kern,
        grid=(t_total // tm,),
        in_specs=[
            pl.BlockSpec((tm, D_MODEL), row),
            pl.BlockSpec((1, D_MODEL), full),
            pl.BlockSpec((1, D_MODEL), full),
            pl.BlockSpec((D_MODEL, D_FF), full),
            pl.BlockSpec((D_FF, D_MODEL), full),
        ],
        out_specs=pl.BlockSpec((tm, D_MODEL), row),
        out_shape=jax.ShapeDtypeStruct((t_total, D_MODEL), F32),
        compiler_params=pltpu.CompilerParams(
            dimension_semantics=("arbitrary",), vmem_limit_bytes=VMEM_LIMIT),
        name="mlp",
    )(h2d, norm_mlp, norm_final, w1, w2)


def _rope_tables(seq):
    half = ROPE_DIM // 2
    pos = jnp.arange(seq, dtype=F32)
    inv_freq = ROPE_THETA ** (-jnp.arange(0, ROPE_DIM, 2, dtype=F32) / ROPE_DIM)
    ang = pos[:, None] * inv_freq[None, :]
    cos, sin = jnp.cos(ang), jnp.sin(ang)
    pad = jnp.zeros((seq, HEAD - ROPE_DIM), F32)
    z8 = jnp.zeros((seq, half), F32)
    c64 = jnp.concatenate([cos, cos, pad + 1.0], axis=1)
    sa64 = jnp.concatenate([z8, sin, pad], axis=1)
    sb64 = jnp.concatenate([-sin, z8, pad], axis=1)
    two = lambda t: jnp.concatenate([t, t], axis=1)
    return two(c64), two(sa64), two(sb64)


def _layer(x, l, prm, norm_final):
    batch, seq, _ = x.shape
    x2d = x.reshape(batch * seq, D_MODEL)
    cos_t, sa_t, sb_t = _rope_tables(seq)
    proj = _inproj(x2d, seq, prm["norm_mix"], prm["w_re"], prm["mu_re"], cos_t, sa_t, sb_t)
    y_dir, s_dir = _wkv_scan(proj, batch, seq, prm["wuh"], prm["wul"], prm["au"],
                             prm["w0"], prm["a0"], prm["k_k"], prm["k_a"], prm["r_k"], prm["seg"])
    lam_init = 0.8 - 0.6 * math.exp(-0.3 * l)
    o_attn = _diff_attn(proj, batch, seq, prm["lq1"], prm["lk1"], prm["lq2"], prm["lk2"],
                        prm["subln_w"], lam_init)
    h = _mix(x2d, proj, y_dir, s_dir, o_attn, prm["ln_w"], prm["ln_b"], prm["g_up"], prm["expand"],
             prm["proj_a"], prm["proj_b"], prm["w_out"])
    return h


def _prep_layer(l, w_in, mu_shift, w0, w_lora_up, a0, a_lora_up, g_lora_up, k_k, k_a, r_k, ln_x_w,
                ln_x_b, lambda_q1, lambda_k1, lambda_q2, lambda_k2, subln_w, proj_a, proj_b, w_out,
                norm_mix, norm_mlp, w_mlp_in, w_mlp_out):
    w = w_in[l]
    pad_cols = N_GROUPS * GROUP - w.shape[1]
    w_re = jnp.concatenate([w[:, :RWKV_MAIN], w[:, RWKV_COLS:], w[:, RWKV_MAIN:RWKV_COLS],
                            jnp.zeros((D_MODEL, pad_cols), F32)], axis=1).astype(BF16)
    mu = mu_shift[l]
    mu_re = jnp.concatenate([mu[:RWKV_MAIN], jnp.zeros((5 * GROUP,), F32), mu[RWKV_MAIN:],
                             jnp.zeros((pad_cols,), F32)])[None, :]

    def lora_pad(up):
        z = jnp.zeros_like(up[0])
        return jnp.stack([jnp.concatenate([up[0], z], axis=0), jnp.concatenate([z, up[1]], axis=0)])

    wu = lora_pad(w_lora_up[l])
    au = lora_pad(a_lora_up[l])
    wuh = wu.astype(BF16)
    head_of_lane = jnp.arange(D_MODEL) // HEAD
    seg = (head_of_lane[:, None] == jnp.arange(128)[None, :]).astype(BF16)
    return dict(
        w_re=w_re, mu_re=mu_re, norm_mix=norm_mix[l][None, :],
        wuh=wuh, wul=(wu - wuh.astype(F32)).astype(BF16),
        au=au.astype(BF16),
        w0=w0[l][:, None, :], a0=a0[l][:, None, :],
        k_k=k_k[l][None, :], k_a=k_a[l][None, :], r_k=r_k[l].reshape(1, D_MODEL),
        seg=seg, expand=seg.T,
        lq1=lambda_q1[l][None, :], lk1=lambda_k1[l][None, :],
        lq2=lambda_q2[l][None, :], lk2=lambda_k2[l][None, :],
        subln_w=subln_w[l][None, :],
        ln_w=ln_x_w[l][None, :], ln_b=ln_x_b[l][None, :],
        g_up=g_lora_up[l].astype(BF16),
        proj_a=proj_a[l].astype(BF16), proj_b=proj_b[l].astype(BF16), w_out=w_out[l].astype(BF16),
        norm_mlp=norm_mlp[l][None, :],
        w1=w_mlp_in[l].astype(BF16), w2=w_mlp_out[l].astype(BF16),
    )


def kernel(x_prompt, x_sample, w_in, mu_shift, w0, w_lora_up, a0, a_lora_up, g_lora_up, k_k, k_a, r_k, ln_x_w, ln_x_b, lambda_q1, lambda_k1, lambda_q2, lambda_k2, subln_w, proj_a, proj_b, w_out, norm_mix, norm_mlp, w_mlp_in, w_mlp_out, norm_final):
    depth = w_in.shape[0]
    assert depth == 1, "the final norm is fused into the (single) layer's MLP kernel"
    prm = _prep_layer(0, w_in, mu_shift, w0, w_lora_up, a0, a_lora_up, g_lora_up, k_k, k_a, r_k, ln_x_w,
                      ln_x_b, lambda_q1, lambda_k1, lambda_q2, lambda_k2, subln_w, proj_a, proj_b,
                      w_out, norm_mix, norm_mlp, w_mlp_in, w_mlp_out)
    outs = []
    for x in (x_prompt, x_sample):
        h = _layer(x, 0, prm, norm_final)
        y = _mlp(h, prm["norm_mlp"], norm_final[None, :], prm["w1"], prm["w2"])
        outs.append(y.reshape(x.shape))
    return tuple(outs)
```

```python
import functools
import math

import jax
import jax.numpy as jnp
from jax import lax
from jax.experimental import pallas as pl
from jax.experimental.pallas import tpu as pltpu

F32 = jnp.float32
BF16 = jnp.bfloat16

D_MODEL = 1024
HEAD = 64
N_PAIR = D_MODEL // 128
DECAY_RANK = 64
ICLR_RANK = 64
GATE_RANK = 128
GN_EPS = 64e-5
ROPE_THETA = 500000.0
ROPE_DIM = 16
SUBLN_EPS = 1e-5
NORM_EPS = 1e-6
D_FF = 4 * D_MODEL
RWKV_MAIN = 3 * D_MODEL
RWKV_COLS = RWKV_MAIN + 2 * DECAY_RANK + 2 * ICLR_RANK + GATE_RANK
GROUP = 1024
N_GROUPS = 9
G_Q, G_KB, G_VB, G_GA, G_GB, G_LORA = 3, 4, 5, 6, 7, 8
GROUPS_PER_STEP = 3
LORA_BLK = G_LORA * GROUP // 128
CHUNK = 64
EXP_M05 = math.exp(-0.5)
LOG2_E = math.log2(math.e)
VMEM_LIMIT = 56 * 1024 * 1024

NN = (((1,), (0,)), ((), ()))
NT = (((1,), (1,)), ((), ()))
TN = (((0,), (0,)), ((), ()))


def _dot(a, b, dims=NN):
    return lax.dot_general(a, b, dims, preferred_element_type=F32)


def _split(x):
    hi = x.astype(BF16)
    lo = (x - hi.astype(F32)).astype(BF16)
    return hi, lo


def _dot3(a, b, dims=NN):
    ah, al = _split(a)
    bh, bl = _split(b)
    return _dot(ah, bh, dims) + (_dot(ah, bl, dims) + _dot(al, bh, dims))


def _dot2_exact_rhs(a, b_bf16, dims=NN):
    ah, al = _split(a)
    return _dot(ah, b_bf16, dims) + _dot(al, b_bf16, dims)


def _sigmoid(x):
    return 1.0 / (1.0 + jnp.exp(-x))


def _rms(x, g):
    return x * lax.rsqrt(jnp.mean(x * x, axis=-1, keepdims=True) + NORM_EPS) * g


def _inproj_kernel(x_ref, xp_ref, xn_ref, g_ref, w_ref, mu_ref, cos_ref, sa_ref, sb_ref,
                   o_ref, xs_ref, hs_ref, *, tm, tiles_per_seq):
    j = pl.program_id(0)
    i = pl.program_id(1)
    g = g_ref[...]
    xs_ref[...] = _rms(x_ref[...], g).astype(BF16)
    hs_ref[0:8, :] = _rms(xp_ref[...], g)
    hs_ref[8:16, :] = _rms(xn_ref[...], g)

    def store_shifted(p, w, cols):
        ph = _dot(hs_ref[...].astype(BF16), w)
        t_in_seq = i % tiles_per_seq
        prev = jnp.where(t_in_seq == 0, 0.0, ph[7:8, :])
        nxt = jnp.where(t_in_seq == tiles_per_seq - 1, 0.0, ph[8:9, :])
        c2 = 0.5 * mu_ref[:, cols]
        c1 = 1.0 - mu_ref[:, cols]
        core = p * c1 + (pltpu.roll(p, 1, 0) + pltpu.roll(p, tm - 1, 0)) * c2
        o_ref[:, cols] = core
        o_ref[0:1, cols] = core[0:1, :] + (prev - p[tm - 1:tm, :]) * c2
        o_ref[tm - 1:tm, cols] = core[tm - 1:tm, :] + (nxt - p[0:1, :]) * c2

    def rotated(p, scale):
        c = jnp.tile(cos_ref[...], (1, N_PAIR))
        sa = jnp.tile(sa_ref[...], (1, N_PAIR))
        sb = jnp.tile(sb_ref[...], (1, N_PAIR))
        half = ROPE_DIM // 2
        out = p * c + pltpu.roll(p, half, 1) * sa + pltpu.roll(p, GROUP - half, 1) * sb
        return out if scale == 1.0 else out * scale

    for step in range(N_GROUPS // GROUPS_PER_STEP):
        @pl.when(j == step)
        def _(step=step):
            for gi in range(GROUPS_PER_STEP):
                g = step * GROUPS_PER_STEP + gi
                cols = slice(gi * GROUP, (gi + 1) * GROUP)
                w = w_ref[:, cols]
                p = _dot(xs_ref[...], w)
                if g < 3 or g == G_LORA:
                    store_shifted(p, w, cols)
                elif g == G_Q:
                    o_ref[:, cols] = rotated(p, HEAD ** -0.5)
                elif g == G_KB:
                    o_ref[:, cols] = rotated(p, 1.0)
                else:
                    o_ref[:, cols] = p


def _inproj(x2d, seq, g, w_re, mu_re, cos_t, sa_t, sb_t):
    t_total = x2d.shape[0]
    tm = min(512, seq)
    tiles_per_seq = seq // tm
    n_tiles = t_total // tm
    last8 = t_total // 8 - 1
    kern = functools.partial(_inproj_kernel, tm=tm, tiles_per_seq=tiles_per_seq)
    return pl.pallas_call(
        kern,
        grid=(N_GROUPS // GROUPS_PER_STEP, n_tiles),
        in_specs=[
            pl.BlockSpec((tm, D_MODEL), lambda j, i: (i, 0)),
            pl.BlockSpec((8, D_MODEL), lambda j, i: (jnp.maximum(i * (tm // 8) - 1, 0), 0)),
            pl.BlockSpec((8, D_MODEL), lambda j, i: (jnp.minimum((i + 1) * (tm // 8), last8), 0)),
            pl.BlockSpec((1, D_MODEL), lambda j, i: (0, 0)),
            pl.BlockSpec((D_MODEL, GROUPS_PER_STEP * GROUP), lambda j, i: (0, j)),
            pl.BlockSpec((1, GROUPS_PER_STEP * GROUP), lambda j, i: (0, j)),
            pl.BlockSpec((tm, 128), lambda j, i: (i % tiles_per_seq, 0)),
            pl.BlockSpec((tm, 128), lambda j, i: (i % tiles_per_seq, 0)),
            pl.BlockSpec((tm, 128), lambda j, i: (i % tiles_per_seq, 0)),
        ],
        out_specs=pl.BlockSpec((tm, GROUPS_PER_STEP * GROUP), lambda j, i: (i, j)),
        out_shape=jax.ShapeDtypeStruct((t_total, N_GROUPS * GROUP), F32),
        scratch_shapes=[pltpu.VMEM((tm, D_MODEL), BF16), pltpu.VMEM((16, D_MODEL), F32)],
        compiler_params=pltpu.CompilerParams(
            dimension_semantics=("arbitrary", "arbitrary"), vmem_limit_bytes=VMEM_LIMIT),
        name="inproj",
    )(x2d, x2d, x2d, g, w_re, mu_re, cos_t, sa_t, sb_t)


QUAD = 256
N_QUAD = D_MODEL // QUAD
HEADS_PER_QUAD = QUAD // HEAD
SCAN_SUB = 8


def _scan_kernel(r_ref, k_ref, v_ref, lw_ref, la_ref, wu_ref, au_ref,
                 w0_ref, a0_ref, kk_ref, ka_ref, rk_ref, seg_ref, y_ref, s_ref, h_ref, *, n_sub):
    d = pl.program_id(1)
    c = pl.program_id(2)
    C = CHUNK
    bf = lambda x: x.astype(BF16)

    @pl.when(c == 0)
    def _():
        h_ref[...] = jnp.zeros_like(h_ref)

    sgn = 1 - 2 * d
    rr = lax.broadcasted_iota(jnp.int32, (C, C), 0)
    cc = lax.broadcasted_iota(jnp.int32, (C, C), 1)
    tri = jnp.where((rr - cc) * sgn >= 0, 1.0, 0.0).astype(BF16)

    tr = lax.broadcasted_iota(jnp.int32, (C, QUAD), 0)
    tc = lax.broadcasted_iota(jnp.int32, (C, QUAD), 1) % C
    dd = (tr - tc) * sgn
    strict = dd > 0
    incl = dd >= 0
    eye = jnp.where(dd == 0, 1.0, 0.0)
    levels = []
    m = 1
    while m < C:
        levels.append(jnp.logical_and((tr // (2 * m)) == (tc // (2 * m)), (tr // m) != (tc // m)))
        m *= 2
    lane_head = lax.broadcasted_iota(jnp.int32, (1, QUAD), 1) // HEAD
    head_lanes = [lane_head == h for h in range(HEADS_PER_QUAD)]
    same_head = (lax.broadcasted_iota(jnp.int32, (QUAD, QUAD), 0) // HEAD) == (
        lax.broadcasted_iota(jnp.int32, (QUAD, QUAD), 1) // HEAD)
    bd_ones = jnp.where(same_head, 1.0, 0.0).astype(BF16)

    def blockdiag(x):
        return jnp.concatenate([jnp.where(hl, x, 0.0) for hl in head_lanes], axis=0)

    quads = range(N_QUAD)
    sls = [slice(QUAD * q, QUAD * (q + 1)) for q in quads]

    staged = []
    for i in range(n_sub):
        jj = i + d * (n_sub - 1 - 2 * i)
        rows = pl.ds(pl.multiple_of(jj * C, C), C)
        r = r_ref[rows, :]
        k = k_ref[rows, :]
        v = v_ref[rows, :]

        w_raw = w0_ref[0] + _dot(bf(jnp.tanh(lw_ref[rows, :])), wu_ref[0])
        logw = (-EXP_M05 * LOG2_E) * _sigmoid(w_raw)
        a = _sigmoid(a0_ref[0] + _dot(bf(la_ref[rows, :]), au_ref[0]))

        l1, l2 = _split(logw)
        cum = _dot(tri, l1) + _dot(tri, l2)
        total = jnp.sum(logw, axis=0, keepdims=True)
        g_incl = jnp.exp2(cum)
        g_excl = jnp.exp2(cum - logw)
        g_inv = jnp.exp2(-cum)
        g_tail = jnp.exp2(total - cum)
        g_tot = jnp.exp2(total)

        kkv = k * kk_ref[...]
        kd = k * (1.0 + (a - 1.0) * ka_ref[...])
        sq = kkv * kkv

        s_ref[0, rows, :] = _dot(bf(r * kd * rk_ref[...]), seg_ref[...])

        ss = [_dot(bf(sq[:, sl]), bd_ones) for sl in sls]
        kkn = [kkv[:, sl] * lax.rsqrt(jnp.maximum(s, 1e-24)) for sl, s in zip(sls, ss)]
        bvec = [kn * a[:, sl] for sl, kn in zip(sls, kkn)]
        lhs = [bf(jnp.concatenate([-kn * g_excl[:, sl], r[:, sl] * g_incl[:, sl]], axis=0))
               for sl, kn in zip(sls, kkn)]
        rhs = [jnp.concatenate([blockdiag(bf(bv * g_inv[:, sl])), blockdiag(bf(kd[:, sl] * g_inv[:, sl]))], axis=0)
               for sl, bv in zip(sls, bvec)]
        bkh = [bf(jnp.concatenate([bv * g_tail[:, sl], kd[:, sl] * g_tail[:, sl]], axis=0))
               for sl, bv in zip(sls, bvec)]
        vb = [bf(v[:, sl]) for sl in sls]
        v_bd = [blockdiag(x) for x in vb]

        gm = [_dot(lhs[q], rhs[q], NT) for q in quads]
        gb = [bf(g) for g in gm]
        a_ab = [jnp.where(strict, g[0:C, 0:QUAD], 0.0) for g in gm]
        a_abb = [jnp.where(strict, g[0:C, 0:QUAD], 0.0) for g in gb]
        a_ak = [jnp.where(strict, g[0:C, QUAD:2 * QUAD], 0.0) for g in gb]
        a_r = [jnp.concatenate([jnp.where(incl, g[C:2 * C, 0:QUAD], 0.0),
                                jnp.where(incl, g[C:2 * C, QUAD:2 * QUAD], 0.0)], axis=1) for g in gb]
        akv = [_dot(a_ak[q], v_bd[q]) for q in quads]
        staged.append(dict(rows=rows, lhs=lhs, a_ab=a_ab, a_abb=a_abb, a_r=a_r, akv=akv, vb=vb, v_bd=v_bd,
                           bkh=bkh, g_tot=g_tot))

    combos = [(i, q) for i in range(n_sub) for q in quads]
    tinv = {(i, q): eye + jnp.where(levels[0], staged[i]["a_ab"][q], 0.0) for i, q in combos}
    for lvl in levels[1:]:
        tb = {key: bf(t) for key, t in tinv.items()}
        wm = {(i, q): _dot(jnp.where(lvl, staged[i]["a_abb"][q], 0.0), blockdiag(tb[i, q])) for i, q in combos}
        tinv = {key: tinv[key] + _dot(tb[key], blockdiag(bf(wm[key]))) for key in combos}

    ht = [h_ref[q] for q in quads]
    for i in range(n_sub):
        st = staged[i]
        ar = [_dot(st["lhs"][q], bf(ht[q]), NT) for q in quads]
        rhs_u = [ar[q][0:C] + st["akv"][q] for q in quads]
        u = [bf(_dot(bf(tinv[i, q]), blockdiag(bf(rhs_u[q])))) for q in quads]
        uv_bd = [jnp.concatenate([blockdiag(u[q]), st["v_bd"][q]], axis=0) for q in quads]
        for q in quads:
            y_ref[0, st["rows"], sls[q]] = (ar[q][C:2 * C] + _dot(st["a_r"][q], uv_bd[q])).astype(y_ref.dtype)
        upd = [_dot(jnp.concatenate([u[q], st["vb"][q]], axis=0), st["bkh"][q], TN) for q in quads]
        ht = [ht[q] * st["g_tot"][:, sls[q]] + jnp.where(same_head, upd[q], 0.0) for q in quads]

    for q in quads:
        h_ref[q] = ht[q]


def _wkv_scan(proj, batch, seq, wu, au, w0, a0, k_k, k_a, r_k, seg):
    t_total = batch * seq
    n_sub = math.gcd(seq // CHUNK, SCAN_SUB)
    blk = n_sub * CHUNK
    nc = seq // blk

    def rowblk(b, d, c):
        return b * nc + c + d * (nc - 1 - 2 * c)

    def col(jblk):
        return lambda b, d, c: (rowblk(b, d, c), jblk)

    full = lambda b, d, c: (0, 0)
    perdir = lambda b, d, c: (d, 0, 0)
    return pl.pallas_call(
        functools.partial(_scan_kernel, n_sub=n_sub),
        grid=(batch, 2, nc),
        in_specs=[
            pl.BlockSpec((blk, GROUP), col(0)),
            pl.BlockSpec((blk, GROUP), col(1)),
            pl.BlockSpec((blk, GROUP), col(2)),
            pl.BlockSpec((blk, 128), col(LORA_BLK)),
            pl.BlockSpec((blk, 128), col(LORA_BLK + 1)),
            pl.BlockSpec((1, 128, D_MODEL), perdir),
            pl.BlockSpec((1, 128, D_MODEL), perdir),
            pl.BlockSpec((1, 1, D_MODEL), perdir),
            pl.BlockSpec((1, 1, D_MODEL), perdir),
            pl.BlockSpec((1, D_MODEL), full),
            pl.BlockSpec((1, D_MODEL), full),
            pl.BlockSpec((1, D_MODEL), full),
            pl.BlockSpec((D_MODEL, 128), full),
        ],
        out_specs=[
            pl.BlockSpec((1, blk, D_MODEL), lambda b, d, c: (d, rowblk(b, d, c), 0)),
            pl.BlockSpec((1, blk, 128), lambda b, d, c: (d, rowblk(b, d, c), 0)),
        ],
        out_shape=[
            jax.ShapeDtypeStruct((2, t_total, D_MODEL), BF16),
            jax.ShapeDtypeStruct((2, t_total, 128), F32),
        ],
        scratch_shapes=[pltpu.VMEM((N_QUAD, QUAD, QUAD), F32)],
        compiler_params=pltpu.CompilerParams(
            dimension_semantics=("arbitrary", "arbitrary", "arbitrary"), vmem_limit_bytes=VMEM_LIMIT),
        name="wkv_scan",
    )(proj, proj, proj, proj, proj, wu, au, w0, a0, k_k, k_a, r_k, seg)


V_ROWS = 128 + 16
PV_GROUP = 2


def _attn_kernel(q_ref, k_ref, v_ref, lq1_ref, lk1_ref, lq2_ref, lk2_ref, sw_ref, o_ref,
                 k0_ref, k1_ref, vt_ref, s_ref, m_ref, p_ref, *, seq, tq, tk, per_trip, nq, lam_init):
    i = pl.program_id(2)
    n_kc = seq // tk
    lane = lax.broadcasted_iota(jnp.int32, (1, 128), 1)
    first = lane < HEAD

    def prepare(rows):
        kf = k_ref[rows, :]
        k0_ref[rows, :] = jnp.where(first, kf, 0.0).astype(BF16)
        k1_ref[rows, :] = jnp.where(first, 0.0, kf).astype(BF16)
        vt_ref[0:128, rows] = v_ref[rows, :].T.astype(BF16)
        vt_ref[128:V_ROWS, rows] = jnp.ones((V_ROWS - 128, tk), BF16)

    def rows_of(kc):
        return pl.ds(pl.multiple_of(kc * tk, tk), tk)

    def fold(op, s):
        return op(s.reshape(tk // 8, 8, tq), axis=0)

    def query_t():
        return (q_ref[...] * LOG2_E).T.astype(BF16)

    neg = jnp.full((8, tq), -jnp.inf, F32)
    za = jnp.zeros((V_ROWS, tq), F32)
    n_trips = n_kc // per_trip

    def finish(acc0, acc1):
        l0 = acc0[128:129, :]
        l1 = acc1[128:129, :]
        lam = (jnp.exp(jnp.sum(lq1_ref[...] * lk1_ref[...], axis=-1, keepdims=True))
               - jnp.exp(jnp.sum(lq2_ref[...] * lk2_ref[...], axis=-1, keepdims=True)) + lam_init)
        o_t = acc0[0:128, :] / l0 - lam * (acc1[0:128, :] / l1)
        o = o_t.T
        o = o * lax.rsqrt(jnp.mean(o * o, axis=-1, keepdims=True) + SUBLN_EPS) * sw_ref[...]
        o_ref[...] = (o * (1.0 - lam_init)).astype(o_ref.dtype)

    def run(slot_a):
        slot_b = 1 - slot_a
        mine = (i % 2) == slot_a

        def trip(t, q_t, mx, m_prev, acc, first_tile=False):
            for c in range(per_trip):
                rows = rows_of(t * per_trip + c)
                if first_tile:
                    prepare(rows)
                if q_t is not None:
                    s0 = _dot(k0_ref[rows, :], q_t)
                    s1 = _dot(k1_ref[rows, :], q_t)
                    s_ref[slot_a, 0, rows, :] = s0
                    s_ref[slot_a, 1, rows, :] = s1
                    mx = (jnp.maximum(mx[0], fold(jnp.max, s0)), jnp.maximum(mx[1], fold(jnp.max, s1)))
                if acc is not None:
                    for comp in range(2):
                        x = s_ref[slot_b, comp, rows, :] - m_prev[comp]
                        p_ref[comp, c * tk:(c + 1) * tk, :] = jnp.exp2(x.astype(BF16))
                    if (c + 1) % PV_GROUP == 0 or c + 1 == per_trip:
                        c0 = (c // PV_GROUP) * PV_GROUP
                        span = (c + 1 - c0) * tk
                        start = (t * per_trip + c0) * tk
                        vt = vt_ref[:, pl.ds(pl.multiple_of(start, tk), span)]
                        acc = tuple(acc[comp] + _dot(vt, p_ref[comp, c0 * tk:(c + 1) * tk, :])
                                    for comp in range(2))
            return mx, acc

        def loop(body, init):
            if n_trips == 1:
                return body(0, init)
            return lax.fori_loop(0, n_trips, body, init)

        def prev_max():
            return (jnp.max(m_ref[slot_b, 0], axis=0, keepdims=True),
                    jnp.max(m_ref[slot_b, 1], axis=0, keepdims=True))

        def store_max(mx):
            m_ref[slot_a, 0] = mx[0]
            m_ref[slot_a, 1] = mx[1]

        if slot_a == 0:
            @pl.when(i == 0)
            def _():
                q_t = query_t()
                store_max(loop(lambda t, mx: trip(t, q_t, mx, None, None, first_tile=True)[0], (neg, neg)))

        @pl.when(jnp.logical_and(mine, jnp.logical_and(i > 0, i < nq)))
        def _():
            q_t = query_t()
            m_prev = prev_max()
            mx, acc = loop(lambda t, c: trip(t, q_t, c[0], m_prev, c[1]), ((neg, neg), (za, za)))
            store_max(mx)
            finish(*acc)

        if nq % 2 == slot_a:
            @pl.when(i == nq)
            def _():
                m_prev = prev_max()
                finish(*loop(lambda t, acc: trip(t, None, None, m_prev, acc)[1], (za, za)))

    run(0)
    run(1)


def _diff_attn(proj, batch, seq, lq1, lk1, lq2, lk2, subln_w, lam_init):
    t_total = batch * seq
    tq = min(512, seq)
    tk = min(256, seq)
    per_trip = math.gcd(seq // tk, 16)
    nq = seq // tq
    n_heads = N_PAIR
    qb, kb, vb = (G_Q * GROUP // 128, G_KB * GROUP // 128, G_VB * GROUP // 128)
    kern = functools.partial(_attn_kernel, seq=seq, tq=tq, tk=tk, per_trip=per_trip, nq=nq, lam_init=lam_init)
    small = lambda b, h, i: (0, 0)
    return pl.pallas_call(
        kern,
        grid=(batch, n_heads, nq + 1),
        in_specs=[
            pl.BlockSpec((tq, 128), lambda b, h, i: (b * nq + jnp.minimum(i, nq - 1), qb + h)),
            pl.BlockSpec((seq, 128), lambda b, h, i: (b, kb + h)),
            pl.BlockSpec((seq, 128), lambda b, h, i: (b, vb + h)),
            pl.BlockSpec((1, HEAD), small),
            pl.BlockSpec((1, HEAD), small),
            pl.BlockSpec((1, HEAD), small),
            pl.BlockSpec((1, HEAD), small),
            pl.BlockSpec((1, 128), small),
        ],
        out_specs=pl.BlockSpec((tq, 128), lambda b, h, i: (b * nq + jnp.maximum(i - 1, 0), h)),
        out_shape=jax.ShapeDtypeStruct((t_total, D_MODEL), BF16),
        scratch_shapes=[pltpu.VMEM((seq, 128), BF16), pltpu.VMEM((seq, 128), BF16),
                        pltpu.VMEM((V_ROWS, seq), BF16), pltpu.VMEM((2, 2, seq, tq), F32),
                        pltpu.VMEM((2, 2, 8, tq), F32), pltpu.VMEM((2, per_trip * tk, tq), BF16)],
        compiler_params=pltpu.CompilerParams(
            dimension_semantics=("arbitrary", "arbitrary", "arbitrary"), vmem_limit_bytes=VMEM_LIMIT),
        name="diff_attn",
    )(proj, proj, proj, lq1, lk1, lq2, lk2, subln_w)


def _mix_kernel(x_ref, yf_ref, yb_ref, sf_ref, sb_ref, v_ref, lg_ref, ga_ref, gb_ref, o_ref,
                lnw_ref, lnb_ref, gup_ref, exp_ref, pa_ref, pb_ref, wo_ref, h_ref, ya_ref):
    y = yf_ref[0].astype(F32) + yb_ref[0].astype(F32)
    rh = lax.broadcasted_iota(jnp.int32, (QUAD, QUAD), 0) // HEAD
    ch = lax.broadcasted_iota(jnp.int32, (QUAD, QUAD), 1) // HEAD
    avg = jnp.where(rh == ch, 1.0 / HEAD, 0.0).astype(BF16)
    for q in range(N_QUAD):
        sl = slice(QUAD * q, QUAD * (q + 1))
        yq = y[:, sl]
        cen = yq - _dot(yq.astype(BF16), avg)
        var = _dot((cen * cen).astype(BF16), avg)
        ya_ref[:, sl] = cen * lax.rsqrt(var + GN_EPS)
    coef = _dot((sf_ref[0] + sb_ref[0]).astype(BF16), exp_ref[...])
    gate = _dot(_sigmoid(lg_ref[...]).astype(BF16), gup_ref[...])
    y_a = (ya_ref[...] * lnw_ref[...] + lnb_ref[...] + coef * v_ref[...]) * gate
    pa = _dot(y_a.astype(BF16), pa_ref[...])
    pb = _dot(o_ref[...], pb_ref[...])
    merged = _sigmoid(ga_ref[...]) * pa + _sigmoid(gb_ref[...]) * pb
    h_ref[...] = x_ref[...] + _dot(merged.astype(BF16), wo_ref[...])


def _mix(x2d, proj, y_dir, s_dir, o_attn, ln_w, ln_b, g_up, expand, proj_a, proj_b, w_out):
    t_total = x2d.shape[0]
    tm = 256
    row = lambda i: (i, 0)
    full = lambda i: (0, 0)
    wspec = pl.BlockSpec((D_MODEL, D_MODEL), full)
    return pl.pallas_call(
        _mix_kernel,
        grid=(t_total // tm,),
        in_specs=[
            pl.BlockSpec((tm, D_MODEL), row),
            pl.BlockSpec((1, tm, D_MODEL), lambda i: (0, i, 0)),
            pl.BlockSpec((1, tm, D_MODEL), lambda i: (1, i, 0)),
            pl.BlockSpec((1, tm, 128), lambda i: (0, i, 0)),
            pl.BlockSpec((1, tm, 128), lambda i: (1, i, 0)),
            pl.BlockSpec((tm, GROUP), lambda i: (i, 2)),
            pl.BlockSpec((tm, 128), lambda i: (i, LORA_BLK + 2)),
            pl.BlockSpec((tm, GROUP), lambda i: (i, G_GA)),
            pl.BlockSpec((tm, GROUP), lambda i: (i, G_GB)),
            pl.BlockSpec((tm, D_MODEL), row),
            pl.BlockSpec((1, D_MODEL), full),
            pl.BlockSpec((1, D_MODEL), full),
            pl.BlockSpec((GATE_RANK, D_MODEL), full),
            pl.BlockSpec((128, D_MODEL), full),
            wspec, wspec, wspec,
        ],
        out_specs=pl.BlockSpec((tm, D_MODEL), row),
        out_shape=jax.ShapeDtypeStruct((t_total, D_MODEL), F32),
        scratch_shapes=[pltpu.VMEM((tm, D_MODEL), F32)],
        compiler_params=pltpu.CompilerParams(
            dimension_semantics=("arbitrary",), vmem_limit_bytes=VMEM_LIMIT),
        name="mix",
    )(x2d, y_dir, y_dir, s_dir, s_dir, proj, proj, proj, proj, o_attn,
      ln_w, ln_b, g_up, expand, proj_a, proj_b, w_out)


def _mlp_kernel(h_ref, gm_ref, gf_ref, w1_ref, w2_ref, o_ref, *, ff_chunk):
    h = h_ref[...]
    xn = _rms(h, gm_ref[...]).astype(BF16)
    acc = h
    for c in range(D_FF // ff_chunk):
        sl = slice(c * ff_chunk, (c + 1) * ff_chunk)
        hid = jnp.maximum(_dot(xn, w1_ref[:, sl]), 0.0)
        acc = acc + _dot((hid * hid).astype(BF16), w2_ref[sl, :])
    o_ref[...] = _rms(acc, gf_ref[...])


def _mlp(h2d, norm_mlp, norm_final, w1, w2):
    t_total = h2d.shape[0]
    tm = 256
    row = lambda i: (i, 0)
    full = lambda i: (0, 0)
    kern = functools.partial(_mlp_kernel, ff_chunk=1024)
    return pl.pallas_call(
        kern,
        grid=(t_total // tm,),
        in_specs=[
            pl.BlockSpec((tm, D_MODEL), row),
            pl.BlockSpec((1, D_MODEL), full),
            pl.BlockSpec((1, D_MODEL), full),
            pl.BlockSpec((D_MODEL, D_FF), full),
            pl.BlockSpec((D_FF, D_MODEL), full),
        ],
        out_specs=pl.BlockSpec((tm, D_MODEL), row),
        out_shape=jax.ShapeDtypeStruct((t_total, D_MODEL), F32),
        compiler_params=pltpu.CompilerParams(
            dimension_semantics=("arbitrary",), vmem_limit_bytes=VMEM_LIMIT),
        name="mlp",
    )(h2d, norm_mlp, norm_final, w1, w2)


def _rope_tables(seq):
    half = ROPE_DIM // 2
    pos = jnp.arange(seq, dtype=F32)
    inv_freq = ROPE_THETA ** (-jnp.arange(0, ROPE_DIM, 2, dtype=F32) / ROPE_DIM)
    ang = pos[:, None] * inv_freq[None, :]
    cos, sin = jnp.cos(ang), jnp.sin(ang)
    pad = jnp.zeros((seq, HEAD - ROPE_DIM), F32)
    z8 = jnp.zeros((seq, half), F32)
    c64 = jnp.concatenate([cos, cos, pad + 1.0], axis=1)
    sa64 = jnp.concatenate([z8, sin, pad], axis=1)
    sb64 = jnp.concatenate([-sin, z8, pad], axis=1)
    two = lambda t: jnp.concatenate([t, t], axis=1)
    return two(c64), two(sa64), two(sb64)


def _layer(x, l, prm, norm_final):
    batch, seq, _ = x.shape
    x2d = x.reshape(batch * seq, D_MODEL)
    cos_t, sa_t, sb_t = _rope_tables(seq)
    proj = _inproj(x2d, seq, prm["norm_mix"], prm["w_re"], prm["mu_re"], cos_t, sa_t, sb_t)
    y_dir, s_dir = _wkv_scan(proj, batch, seq, prm["wu"], prm["au"],
                             prm["w0"], prm["a0"], prm["k_k"], prm["k_a"], prm["r_k"], prm["seg"])
    lam_init = 0.8 - 0.6 * math.exp(-0.3 * l)
    o_attn = _diff_attn(proj, batch, seq, prm["lq1"], prm["lk1"], prm["lq2"], prm["lk2"],
                        prm["subln_w"], lam_init)
    h = _mix(x2d, proj, y_dir, s_dir, o_attn, prm["ln_w"], prm["ln_b"], prm["g_up"], prm["expand"],
             prm["proj_a"], prm["proj_b"], prm["w_out"])
    return h


def _prep_layer(l, w_in, mu_shift, w0, w_lora_up, a0, a_lora_up, g_lora_up, k_k, k_a, r_k, ln_x_w,
                ln_x_b, lambda_q1, lambda_k1, lambda_q2, lambda_k2, subln_w, proj_a, proj_b, w_out,
                norm_mix, norm_mlp, w_mlp_in, w_mlp_out):
    w = w_in[l]
    pad_cols = N_GROUPS * GROUP - w.shape[1]
    w_re = jnp.concatenate([w[:, :RWKV_MAIN], w[:, RWKV_COLS:], w[:, RWKV_MAIN:RWKV_COLS],
                            jnp.zeros((D_MODEL, pad_cols), F32)], axis=1).astype(BF16)
    mu = mu_shift[l]
    mu_re = jnp.concatenate([mu[:RWKV_MAIN], jnp.zeros((5 * GROUP,), F32), mu[RWKV_MAIN:],
                             jnp.zeros((pad_cols,), F32)])[None, :]

    def lora_pad(up):
        z = jnp.zeros_like(up[0])
        return jnp.stack([jnp.concatenate([up[0], z], axis=0), jnp.concatenate([z, up[1]], axis=0)])

    wu = lora_pad(w_lora_up[l])
    au = lora_pad(a_lora_up[l])
    head_of_lane = jnp.arange(D_MODEL) // HEAD
    seg = (head_of_lane[:, None] == jnp.arange(128)[None, :]).astype(BF16)
    return dict(
        w_re=w_re, mu_re=mu_re, norm_mix=norm_mix[l][None, :],
        wu=wu.astype(BF16), au=au.astype(BF16),
        w0=w0[l][:, None, :], a0=a0[l][:, None, :],
        k_k=k_k[l][None, :], k_a=k_a[l][None, :], r_k=r_k[l].reshape(1, D_MODEL),
        seg=seg, expand=seg.T,
        lq1=lambda_q1[l][None, :], lk1=lambda_k1[l][None, :],
        lq2=lambda_q2[l][None, :], lk2=lambda_k2[l][None, :],
        subln_w=subln_w[l][None, :],
        ln_w=ln_x_w[l][None, :], ln_b=ln_x_b[l][None, :],
        g_up=g_lora_up[l].astype(BF16),
        proj_a=proj_a[l].astype(BF16), proj_b=proj_b[l].astype(BF16), w_out=w_out[l].astype(BF16),
        norm_mlp=norm_mlp[l][None, :],
        w1=w_mlp_in[l].astype(BF16), w2=w_mlp_out[l].astype(BF16),
    )


def kernel(x_prompt, x_sample, w_in, mu_shift, w0, w_lora_up, a0, a_lora_up, g_lora_up, k_k, k_a, r_k, ln_x_w, ln_x_b, lambda_q1, lambda_k1, lambda_q2, lambda_k2, subln_w, proj_a, proj_b, w_out, norm_mix, norm_mlp, w_mlp_in, w_mlp_out, norm_final):
    depth = w_in.shape[0]
    assert depth == 1, "the final norm is fused into the (single) layer's MLP kernel"
    prm = _prep_layer(0, w_in, mu_shift, w0, w_lora_up, a0, a_lora_up, g_lora_up, k_k, k_a, r_k, ln_x_w,
                      ln_x_b, lambda_q1, lambda_k1, lambda_q2, lambda_k2, subln_w, proj_a, proj_b,
                      w_out, norm_mix, norm_mlp, w_mlp_in, w_mlp_out)
    outs = []
    for x in (x_prompt, x_sample):
        h = _layer(x, 0, prm, norm_final)
        y = _mlp(h, prm["norm_mlp"], norm_final[None, :], prm["w1"], prm["w2"])
        outs.append(y.reshape(x.shape))
    return tuple(outs)
```

```python
import functools
import math

import jax
import jax.numpy as jnp
from jax import lax
from jax.experimental import pallas as pl
from jax.experimental.pallas import tpu as pltpu

F32 = jnp.float32
BF16 = jnp.bfloat16

D_MODEL = 1024
HEAD = 64
N_PAIR = D_MODEL // 128
DECAY_RANK = 64
ICLR_RANK = 64
GATE_RANK = 128
GN_EPS = 64e-5
ROPE_THETA = 500000.0
ROPE_DIM = 16
SUBLN_EPS = 1e-5
NORM_EPS = 1e-6
D_FF = 4 * D_MODEL
RWKV_MAIN = 3 * D_MODEL
RWKV_COLS = RWKV_MAIN + 2 * DECAY_RANK + 2 * ICLR_RANK + GATE_RANK
GROUP = 1024
N_GROUPS = 9
G_Q, G_KB, G_VB, G_GA, G_GB, G_LORA = 3, 4, 5, 6, 7, 8
GROUPS_PER_STEP = 3
LORA_BLK = G_LORA * GROUP // 128
CHUNK = 64
EXP_M05 = math.exp(-0.5)
LOG2_E = math.log2(math.e)
VMEM_LIMIT = 56 * 1024 * 1024
TM_INPROJ = 512
TM_MIX = 512
TM_MLP = 512
FF_CHUNK = 1024
TQ_ATTN = 512
TK_ATTN = 256

NN = (((1,), (0,)), ((), ()))
NT = (((1,), (1,)), ((), ()))
TN = (((0,), (0,)), ((), ()))


def _dot(a, b, dims=NN):
    return lax.dot_general(a, b, dims, preferred_element_type=F32)


def _split(x):
    hi = x.astype(BF16)
    lo = (x - hi.astype(F32)).astype(BF16)
    return hi, lo


def _sigmoid(x):
    return 1.0 / (1.0 + jnp.exp(-x))


def _rms(x, g):
    return x * lax.rsqrt(jnp.mean(x * x, axis=-1, keepdims=True) + NORM_EPS) * g


def _inproj_kernel(x_ref, xp_ref, xn_ref, g_ref, w_ref, mu_ref, cos_ref, sa_ref, sb_ref,
                   o_ref, xs_ref, hs_ref, *, tm, tiles_per_seq):
    j = pl.program_id(0)
    i = pl.program_id(1)
    g = g_ref[...]
    xs_ref[...] = _rms(x_ref[...], g).astype(BF16)
    hs_ref[0:8, :] = _rms(xp_ref[...], g)
    hs_ref[8:16, :] = _rms(xn_ref[...], g)

    def store_shifted(p, w, cols):
        ph = _dot(hs_ref[...].astype(BF16), w)
        t_in_seq = i % tiles_per_seq
        prev = jnp.where(t_in_seq == 0, 0.0, ph[7:8, :])
        nxt = jnp.where(t_in_seq == tiles_per_seq - 1, 0.0, ph[8:9, :])
        c2 = 0.5 * mu_ref[:, cols]
        c1 = 1.0 - mu_ref[:, cols]
        core = p * c1 + (pltpu.roll(p, 1, 0) + pltpu.roll(p, tm - 1, 0)) * c2
        o_ref[:, cols] = core
        o_ref[0:1, cols] = core[0:1, :] + (prev - p[tm - 1:tm, :]) * c2
        o_ref[tm - 1:tm, cols] = core[tm - 1:tm, :] + (nxt - p[0:1, :]) * c2

    def rotated(p, scale):
        c = jnp.tile(cos_ref[...], (1, N_PAIR))
        sa = jnp.tile(sa_ref[...], (1, N_PAIR))
        sb = jnp.tile(sb_ref[...], (1, N_PAIR))
        half = ROPE_DIM // 2
        out = p * c + pltpu.roll(p, half, 1) * sa + pltpu.roll(p, GROUP - half, 1) * sb
        return out if scale == 1.0 else out * scale

    for step in range(N_GROUPS // GROUPS_PER_STEP):
        @pl.when(j == step)
        def _(step=step):
            for gi in range(GROUPS_PER_STEP):
                g = step * GROUPS_PER_STEP + gi
                cols = slice(gi * GROUP, (gi + 1) * GROUP)
                w = w_ref[:, cols]
                p = _dot(xs_ref[...], w)
                if g < 3 or g == G_LORA:
                    store_shifted(p, w, cols)
                elif g == G_Q:
                    o_ref[:, cols] = rotated(p, HEAD ** -0.5 * LOG2_E)
                elif g == G_KB:
                    o_ref[:, cols] = rotated(p, 1.0)
                else:
                    o_ref[:, cols] = p


def _inproj(x2d, seq, g, w_re, mu_re, cos_t, sa_t, sb_t):
    t_total = x2d.shape[0]
    tm = min(TM_INPROJ, seq)
    tiles_per_seq = seq // tm
    n_tiles = t_total // tm
    last8 = t_total // 8 - 1
    kern = functools.partial(_inproj_kernel, tm=tm, tiles_per_seq=tiles_per_seq)
    return pl.pallas_call(
        kern,
        grid=(N_GROUPS // GROUPS_PER_STEP, n_tiles),
        in_specs=[
            pl.BlockSpec((tm, D_MODEL), lambda j, i: (i, 0)),
            pl.BlockSpec((8, D_MODEL), lambda j, i: (jnp.maximum(i * (tm // 8) - 1, 0), 0)),
            pl.BlockSpec((8, D_MODEL), lambda j, i: (jnp.minimum((i + 1) * (tm // 8), last8), 0)),
            pl.BlockSpec((1, D_MODEL), lambda j, i: (0, 0)),
            pl.BlockSpec((D_MODEL, GROUPS_PER_STEP * GROUP), lambda j, i: (0, j)),
            pl.BlockSpec((1, GROUPS_PER_STEP * GROUP), lambda j, i: (0, j)),
            pl.BlockSpec((tm, 128), lambda j, i: (i % tiles_per_seq, 0)),
            pl.BlockSpec((tm, 128), lambda j, i: (i % tiles_per_seq, 0)),
            pl.BlockSpec((tm, 128), lambda j, i: (i % tiles_per_seq, 0)),
        ],
        out_specs=pl.BlockSpec((tm, GROUPS_PER_STEP * GROUP), lambda j, i: (i, j)),
        out_shape=jax.ShapeDtypeStruct((t_total, N_GROUPS * GROUP), F32),
        scratch_shapes=[pltpu.VMEM((tm, D_MODEL), BF16), pltpu.VMEM((16, D_MODEL), F32)],
        compiler_params=pltpu.CompilerParams(
            dimension_semantics=("arbitrary", "arbitrary"), vmem_limit_bytes=VMEM_LIMIT),
        name="inproj",
    )(x2d, x2d, x2d, g, w_re, mu_re, cos_t, sa_t, sb_t)


QUAD = 256
N_QUAD = D_MODEL // QUAD
HEADS_PER_QUAD = QUAD // HEAD
SCAN_SUB = 8


def _scan_kernel(r_ref, k_ref, v_ref, lw_ref, la_ref, wu_ref, au_ref,
                 w0_ref, a0_ref, kk_ref, ka_ref, rk_ref, seg_ref, y_ref, s_ref, h_ref, *, n_sub):
    d = pl.program_id(1)
    c = pl.program_id(2)
    C = CHUNK
    bf = lambda x: x.astype(BF16)

    @pl.when(c == 0)
    def _():
        h_ref[...] = jnp.zeros_like(h_ref)

    sgn = 1 - 2 * d
    rr = lax.broadcasted_iota(jnp.int32, (C, C), 0)
    cc = lax.broadcasted_iota(jnp.int32, (C, C), 1)
    tri = jnp.where((rr - cc) * sgn >= 0, 1.0, 0.0).astype(BF16)

    tr = lax.broadcasted_iota(jnp.int32, (C, QUAD), 0)
    tc = lax.broadcasted_iota(jnp.int32, (C, QUAD), 1) % C
    dd = (tr - tc) * sgn
    strict = dd > 0
    incl = dd >= 0
    eye = jnp.where(dd == 0, 1.0, 0.0)
    levels = []
    m = 1
    while m < C:
        levels.append(jnp.logical_and((tr // (2 * m)) == (tc // (2 * m)), (tr // m) != (tc // m)))
        m *= 2
    lane_head = lax.broadcasted_iota(jnp.int32, (1, QUAD), 1) // HEAD
    head_lanes = [lane_head == h for h in range(HEADS_PER_QUAD)]
    same_head = (lax.broadcasted_iota(jnp.int32, (QUAD, QUAD), 0) // HEAD) == (
        lax.broadcasted_iota(jnp.int32, (QUAD, QUAD), 1) // HEAD)
    bd_ones = jnp.where(same_head, 1.0, 0.0).astype(BF16)

    def blockdiag(x):
        return jnp.concatenate([jnp.where(hl, x, 0.0) for hl in head_lanes], axis=0)

    quads = range(N_QUAD)
    sls = [slice(QUAD * q, QUAD * (q + 1)) for q in quads]

    staged = []
    for i in range(n_sub):
        jj = i + d * (n_sub - 1 - 2 * i)
        rows = pl.ds(pl.multiple_of(jj * C, C), C)
        r = r_ref[rows, :]
        k = k_ref[rows, :]
        v = v_ref[rows, :]

        w_raw = w0_ref[0] + _dot(bf(jnp.tanh(lw_ref[rows, :])), wu_ref[0])
        logw = (-EXP_M05 * LOG2_E) * _sigmoid(w_raw)
        a = _sigmoid(a0_ref[0] + _dot(bf(la_ref[rows, :]), au_ref[0]))

        l1, l2 = _split(logw)
        cum = _dot(tri, l1) + _dot(tri, l2)
        total = jnp.sum(logw, axis=0, keepdims=True)
        g_incl = jnp.exp2(cum)
        g_excl = jnp.exp2(cum - logw)
        g_inv = jnp.exp2(-cum)
        g_tail = jnp.exp2(total - cum)
        g_tot = jnp.exp2(total)

        kkv = k * kk_ref[...]
        kd = k * (1.0 + (a - 1.0) * ka_ref[...])
        sq = kkv * kkv

        s_ref[0, rows, :] = _dot(bf(r * kd * rk_ref[...]), seg_ref[...])

        ss = [_dot(bf(sq[:, sl]), bd_ones) for sl in sls]
        kkn = [kkv[:, sl] * lax.rsqrt(jnp.maximum(s, 1e-24)) for sl, s in zip(sls, ss)]
        bvec = [kn * a[:, sl] for sl, kn in zip(sls, kkn)]
        lhs = [bf(jnp.concatenate([-kn * g_excl[:, sl], r[:, sl] * g_incl[:, sl]], axis=0))
               for sl, kn in zip(sls, kkn)]
        rhs = [jnp.concatenate([blockdiag(bf(bv * g_inv[:, sl])), blockdiag(bf(kd[:, sl] * g_inv[:, sl]))], axis=0)
               for sl, bv in zip(sls, bvec)]
        bkh = [bf(jnp.concatenate([bv * g_tail[:, sl], kd[:, sl] * g_tail[:, sl]], axis=0))
               for sl, bv in zip(sls, bvec)]
        vb = [bf(v[:, sl]) for sl in sls]
        v_bd = [blockdiag(x) for x in vb]

        gm = [_dot(lhs[q], rhs[q], NT) for q in quads]
        gb = [bf(g) for g in gm]
        a_ab = [jnp.where(strict, g[0:C, 0:QUAD], 0.0) for g in gm]
        a_abb = [jnp.where(strict, g[0:C, 0:QUAD], 0.0) for g in gb]
        a_ak = [jnp.where(strict, g[0:C, QUAD:2 * QUAD], 0.0) for g in gb]
        a_r = [jnp.concatenate([jnp.where(incl, g[C:2 * C, 0:QUAD], 0.0),
                                jnp.where(incl, g[C:2 * C, QUAD:2 * QUAD], 0.0)], axis=1) for g in gb]
        akv = [_dot(a_ak[q], v_bd[q]) for q in quads]
        staged.append(dict(rows=rows, lhs=lhs, a_ab=a_ab, a_abb=a_abb, a_r=a_r, akv=akv, vb=vb, v_bd=v_bd,
                           bkh=bkh, g_tot=g_tot))

    combos = [(i, q) for i in range(n_sub) for q in quads]
    tinv = {(i, q): eye + jnp.where(levels[0], staged[i]["a_ab"][q], 0.0) for i, q in combos}
    for lvl in levels[1:]:
        tb = {key: bf(t) for key, t in tinv.items()}
        wm = {(i, q): _dot(jnp.where(lvl, staged[i]["a_abb"][q], 0.0), blockdiag(tb[i, q])) for i, q in combos}
        tinv = {key: tinv[key] + _dot(tb[key], blockdiag(bf(wm[key]))) for key in combos}

    ht = [h_ref[q] for q in quads]
    for i in range(n_sub):
        st = staged[i]
        ar = [_dot(st["lhs"][q], bf(ht[q]), NT) for q in quads]
        rhs_u = [ar[q][0:C] + st["akv"][q] for q in quads]
        u = [bf(_dot(bf(tinv[i, q]), blockdiag(bf(rhs_u[q])))) for q in quads]
        uv_bd = [jnp.concatenate([blockdiag(u[q]), st["v_bd"][q]], axis=0) for q in quads]
        for q in quads:
            y_ref[0, st["rows"], sls[q]] = (ar[q][C:2 * C] + _dot(st["a_r"][q], uv_bd[q])).astype(y_ref.dtype)
        upd = [_dot(jnp.concatenate([u[q], st["vb"][q]], axis=0), st["bkh"][q], TN) for q in quads]
        ht = [ht[q] * st["g_tot"][:, sls[q]] + jnp.where(same_head, upd[q], 0.0) for q in quads]

    for q in quads:
        h_ref[q] = ht[q]


def _wkv_scan(proj, batch, seq, wu, au, w0, a0, k_k, k_a, r_k, seg):
    t_total = batch * seq
    n_sub = math.gcd(seq // CHUNK, SCAN_SUB)
    blk = n_sub * CHUNK
    nc = seq // blk

    def rowblk(b, d, c):
        return b * nc + c + d * (nc - 1 - 2 * c)

    def col(jblk):
        return lambda b, d, c: (rowblk(b, d, c), jblk)

    full = lambda b, d, c: (0, 0)
    perdir = lambda b, d, c: (d, 0, 0)
    return pl.pallas_call(
        functools.partial(_scan_kernel, n_sub=n_sub),
        grid=(batch, 2, nc),
        in_specs=[
            pl.BlockSpec((blk, GROUP), col(0)),
            pl.BlockSpec((blk, GROUP), col(1)),
            pl.BlockSpec((blk, GROUP), col(2)),
            pl.BlockSpec((blk, 128), col(LORA_BLK)),
            pl.BlockSpec((blk, 128), col(LORA_BLK + 1)),
            pl.BlockSpec((1, 128, D_MODEL), perdir),
            pl.BlockSpec((1, 128, D_MODEL), perdir),
            pl.BlockSpec((1, 1, D_MODEL), perdir),
            pl.BlockSpec((1, 1, D_MODEL), perdir),
            pl.BlockSpec((1, D_MODEL), full),
            pl.BlockSpec((1, D_MODEL), full),
            pl.BlockSpec((1, D_MODEL), full),
            pl.BlockSpec((D_MODEL, 128), full),
        ],
        out_specs=[
            pl.BlockSpec((1, blk, D_MODEL), lambda b, d, c: (d, rowblk(b, d, c), 0)),
            pl.BlockSpec((1, blk, 128), lambda b, d, c: (d, rowblk(b, d, c), 0)),
        ],
        out_shape=[
            jax.ShapeDtypeStruct((2, t_total, D_MODEL), BF16),
            jax.ShapeDtypeStruct((2, t_total, 128), F32),
        ],
        scratch_shapes=[pltpu.VMEM((N_QUAD, QUAD, QUAD), F32)],
        compiler_params=pltpu.CompilerParams(
            dimension_semantics=("arbitrary", "arbitrary", "arbitrary"), vmem_limit_bytes=VMEM_LIMIT),
        name="wkv_scan",
    )(proj, proj, proj, proj, proj, wu, au, w0, a0, k_k, k_a, r_k, seg)


V_ROWS = 128 + 16
PV_GROUP = 2


def _attn_kernel(q_ref, k_ref, v_ref, lq1_ref, lk1_ref, lq2_ref, lk2_ref, sw_ref, o_ref,
                 k0_ref, k1_ref, vt_ref, s_ref, m_ref, p_ref, *, seq, tq, tk, per_trip, nq, lam_init):
    i = pl.program_id(2)
    n_kc = seq // tk
    lane = lax.broadcasted_iota(jnp.int32, (1, 128), 1)
    first = lane < HEAD

    def prepare(rows):
        kf = k_ref[rows, :]
        k0_ref[rows, :] = jnp.where(first, kf, 0.0).astype(BF16)
        k1_ref[rows, :] = jnp.where(first, 0.0, kf).astype(BF16)
        vt_ref[0:128, rows] = v_ref[rows, :].T.astype(BF16)
        vt_ref[128:V_ROWS, rows] = jnp.ones((V_ROWS - 128, tk), BF16)

    def rows_of(kc):
        return pl.ds(pl.multiple_of(kc * tk, tk), tk)

    def fold(op, s):
        return op(s.reshape(tk // 8, 8, tq), axis=0)

    def query_t():
        return q_ref[...].T.astype(BF16)

    neg = jnp.full((8, tq), -jnp.inf, F32)
    za = jnp.zeros((V_ROWS, tq), F32)
    n_trips = n_kc // per_trip

    def finish(acc0, acc1):
        l0 = acc0[128:129, :]
        l1 = acc1[128:129, :]
        lam = (jnp.exp(jnp.sum(lq1_ref[...] * lk1_ref[...], axis=-1, keepdims=True))
               - jnp.exp(jnp.sum(lq2_ref[...] * lk2_ref[...], axis=-1, keepdims=True)) + lam_init)
        o_t = acc0[0:128, :] * (1.0 / l0) - acc1[0:128, :] * (lam / l1)
        o = o_t.T
        o = o * lax.rsqrt(jnp.mean(o * o, axis=-1, keepdims=True) + SUBLN_EPS) * sw_ref[...]
        o_ref[...] = (o * (1.0 - lam_init)).astype(o_ref.dtype)

    def run(slot_a):
        slot_b = 1 - slot_a
        mine = (i % 2) == slot_a

        def trip(t, q_t, mx, m_prev, acc, first_tile=False):
            for c in range(per_trip):
                rows = rows_of(t * per_trip + c)
                if first_tile:
                    prepare(rows)
                if q_t is not None:
                    s0 = _dot(k0_ref[rows, :], q_t)
                    s1 = _dot(k1_ref[rows, :], q_t)
                    s_ref[slot_a, 0, rows, :] = s0
                    s_ref[slot_a, 1, rows, :] = s1
                    mx = (jnp.maximum(mx[0], fold(jnp.max, s0)), jnp.maximum(mx[1], fold(jnp.max, s1)))
                if acc is not None:
                    for comp in range(2):
                        x = s_ref[slot_b, comp, rows, :] - m_prev[comp]
                        p_ref[comp, c * tk:(c + 1) * tk, :] = jnp.exp2(x.astype(BF16))
                    if (c + 1) % PV_GROUP == 0 or c + 1 == per_trip:
                        c0 = (c // PV_GROUP) * PV_GROUP
                        span = (c + 1 - c0) * tk
                        start = (t * per_trip + c0) * tk
                        vt = vt_ref[:, pl.ds(pl.multiple_of(start, tk), span)]
                        acc = tuple(acc[comp] + _dot(vt, p_ref[comp, c0 * tk:(c + 1) * tk, :])
                                    for comp in range(2))
            return mx, acc

        def loop(body, init):
            if n_trips == 1:
                return body(0, init)
            return lax.fori_loop(0, n_trips, body, init)

        def prev_max():
            return (jnp.max(m_ref[slot_b, 0], axis=0, keepdims=True),
                    jnp.max(m_ref[slot_b, 1], axis=0, keepdims=True))

        def store_max(mx):
            m_ref[slot_a, 0] = mx[0]
            m_ref[slot_a, 1] = mx[1]

        if slot_a == 0:
            @pl.when(i == 0)
            def _():
                q_t = query_t()
                store_max(loop(lambda t, mx: trip(t, q_t, mx, None, None, first_tile=True)[0], (neg, neg)))

        @pl.when(jnp.logical_and(mine, jnp.logical_and(i > 0, i < nq)))
        def _():
            q_t = query_t()
            m_prev = prev_max()
            mx, acc = loop(lambda t, c: trip(t, q_t, c[0], m_prev, c[1]), ((neg, neg), (za, za)))
            store_max(mx)
            finish(*acc)

        if nq % 2 == slot_a:
            @pl.when(i == nq)
            def _():
                m_prev = prev_max()
                finish(*loop(lambda t, acc: trip(t, None, None, m_prev, acc)[1], (za, za)))

    run(0)
    run(1)


def _diff_attn(proj, batch, seq, lq1, lk1, lq2, lk2, subln_w, lam_init):
    t_total = batch * seq
    tq = min(TQ_ATTN, seq)
    tk = min(TK_ATTN, seq)
    per_trip = math.gcd(seq // tk, 16)
    nq = seq // tq
    n_heads = N_PAIR
    qb, kb, vb = (G_Q * GROUP // 128, G_KB * GROUP // 128, G_VB * GROUP // 128)
    kern = functools.partial(_attn_kernel, seq=seq, tq=tq, tk=tk, per_trip=per_trip, nq=nq, lam_init=lam_init)
    small = lambda b, h, i: (0, 0)
    return pl.pallas_call(
        kern,
        grid=(batch, n_heads, nq + 1),
        in_specs=[
            pl.BlockSpec((tq, 128), lambda b, h, i: (b * nq + jnp.minimum(i, nq - 1), qb + h)),
            pl.BlockSpec((seq, 128), lambda b, h, i: (b, kb + h)),
            pl.BlockSpec((seq, 128), lambda b, h, i: (b, vb + h)),
            pl.BlockSpec((1, HEAD), small),
            pl.BlockSpec((1, HEAD), small),
            pl.BlockSpec((1, HEAD), small),
            pl.BlockSpec((1, HEAD), small),
            pl.BlockSpec((1, 128), small),
        ],
        out_specs=pl.BlockSpec((tq, 128), lambda b, h, i: (b * nq + jnp.maximum(i - 1, 0), h)),
        out_shape=jax.ShapeDtypeStruct((t_total, D_MODEL), BF16),
        scratch_shapes=[pltpu.VMEM((seq, 128), BF16), pltpu.VMEM((seq, 128), BF16),
                        pltpu.VMEM((V_ROWS, seq), BF16), pltpu.VMEM((2, 2, seq, tq), F32),
                        pltpu.VMEM((2, 2, 8, tq), F32), pltpu.VMEM((2, per_trip * tk, tq), BF16)],
        compiler_params=pltpu.CompilerParams(
            dimension_semantics=("arbitrary", "arbitrary", "arbitrary"), vmem_limit_bytes=VMEM_LIMIT),
        name="diff_attn",
    )(proj, proj, proj, lq1, lk1, lq2, lk2, subln_w)


def _mix_kernel(x_ref, yf_ref, yb_ref, sf_ref, sb_ref, v_ref, lg_ref, ga_ref, gb_ref, o_ref,
                lnw_ref, lnb_ref, gup_ref, exp_ref, pa_ref, pb_ref, wo_ref, h_ref, ya_ref):
    y = yf_ref[0].astype(F32) + yb_ref[0].astype(F32)
    rh = lax.broadcasted_iota(jnp.int32, (QUAD, QUAD), 0) // HEAD
    ch = lax.broadcasted_iota(jnp.int32, (QUAD, QUAD), 1) // HEAD
    avg = jnp.where(rh == ch, 1.0 / HEAD, 0.0).astype(BF16)
    for q in range(N_QUAD):
        sl = slice(QUAD * q, QUAD * (q + 1))
        yq = y[:, sl]
        cen = yq - _dot(yq.astype(BF16), avg)
        var = _dot((cen * cen).astype(BF16), avg)
        ya_ref[:, sl] = cen * lax.rsqrt(var + GN_EPS)
    coef = _dot((sf_ref[0] + sb_ref[0]).astype(BF16), exp_ref[...])
    gate = _dot(_sigmoid(lg_ref[...]).astype(BF16), gup_ref[...])
    y_a = (ya_ref[...] * lnw_ref[...] + lnb_ref[...] + coef * v_ref[...]) * gate
    pa = _dot(y_a.astype(BF16), pa_ref[...])
    pb = _dot(o_ref[...], pb_ref[...])
    merged = _sigmoid(ga_ref[...]) * pa + _sigmoid(gb_ref[...]) * pb
    h_ref[...] = x_ref[...] + _dot(merged.astype(BF16), wo_ref[...])


def _mix(x2d, proj, y_dir, s_dir, o_attn, ln_w, ln_b, g_up, expand, proj_a, proj_b, w_out):
    t_total = x2d.shape[0]
    tm = TM_MIX
    row = lambda i: (i, 0)
    full = lambda i: (0, 0)
    wspec = pl.BlockSpec((D_MODEL, D_MODEL), full)
    return pl.pallas_call(
        _mix_kernel,
        grid=(t_total // tm,),
        in_specs=[
            pl.BlockSpec((tm, D_MODEL), row),
            pl.BlockSpec((1, tm, D_MODEL), lambda i: (0, i, 0)),
            pl.BlockSpec((1, tm, D_MODEL), lambda i: (1, i, 0)),
            pl.BlockSpec((1, tm, 128), lambda i: (0, i, 0)),
            pl.BlockSpec((1, tm, 128), lambda i: (1, i, 0)),
            pl.BlockSpec((tm, GROUP), lambda i: (i, 2)),
            pl.BlockSpec((tm, 128), lambda i: (i, LORA_BLK + 2)),
            pl.BlockSpec((tm, GROUP), lambda i: (i, G_GA)),
            pl.BlockSpec((tm, GROUP), lambda i: (i, G_GB)),
            pl.BlockSpec((tm, D_MODEL), row),
            pl.BlockSpec((1, D_MODEL), full),
            pl.BlockSpec((1, D_MODEL), full),
            pl.BlockSpec((GATE_RANK, D_MODEL), full),
            pl.BlockSpec((128, D_MODEL), full),
            wspec, wspec, wspec,
        ],
        out_specs=pl.BlockSpec((tm, D_MODEL), row),
        out_shape=jax.ShapeDtypeStruct((t_total, D_MODEL), F32),
        scratch_shapes=[pltpu.VMEM((tm, D_MODEL), F32)],
        compiler_params=pltpu.CompilerParams(
            dimension_semantics=("arbitrary",), vmem_limit_bytes=VMEM_LIMIT),
        name="mix",
    )(x2d, y_dir, y_dir, s_dir, s_dir, proj, proj, proj, proj, o_attn,
      ln_w, ln_b, g_up, expand, proj_a, proj_b, w_out)


def _mlp_kernel(h_ref, gm_ref, gf_ref, w1_ref, w2_ref, o_ref, *, ff_chunk):
    h = h_ref[...]
    xn = _rms(h, gm_ref[...]).astype(BF16)
    acc = h
    for c in range(D_FF // ff_chunk):
        sl = slice(c * ff_chunk, (c + 1) * ff_chunk)
        hid = jnp.maximum(_dot(xn, w1_ref[:, sl]), 0.0)
        acc = acc + _dot((hid * hid).astype(BF16), w2_ref[sl, :])
    o_ref[...] = _rms(acc, gf_ref[...])


def _mlp(h2d, norm_mlp, norm_final, w1, w2):
    t_total = h2d.shape[0]
    tm = TM_MLP
    row = lambda i: (i, 0)
    full = lambda i: (0, 0)
    kern = functools.partial(_mlp_kernel, ff_chunk=FF_CHUNK)
    return pl.pallas_call(
        kern,
        grid=(t_total // tm,),
        in_specs=[
            pl.BlockSpec((tm, D_MODEL), row),
            pl.BlockSpec((1, D_MODEL), full),
            pl.BlockSpec((1, D_MODEL), full),
            pl.BlockSpec((D_MODEL, D_FF), full),
            pl.BlockSpec((D_FF, D_MODEL), full),
        ],
        out_specs=pl.BlockSpec((tm, D_MODEL), row),
        out_shape=jax.ShapeDtypeStruct((t_total, D_MODEL), F32),
        compiler_params=pltpu.CompilerParams(
            dimension_semantics=("arbitrary",), vmem_limit_bytes=VMEM_LIMIT),
        name="mlp",
    )(h2d, norm_mlp, norm_final, w1, w2)


def _rope_tables(seq):
    half = ROPE_DIM // 2
    pos = jnp.arange(seq, dtype=F32)
    inv_freq = ROPE_THETA ** (-jnp.arange(0, ROPE_DIM, 2, dtype=F32) / ROPE_DIM)
    ang = pos[:, None] * inv_freq[None, :]
    cos, sin = jnp.cos(ang), jnp.sin(ang)
    pad = jnp.zeros((seq, HEAD - ROPE_DIM), F32)
    z8 = jnp.zeros((seq, half), F32)
    c64 = jnp.concatenate([cos, cos, pad + 1.0], axis=1)
    sa64 = jnp.concatenate([z8, sin, pad], axis=1)
    sb64 = jnp.concatenate([-sin, z8, pad], axis=1)
    two = lambda t: jnp.concatenate([t, t], axis=1)
    return two(c64), two(sa64), two(sb64)


def _layer(x, l, prm, norm_final):
    batch, seq, _ = x.shape
    x2d = x.reshape(batch * seq, D_MODEL)
    cos_t, sa_t, sb_t = _rope_tables(seq)
    proj = _inproj(x2d, seq, prm["norm_mix"], prm["w_re"], prm["mu_re"], cos_t, sa_t, sb_t)
    y_dir, s_dir = _wkv_scan(proj, batch, seq, prm["wu"], prm["au"],
                             prm["w0"], prm["a0"], prm["k_k"], prm["k_a"], prm["r_k"], prm["seg"])
    lam_init = 0.8 - 0.6 * math.exp(-0.3 * l)
    o_attn = _diff_attn(proj, batch, seq, prm["lq1"], prm["lk1"], prm["lq2"], prm["lk2"],
                        prm["subln_w"], lam_init)
    h = _mix(x2d, proj, y_dir, s_dir, o_attn, prm["ln_w"], prm["ln_b"], prm["g_up"], prm["expand"],
             prm["proj_a"], prm["proj_b"], prm["w_out"])
    return h


def _prep_layer(l, w_in, mu_shift, w0, w_lora_up, a0, a_lora_up, g_lora_up, k_k, k_a, r_k, ln_x_w,
                ln_x_b, lambda_q1, lambda_k1, lambda_q2, lambda_k2, subln_w, proj_a, proj_b, w_out,
                norm_mix, norm_mlp, w_mlp_in, w_mlp_out):
    w = w_in[l]
    pad_cols = N_GROUPS * GROUP - w.shape[1]
    w_re = jnp.concatenate([w[:, :RWKV_MAIN], w[:, RWKV_COLS:], w[:, RWKV_MAIN:RWKV_COLS],
                            jnp.zeros((D_MODEL, pad_cols), F32)], axis=1).astype(BF16)
    mu = mu_shift[l]
    mu_re = jnp.concatenate([mu[:RWKV_MAIN], jnp.zeros((5 * GROUP,), F32), mu[RWKV_MAIN:],
                             jnp.zeros((pad_cols,), F32)])[None, :]

    def lora_pad(up):
        z = jnp.zeros_like(up[0])
        return jnp.stack([jnp.concatenate([up[0], z], axis=0), jnp.concatenate([z, up[1]], axis=0)])

    wu = lora_pad(w_lora_up[l])
    au = lora_pad(a_lora_up[l])
    head_of_lane = jnp.arange(D_MODEL) // HEAD
    seg = (head_of_lane[:, None] == jnp.arange(128)[None, :]).astype(BF16)
    return dict(
        w_re=w_re, mu_re=mu_re, norm_mix=norm_mix[l][None, :],
        wu=wu.astype(BF16), au=au.astype(BF16),
        w0=w0[l][:, None, :], a0=a0[l][:, None, :],
        k_k=k_k[l][None, :], k_a=k_a[l][None, :], r_k=r_k[l].reshape(1, D_MODEL),
        seg=seg, expand=seg.T,
        lq1=lambda_q1[l][None, :], lk1=lambda_k1[l][None, :],
        lq2=lambda_q2[l][None, :], lk2=lambda_k2[l][None, :],
        subln_w=subln_w[l][None, :],
        ln_w=ln_x_w[l][None, :], ln_b=ln_x_b[l][None, :],
        g_up=g_lora_up[l].astype(BF16),
        proj_a=proj_a[l].astype(BF16), proj_b=proj_b[l].astype(BF16), w_out=w_out[l].astype(BF16),
        norm_mlp=norm_mlp[l][None, :],
        w1=w_mlp_in[l].astype(BF16), w2=w_mlp_out[l].astype(BF16),
    )


def kernel(x_prompt, x_sample, w_in, mu_shift, w0, w_lora_up, a0, a_lora_up, g_lora_up, k_k, k_a, r_k, ln_x_w, ln_x_b, lambda_q1, lambda_k1, lambda_q2, lambda_k2, subln_w, proj_a, proj_b, w_out, norm_mix, norm_mlp, w_mlp_in, w_mlp_out, norm_final):
    depth = w_in.shape[0]
    assert depth == 1, "the final norm is fused into the (single) layer's MLP kernel"
    prm = _prep_layer(0, w_in, mu_shift, w0, w_lora_up, a0, a_lora_up, g_lora_up, k_k, k_a, r_k, ln_x_w,
                      ln_x_b, lambda_q1, lambda_k1, lambda_q2, lambda_k2, subln_w, proj_a, proj_b,
                      w_out, norm_mix, norm_mlp, w_mlp_in, w_mlp_out)
    outs = []
    for x in (x_prompt, x_sample):
        h = _layer(x, 0, prm, norm_final)
        y = _mlp(h, prm["norm_mlp"], norm_final[None, :], prm["w1"], prm["w2"])
        outs.append(y.reshape(x.shape))
    return tuple(outs)
```

```python
import functools
import math

import jax
import jax.numpy as jnp
from jax import lax
from jax.experimental import pallas as pl
from jax.experimental.pallas import tpu as pltpu

F32 = jnp.float32
BF16 = jnp.bfloat16

D_MODEL = 1024
HEAD = 64
N_PAIR = D_MODEL // 128
DECAY_RANK = 64
ICLR_RANK = 64
GATE_RANK = 128
GN_EPS = 64e-5
ROPE_THETA = 500000.0
ROPE_DIM = 16
SUBLN_EPS = 1e-5
NORM_EPS = 1e-6
D_FF = 4 * D_MODEL
RWKV_MAIN = 3 * D_MODEL
RWKV_COLS = RWKV_MAIN + 2 * DECAY_RANK + 2 * ICLR_RANK + GATE_RANK
GROUP = 1024
N_GROUPS = 9
G_Q, G_KB, G_VB, G_GA, G_GB, G_LORA = 3, 4, 5, 6, 7, 8
GROUPS_PER_STEP = 3
LORA_BLK = G_LORA * GROUP // 128
CHUNK = 64
EXP_M05 = math.exp(-0.5)
LOG2_E = math.log2(math.e)
VMEM_LIMIT = 56 * 1024 * 1024
TM_INPROJ = 512
TM_MIX = 512
TM_MLP = 512
FF_CHUNK = 1024
TQ_ATTN = 512
TK_ATTN = 256

NN = (((1,), (0,)), ((), ()))
NT = (((1,), (1,)), ((), ()))
TN = (((0,), (0,)), ((), ()))


def _dot(a, b, dims=NN):
    return lax.dot_general(a, b, dims, preferred_element_type=F32)


def _split(x):
    hi = x.astype(BF16)
    lo = (x - hi.astype(F32)).astype(BF16)
    return hi, lo


def _sigmoid(x):
    return 1.0 / (1.0 + jnp.exp(-x))


def _rms(x, g):
    return x * lax.rsqrt(jnp.mean(x * x, axis=-1, keepdims=True) + NORM_EPS) * g


def _inproj_kernel(x_ref, xp_ref, xn_ref, g_ref, w_ref, mu_ref, cos_ref, sa_ref, sb_ref,
                   o_ref, xs_ref, hs_ref, *, tm, tiles_per_seq):
    j = pl.program_id(0)
    i = pl.program_id(1)
    g = g_ref[...]
    xs_ref[...] = _rms(x_ref[...], g).astype(BF16)
    hs_ref[0:8, :] = _rms(xp_ref[...], g)
    hs_ref[8:16, :] = _rms(xn_ref[...], g)

    def store_shifted(p, w, cols):
        ph = _dot(hs_ref[...].astype(BF16), w)
        t_in_seq = i % tiles_per_seq
        prev = jnp.where(t_in_seq == 0, 0.0, ph[7:8, :])
        nxt = jnp.where(t_in_seq == tiles_per_seq - 1, 0.0, ph[8:9, :])
        c2 = 0.5 * mu_ref[:, cols]
        c1 = 1.0 - mu_ref[:, cols]
        core = p * c1 + (pltpu.roll(p, 1, 0) + pltpu.roll(p, tm - 1, 0)) * c2
        o_ref[:, cols] = core
        o_ref[0:1, cols] = core[0:1, :] + (prev - p[tm - 1:tm, :]) * c2
        o_ref[tm - 1:tm, cols] = core[tm - 1:tm, :] + (nxt - p[0:1, :]) * c2

    def rotated(p, scale):
        c = jnp.tile(cos_ref[...], (1, N_PAIR))
        sa = jnp.tile(sa_ref[...], (1, N_PAIR))
        sb = jnp.tile(sb_ref[...], (1, N_PAIR))
        half = ROPE_DIM // 2
        out = p * c + pltpu.roll(p, half, 1) * sa + pltpu.roll(p, GROUP - half, 1) * sb
        return out if scale == 1.0 else out * scale

    for step in range(N_GROUPS // GROUPS_PER_STEP):
        @pl.when(j == step)
        def _(step=step):
            for gi in range(GROUPS_PER_STEP):
                g = step * GROUPS_PER_STEP + gi
                cols = slice(gi * GROUP, (gi + 1) * GROUP)
                w = w_ref[:, cols]
                p = _dot(xs_ref[...], w)
                if g < 3 or g == G_LORA:
                    store_shifted(p, w, cols)
                elif g == G_Q:
                    o_ref[:, cols] = rotated(p, HEAD ** -0.5 * LOG2_E)
                elif g == G_KB:
                    o_ref[:, cols] = rotated(p, 1.0)
                else:
                    o_ref[:, cols] = p


def _inproj(x2d, seq, g, w_re, mu_re, cos_t, sa_t, sb_t):
    t_total = x2d.shape[0]
    tm = min(TM_INPROJ, seq)
    tiles_per_seq = seq // tm
    n_tiles = t_total // tm
    last8 = t_total // 8 - 1
    kern = functools.partial(_inproj_kernel, tm=tm, tiles_per_seq=tiles_per_seq)
    return pl.pallas_call(
        kern,
        grid=(N_GROUPS // GROUPS_PER_STEP, n_tiles),
        in_specs=[
            pl.BlockSpec((tm, D_MODEL), lambda j, i: (i, 0)),
            pl.BlockSpec((8, D_MODEL), lambda j, i: (jnp.maximum(i * (tm // 8) - 1, 0), 0)),
            pl.BlockSpec((8, D_MODEL), lambda j, i: (jnp.minimum((i + 1) * (tm // 8), last8), 0)),
            pl.BlockSpec((1, D_MODEL), lambda j, i: (0, 0)),
            pl.BlockSpec((D_MODEL, GROUPS_PER_STEP * GROUP), lambda j, i: (0, j)),
            pl.BlockSpec((1, GROUPS_PER_STEP * GROUP), lambda j, i: (0, j)),
            pl.BlockSpec((tm, 128), lambda j, i: (i % tiles_per_seq, 0)),
            pl.BlockSpec((tm, 128), lambda j, i: (i % tiles_per_seq, 0)),
            pl.BlockSpec((tm, 128), lambda j, i: (i % tiles_per_seq, 0)),
        ],
        out_specs=pl.BlockSpec((tm, GROUPS_PER_STEP * GROUP), lambda j, i: (i, j)),
        out_shape=jax.ShapeDtypeStruct((t_total, N_GROUPS * GROUP), F32),
        scratch_shapes=[pltpu.VMEM((tm, D_MODEL), BF16), pltpu.VMEM((16, D_MODEL), F32)],
        compiler_params=pltpu.CompilerParams(
            dimension_semantics=("arbitrary", "arbitrary"), vmem_limit_bytes=VMEM_LIMIT),
        name="inproj",
    )(x2d, x2d, x2d, g, w_re, mu_re, cos_t, sa_t, sb_t)


QUAD = 256
N_QUAD = D_MODEL // QUAD
HEADS_PER_QUAD = QUAD // HEAD
SCAN_SUB = 8


def _scan_kernel(r_ref, k_ref, v_ref, lw_ref, la_ref, wu_ref, au_ref,
                 w0_ref, a0_ref, kk_ref, ka_ref, rk_ref, seg_ref, y_ref, s_ref, h_ref, *, n_sub):
    d = pl.program_id(1)
    c = pl.program_id(2)
    C = CHUNK
    bf = lambda x: x.astype(BF16)

    @pl.when(c == 0)
    def _():
        h_ref[...] = jnp.zeros_like(h_ref)

    sgn = 1 - 2 * d
    rr = lax.broadcasted_iota(jnp.int32, (C, C), 0)
    cc = lax.broadcasted_iota(jnp.int32, (C, C), 1)
    tri = jnp.where((rr - cc) * sgn >= 0, 1.0, 0.0).astype(BF16)

    tr = lax.broadcasted_iota(jnp.int32, (C, QUAD), 0)
    tc = lax.broadcasted_iota(jnp.int32, (C, QUAD), 1) % C
    dd = (tr - tc) * sgn
    strict = dd > 0
    incl = dd >= 0
    eye = jnp.where(dd == 0, 1.0, 0.0)
    levels = []
    m = 1
    while m < C:
        levels.append(jnp.logical_and((tr // (2 * m)) == (tc // (2 * m)), (tr // m) != (tc // m)))
        m *= 2
    lane_head = lax.broadcasted_iota(jnp.int32, (1, QUAD), 1) // HEAD
    head_lanes = [lane_head == h for h in range(HEADS_PER_QUAD)]
    same_head = (lax.broadcasted_iota(jnp.int32, (QUAD, QUAD), 0) // HEAD) == (
        lax.broadcasted_iota(jnp.int32, (QUAD, QUAD), 1) // HEAD)
    bd_ones = jnp.where(same_head, 1.0, 0.0).astype(BF16)

    def blockdiag(x):
        return jnp.concatenate([jnp.where(hl, x, 0.0) for hl in head_lanes], axis=0)

    quads = range(N_QUAD)
    sls = [slice(QUAD * q, QUAD * (q + 1)) for q in quads]

    staged = []
    for i in range(n_sub):
        jj = i + d * (n_sub - 1 - 2 * i)
        rows = pl.ds(pl.multiple_of(jj * C, C), C)
        r = r_ref[rows, :]
        k = k_ref[rows, :]
        v = v_ref[rows, :]

        w_raw = w0_ref[0] + _dot(bf(jnp.tanh(lw_ref[rows, :])), wu_ref[0])
        logw = (-EXP_M05 * LOG2_E) * _sigmoid(w_raw)
        a = _sigmoid(a0_ref[0] + _dot(bf(la_ref[rows, :]), au_ref[0]))

        l1, l2 = _split(logw)
        cum = _dot(tri, l1) + _dot(tri, l2)
        total = jnp.sum(logw, axis=0, keepdims=True)
        g_incl = jnp.exp2(cum)
        g_excl = jnp.exp2(cum - logw)
        g_inv = jnp.exp2(-cum)
        g_tail = jnp.exp2(total - cum)
        g_tot = jnp.exp2(total)

        kkv = k * kk_ref[...]
        kd = k * (1.0 + (a - 1.0) * ka_ref[...])
        sq = kkv * kkv

        s_ref[0, rows, :] = _dot(bf(r * kd * rk_ref[...]), seg_ref[...])

        ss = [_dot(bf(sq[:, sl]), bd_ones) for sl in sls]
        kkn = [kkv[:, sl] * lax.rsqrt(jnp.maximum(s, 1e-24)) for sl, s in zip(sls, ss)]
        bvec = [kn * a[:, sl] for sl, kn in zip(sls, kkn)]
        lhs = [bf(jnp.concatenate([-kn * g_excl[:, sl], r[:, sl] * g_incl[:, sl]], axis=0))
               for sl, kn in zip(sls, kkn)]
        rhs = [jnp.concatenate([blockdiag(bf(bv * g_inv[:, sl])), blockdiag(bf(kd[:, sl] * g_inv[:, sl]))], axis=0)
               for sl, bv in zip(sls, bvec)]
        bkh = [bf(jnp.concatenate([bv * g_tail[:, sl], kd[:, sl] * g_tail[:, sl]], axis=0))
               for sl, bv in zip(sls, bvec)]
        vb = [bf(v[:, sl]) for sl in sls]
        v_bd = [blockdiag(x) for x in vb]

        gm = [_dot(lhs[q], rhs[q], NT) for q in quads]
        gb = [bf(g) for g in gm]
        a_ab = [jnp.where(strict, g[0:C, 0:QUAD], 0.0) for g in gm]
        a_abb = [jnp.where(strict, g[0:C, 0:QUAD], 0.0) for g in gb]
        a_ak = [jnp.where(strict, g[0:C, QUAD:2 * QUAD], 0.0) for g in gb]
        a_r = [jnp.concatenate([jnp.where(incl, g[C:2 * C, 0:QUAD], 0.0),
                                jnp.where(incl, g[C:2 * C, QUAD:2 * QUAD], 0.0)], axis=1) for g in gb]
        akv = [_dot(a_ak[q], v_bd[q]) for q in quads]
        staged.append(dict(rows=rows, lhs=lhs, a_ab=a_ab, a_abb=a_abb, a_r=a_r, akv=akv, vb=vb, v_bd=v_bd,
                           bkh=bkh, g_tot=g_tot))

    combos = [(i, q) for i in range(n_sub) for q in quads]
    tinv = {(i, q): eye + jnp.where(levels[0], staged[i]["a_ab"][q], 0.0) for i, q in combos}
    for lvl in levels[1:]:
        tb = {key: bf(t) for key, t in tinv.items()}
        wm = {(i, q): _dot(jnp.where(lvl, staged[i]["a_abb"][q], 0.0), blockdiag(tb[i, q])) for i, q in combos}
        tinv = {key: tinv[key] + _dot(tb[key], blockdiag(bf(wm[key]))) for key in combos}

    ht = [h_ref[q] for q in quads]
    for i in range(n_sub):
        st = staged[i]
        ar = [_dot(st["lhs"][q], bf(ht[q]), NT) for q in quads]
        rhs_u = [ar[q][0:C] + st["akv"][q] for q in quads]
        u = [bf(_dot(bf(tinv[i, q]), blockdiag(bf(rhs_u[q])))) for q in quads]
        uv_bd = [jnp.concatenate([blockdiag(u[q]), st["v_bd"][q]], axis=0) for q in quads]
        for q in quads:
            y_ref[0, st["rows"], sls[q]] = (ar[q][C:2 * C] + _dot(st["a_r"][q], uv_bd[q])).astype(y_ref.dtype)
        upd = [_dot(jnp.concatenate([u[q], st["vb"][q]], axis=0), st["bkh"][q], TN) for q in quads]
        ht = [ht[q] * st["g_tot"][:, sls[q]] + jnp.where(same_head, upd[q], 0.0) for q in quads]

    for q in quads:
        h_ref[q] = ht[q]


def _wkv_scan(proj, batch, seq, wu, au, w0, a0, k_k, k_a, r_k, seg):
    t_total = batch * seq
    n_sub = math.gcd(seq // CHUNK, SCAN_SUB)
    blk = n_sub * CHUNK
    nc = seq // blk

    def rowblk(b, d, c):
        return b * nc + c + d * (nc - 1 - 2 * c)

    def col(jblk):
        return lambda b, d, c: (rowblk(b, d, c), jblk)

    full = lambda b, d, c: (0, 0)
    perdir = lambda b, d, c: (d, 0, 0)
    return pl.pallas_call(
        functools.partial(_scan_kernel, n_sub=n_sub),
        grid=(batch, 2, nc),
        in_specs=[
            pl.BlockSpec((blk, GROUP), col(0)),
            pl.BlockSpec((blk, GROUP), col(1)),
            pl.BlockSpec((blk, GROUP), col(2)),
            pl.BlockSpec((blk, 128), col(LORA_BLK)),
            pl.BlockSpec((blk, 128), col(LORA_BLK + 1)),
            pl.BlockSpec((1, 128, D_MODEL), perdir),
            pl.BlockSpec((1, 128, D_MODEL), perdir),
            pl.BlockSpec((1, 1, D_MODEL), perdir),
            pl.BlockSpec((1, 1, D_MODEL), perdir),
            pl.BlockSpec((1, D_MODEL), full),
            pl.BlockSpec((1, D_MODEL), full),
            pl.BlockSpec((1, D_MODEL), full),
            pl.BlockSpec((D_MODEL, 128), full),
        ],
        out_specs=[
            pl.BlockSpec((1, blk, D_MODEL), lambda b, d, c: (d, rowblk(b, d, c), 0)),
            pl.BlockSpec((1, blk, 128), lambda b, d, c: (d, rowblk(b, d, c), 0)),
        ],
        out_shape=[
            jax.ShapeDtypeStruct((2, t_total, D_MODEL), BF16),
            jax.ShapeDtypeStruct((2, t_total, 128), F32),
        ],
        scratch_shapes=[pltpu.VMEM((N_QUAD, QUAD, QUAD), F32)],
        compiler_params=pltpu.CompilerParams(
            dimension_semantics=("arbitrary", "arbitrary", "arbitrary"), vmem_limit_bytes=VMEM_LIMIT),
        name="wkv_scan",
    )(proj, proj, proj, proj, proj, wu, au, w0, a0, k_k, k_a, r_k, seg)


V_ROWS = 128 + 16
PV_GROUP = 2


def _attn_kernel(q_ref, k_ref, v_ref, lq1_ref, lk1_ref, lq2_ref, lk2_ref, sw_ref, o_ref,
                 k0_ref, k1_ref, vt_ref, s_ref, m_ref, p_ref, *, seq, tq, tk, per_trip, nq, lam_init):
    i = pl.program_id(2)
    n_kc = seq // tk
    lane = lax.broadcasted_iota(jnp.int32, (1, 128), 1)
    first = lane < HEAD

    def prepare(rows):
        kf = k_ref[rows, :]
        k0_ref[rows, :] = jnp.where(first, kf, 0.0).astype(BF16)
        k1_ref[rows, :] = jnp.where(first, 0.0, kf).astype(BF16)
        vt_ref[0:128, rows] = v_ref[rows, :].T.astype(BF16)
        vt_ref[128:V_ROWS, rows] = jnp.ones((V_ROWS - 128, tk), BF16)

    def rows_of(kc):
        return pl.ds(pl.multiple_of(kc * tk, tk), tk)

    def fold(op, s):
        return op(s.reshape(tk // 8, 8, tq), axis=0)

    def query_t():
        return q_ref[...].T.astype(BF16)

    neg = jnp.full((8, tq), -jnp.inf, F32)
    za = jnp.zeros((V_ROWS, tq), F32)
    n_trips = n_kc // per_trip

    def finish(acc0, acc1):
        l0 = acc0[128:129, :]
        l1 = acc1[128:129, :]
        lam = (jnp.exp(jnp.sum(lq1_ref[...] * lk1_ref[...], axis=-1, keepdims=True))
               - jnp.exp(jnp.sum(lq2_ref[...] * lk2_ref[...], axis=-1, keepdims=True)) + lam_init)
        o_t = acc0[0:128, :] * (1.0 / l0) - acc1[0:128, :] * (lam / l1)
        o_t = o_t * lax.rsqrt(jnp.mean(o_t * o_t, axis=0, keepdims=True) + SUBLN_EPS)
        o = o_t.T * (sw_ref[...] * (1.0 - lam_init))
        o_ref[...] = o.astype(o_ref.dtype)

    def run(slot_a):
        slot_b = 1 - slot_a
        mine = (i % 2) == slot_a

        def trip(t, q_t, mx, m_prev, acc, first_tile=False):
            for c in range(per_trip):
                rows = rows_of(t * per_trip + c)
                if first_tile:
                    prepare(rows)
                if q_t is not None:
                    s0 = _dot(k0_ref[rows, :], q_t)
                    s1 = _dot(k1_ref[rows, :], q_t)
                    s_ref[slot_a, 0, rows, :] = s0
                    s_ref[slot_a, 1, rows, :] = s1
                    mx = (jnp.maximum(mx[0], fold(jnp.max, s0)), jnp.maximum(mx[1], fold(jnp.max, s1)))
                if acc is not None:
                    for comp in range(2):
                        x = s_ref[slot_b, comp, rows, :] - m_prev[comp]
                        p_ref[comp, c * tk:(c + 1) * tk, :] = jnp.exp2(x.astype(BF16))
                    if (c + 1) % PV_GROUP == 0 or c + 1 == per_trip:
                        c0 = (c // PV_GROUP) * PV_GROUP
                        span = (c + 1 - c0) * tk
                        start = (t * per_trip + c0) * tk
                        vt = vt_ref[:, pl.ds(pl.multiple_of(start, tk), span)]
                        acc = tuple(acc[comp] + _dot(vt, p_ref[comp, c0 * tk:(c + 1) * tk, :])
                                    for comp in range(2))
            return mx, acc

        def loop(body, init):
            if n_trips == 1:
                return body(0, init)
            return lax.fori_loop(0, n_trips, body, init)

        def prev_max():
            return (jnp.max(m_ref[slot_b, 0], axis=0, keepdims=True),
                    jnp.max(m_ref[slot_b, 1], axis=0, keepdims=True))

        def store_max(mx):
            m_ref[slot_a, 0] = mx[0]
            m_ref[slot_a, 1] = mx[1]

        if slot_a == 0:
            @pl.when(i == 0)
            def _():
                q_t = query_t()
                store_max(loop(lambda t, mx: trip(t, q_t, mx, None, None, first_tile=True)[0], (neg, neg)))

        @pl.when(jnp.logical_and(mine, jnp.logical_and(i > 0, i < nq)))
        def _():
            q_t = query_t()
            m_prev = prev_max()
            mx, acc = loop(lambda t, c: trip(t, q_t, c[0], m_prev, c[1]), ((neg, neg), (za, za)))
            store_max(mx)
            finish(*acc)

        if nq % 2 == slot_a:
            @pl.when(i == nq)
            def _():
                m_prev = prev_max()
                finish(*loop(lambda t, acc: trip(t, None, None, m_prev, acc)[1], (za, za)))

    run(0)
    run(1)


def _diff_attn(proj, batch, seq, lq1, lk1, lq2, lk2, subln_w, lam_init):
    t_total = batch * seq
    tq = min(TQ_ATTN, seq)
    tk = min(TK_ATTN, seq)
    per_trip = math.gcd(seq // tk, 16)
    nq = seq // tq
    n_heads = N_PAIR
    qb, kb, vb = (G_Q * GROUP // 128, G_KB * GROUP // 128, G_VB * GROUP // 128)
    kern = functools.partial(_attn_kernel, seq=seq, tq=tq, tk=tk, per_trip=per_trip, nq=nq, lam_init=lam_init)
    small = lambda b, h, i: (0, 0)
    return pl.pallas_call(
        kern,
        grid=(batch, n_heads, nq + 1),
        in_specs=[
            pl.BlockSpec((tq, 128), lambda b, h, i: (b * nq + jnp.minimum(i, nq - 1), qb + h)),
            pl.BlockSpec((seq, 128), lambda b, h, i: (b, kb + h)),
            pl.BlockSpec((seq, 128), lambda b, h, i: (b, vb + h)),
            pl.BlockSpec((1, HEAD), small),
            pl.BlockSpec((1, HEAD), small),
            pl.BlockSpec((1, HEAD), small),
            pl.BlockSpec((1, HEAD), small),
            pl.BlockSpec((1, 128), small),
        ],
        out_specs=pl.BlockSpec((tq, 128), lambda b, h, i: (b * nq + jnp.maximum(i - 1, 0), h)),
        out_shape=jax.ShapeDtypeStruct((t_total, D_MODEL), BF16),
        scratch_shapes=[pltpu.VMEM((seq, 128), BF16), pltpu.VMEM((seq, 128), BF16),
                        pltpu.VMEM((V_ROWS, seq), BF16), pltpu.VMEM((2, 2, seq, tq), F32),
                        pltpu.VMEM((2, 2, 8, tq), F32), pltpu.VMEM((2, per_trip * tk, tq), BF16)],
        compiler_params=pltpu.CompilerParams(
            dimension_semantics=("arbitrary", "arbitrary", "arbitrary"), vmem_limit_bytes=VMEM_LIMIT),
        name="diff_attn",
    )(proj, proj, proj, lq1, lk1, lq2, lk2, subln_w)


def _mix_kernel(x_ref, yf_ref, yb_ref, sf_ref, sb_ref, v_ref, lg_ref, ga_ref, gb_ref, o_ref,
                lnw_ref, lnb_ref, gup_ref, exp_ref, pa_ref, pb_ref, wo_ref, h_ref, ya_ref):
    y = yf_ref[0].astype(F32) + yb_ref[0].astype(F32)
    rh = lax.broadcasted_iota(jnp.int32, (QUAD, QUAD), 0) // HEAD
    ch = lax.broadcasted_iota(jnp.int32, (QUAD, QUAD), 1) // HEAD
    avg = jnp.where(rh == ch, 1.0 / HEAD, 0.0).astype(BF16)
    for q in range(N_QUAD):
        sl = slice(QUAD * q, QUAD * (q + 1))
        yq = y[:, sl]
        cen = yq - _dot(yq.astype(BF16), avg)
        var = _dot((cen * cen).astype(BF16), avg)
        ya_ref[:, sl] = cen * lax.rsqrt(var + GN_EPS)
    coef = _dot((sf_ref[0] + sb_ref[0]).astype(BF16), exp_ref[...])
    gate = _dot(_sigmoid(lg_ref[...]).astype(BF16), gup_ref[...])
    y_a = (ya_ref[...] * lnw_ref[...] + lnb_ref[...] + coef * v_ref[...]) * gate
    pa = _dot(y_a.astype(BF16), pa_ref[...])
    pb = _dot(o_ref[...], pb_ref[...])
    merged = _sigmoid(ga_ref[...]) * pa + _sigmoid(gb_ref[...]) * pb
    h_ref[...] = x_ref[...] + _dot(merged.astype(BF16), wo_ref[...])


def _mix(x2d, proj, y_dir, s_dir, o_attn, ln_w, ln_b, g_up, expand, proj_a, proj_b, w_out):
    t_total = x2d.shape[0]
    tm = TM_MIX
    row = lambda i: (i, 0)
    full = lambda i: (0, 0)
    wspec = pl.BlockSpec((D_MODEL, D_MODEL), full)
    return pl.pallas_call(
        _mix_kernel,
        grid=(t_total // tm,),
        in_specs=[
            pl.BlockSpec((tm, D_MODEL), row),
            pl.BlockSpec((1, tm, D_MODEL), lambda i: (0, i, 0)),
            pl.BlockSpec((1, tm, D_MODEL), lambda i: (1, i, 0)),
            pl.BlockSpec((1, tm, 128), lambda i: (0, i, 0)),
            pl.BlockSpec((1, tm, 128), lambda i: (1, i, 0)),
            pl.BlockSpec((tm, GROUP), lambda i: (i, 2)),
            pl.BlockSpec((tm, 128), lambda i: (i, LORA_BLK + 2)),
            pl.BlockSpec((tm, GROUP), lambda i: (i, G_GA)),
            pl.BlockSpec((tm, GROUP), lambda i: (i, G_GB)),
            pl.BlockSpec((tm, D_MODEL), row),
            pl.BlockSpec((1, D_MODEL), full),
            pl.BlockSpec((1, D_MODEL), full),
            pl.BlockSpec((GATE_RANK, D_MODEL), full),
            pl.BlockSpec((128, D_MODEL), full),
            wspec, wspec, wspec,
        ],
        out_specs=pl.BlockSpec((tm, D_MODEL), row),
        out_shape=jax.ShapeDtypeStruct((t_total, D_MODEL), F32),
        scratch_shapes=[pltpu.VMEM((tm, D_MODEL), F32)],
        compiler_params=pltpu.CompilerParams(
            dimension_semantics=("arbitrary",), vmem_limit_bytes=VMEM_LIMIT),
        name="mix",
    )(x2d, y_dir, y_dir, s_dir, s_dir, proj, proj, proj, proj, o_attn,
      ln_w, ln_b, g_up, expand, proj_a, proj_b, w_out)


def _mlp_kernel(h_ref, gm_ref, gf_ref, w1_ref, w2_ref, o_ref, *, ff_chunk):
    h = h_ref[...]
    xn = _rms(h, gm_ref[...]).astype(BF16)
    acc = h
    for c in range(D_FF // ff_chunk):
        sl = slice(c * ff_chunk, (c + 1) * ff_chunk)
        hid = jnp.maximum(_dot(xn, w1_ref[:, sl]), 0.0)
        acc = acc + _dot((hid * hid).astype(BF16), w2_ref[sl, :])
    o_ref[...] = _rms(acc, gf_ref[...])


def _mlp(h2d, norm_mlp, norm_final, w1, w2):
    t_total = h2d.shape[0]
    tm = TM_MLP
    row = lambda i: (i, 0)
    full = lambda i: (0, 0)
    kern = functools.partial(_mlp_kernel, ff_chunk=FF_CHUNK)
    return pl.pallas_call(
        kern,
        grid=(t_total // tm,),
        in_specs=[
            pl.BlockSpec((tm, D_MODEL), row),
            pl.BlockSpec((1, D_MODEL), full),
            pl.BlockSpec((1, D_MODEL), full),
            pl.BlockSpec((D_MODEL, D_FF), full),
            pl.BlockSpec((D_FF, D_MODEL), full),
        ],
        out_specs=pl.BlockSpec((tm, D_MODEL), row),
        out_shape=jax.ShapeDtypeStruct((t_total, D_MODEL), F32),
        compiler_params=pltpu.CompilerParams(
            dimension_semantics=("arbitrary",), vmem_limit_bytes=VMEM_LIMIT),
        name="mlp",
    )(h2d, norm_mlp, norm_final, w1, w2)


def _rope_tables(seq):
    half = ROPE_DIM // 2
    pos = jnp.arange(seq, dtype=F32)
    inv_freq = ROPE_THETA ** (-jnp.arange(0, ROPE_DIM, 2, dtype=F32) / ROPE_DIM)
    ang = pos[:, None] * inv_freq[None, :]
    cos, sin = jnp.cos(ang), jnp.sin(ang)
    pad = jnp.zeros((seq, HEAD - ROPE_DIM), F32)
    z8 = jnp.zeros((seq, half), F32)
    c64 = jnp.concatenate([cos, cos, pad + 1.0], axis=1)
    sa64 = jnp.concatenate([z8, sin, pad], axis=1)
    sb64 = jnp.concatenate([-sin, z8, pad], axis=1)
    two = lambda t: jnp.concatenate([t, t], axis=1)
    return two(c64), two(sa64), two(sb64)


def _layer(x, l, prm, norm_final):
    batch, seq, _ = x.shape
    x2d = x.reshape(batch * seq, D_MODEL)
    cos_t, sa_t, sb_t = _rope_tables(seq)
    proj = _inproj(x2d, seq, prm["norm_mix"], prm["w_re"], prm["mu_re"], cos_t, sa_t, sb_t)
    y_dir, s_dir = _wkv_scan(proj, batch, seq, prm["wu"], prm["au"],
                             prm["w0"], prm["a0"], prm["k_k"], prm["k_a"], prm["r_k"], prm["seg"])
    lam_init = 0.8 - 0.6 * math.exp(-0.3 * l)
    o_attn = _diff_attn(proj, batch, seq, prm["lq1"], prm["lk1"], prm["lq2"], prm["lk2"],
                        prm["subln_w"], lam_init)
    h = _mix(x2d, proj, y_dir, s_dir, o_attn, prm["ln_w"], prm["ln_b"], prm["g_up"], prm["expand"],
             prm["proj_a"], prm["proj_b"], prm["w_out"])
    return h


def _prep_layer(l, w_in, mu_shift, w0, w_lora_up, a0, a_lora_up, g_lora_up, k_k, k_a, r_k, ln_x_w,
                ln_x_b, lambda_q1, lambda_k1, lambda_q2, lambda_k2, subln_w, proj_a, proj_b, w_out,
                norm_mix, norm_mlp, w_mlp_in, w_mlp_out):
    w = w_in[l]
    pad_cols = N_GROUPS * GROUP - w.shape[1]
    w_re = jnp.concatenate([w[:, :RWKV_MAIN], w[:, RWKV_COLS:], w[:, RWKV_MAIN:RWKV_COLS],
                            jnp.zeros((D_MODEL, pad_cols), F32)], axis=1).astype(BF16)
    mu = mu_shift[l]
    mu_re = jnp.concatenate([mu[:RWKV_MAIN], jnp.zeros((5 * GROUP,), F32), mu[RWKV_MAIN:],
                             jnp.zeros((pad_cols,), F32)])[None, :]

    def lora_pad(up):
        z = jnp.zeros_like(up[0])
        return jnp.stack([jnp.concatenate([up[0], z], axis=0), jnp.concatenate([z, up[1]], axis=0)])

    wu = lora_pad(w_lora_up[l])
    au = lora_pad(a_lora_up[l])
    head_of_lane = jnp.arange(D_MODEL) // HEAD
    seg = (head_of_lane[:, None] == jnp.arange(128)[None, :]).astype(BF16)
    return dict(
        w_re=w_re, mu_re=mu_re, norm_mix=norm_mix[l][None, :],
        wu=wu.astype(BF16), au=au.astype(BF16),
        w0=w0[l][:, None, :], a0=a0[l][:, None, :],
        k_k=k_k[l][None, :], k_a=k_a[l][None, :], r_k=r_k[l].reshape(1, D_MODEL),
        seg=seg, expand=seg.T,
        lq1=lambda_q1[l][None, :], lk1=lambda_k1[l][None, :],
        lq2=lambda_q2[l][None, :], lk2=lambda_k2[l][None, :],
        subln_w=subln_w[l][None, :],
        ln_w=ln_x_w[l][None, :], ln_b=ln_x_b[l][None, :],
        g_up=g_lora_up[l].astype(BF16),
        proj_a=proj_a[l].astype(BF16), proj_b=proj_b[l].astype(BF16), w_out=w_out[l].astype(BF16),
        norm_mlp=norm_mlp[l][None, :],
        w1=w_mlp_in[l].astype(BF16), w2=w_mlp_out[l].astype(BF16),
    )


def kernel(x_prompt, x_sample, w_in, mu_shift, w0, w_lora_up, a0, a_lora_up, g_lora_up, k_k, k_a, r_k, ln_x_w, ln_x_b, lambda_q1, lambda_k1, lambda_q2, lambda_k2, subln_w, proj_a, proj_b, w_out, norm_mix, norm_mlp, w_mlp_in, w_mlp_out, norm_final):
    depth = w_in.shape[0]
    assert depth == 1, "the final norm is fused into the (single) layer's MLP kernel"
    prm = _prep_layer(0, w_in, mu_shift, w0, w_lora_up, a0, a_lora_up, g_lora_up, k_k, k_a, r_k, ln_x_w,
                      ln_x_b, lambda_q1, lambda_k1, lambda_q2, lambda_k2, subln_w, proj_a, proj_b,
                      w_out, norm_mix, norm_mlp, w_mlp_in, w_mlp_out)
    outs = []
    for x in (x_prompt, x_sample):
        h = _layer(x, 0, prm, norm_final)
        y = _mlp(h, prm["norm_mlp"], norm_final[None, :], prm["w1"], prm["w2"])
        outs.append(y.reshape(x.shape))
    return tuple(outs)
```

```python
import functools
import math

import jax
import jax.numpy as jnp
from jax import lax
from jax.experimental import pallas as pl
from jax.experimental.pallas import tpu as pltpu

F32 = jnp.float32
BF16 = jnp.bfloat16

D_MODEL = 1024
HEAD = 64
N_PAIR = D_MODEL // 128
DECAY_RANK = 64
ICLR_RANK = 64
GATE_RANK = 128
GN_EPS = 64e-5
ROPE_THETA = 500000.0
ROPE_DIM = 16
SUBLN_EPS = 1e-5
NORM_EPS = 1e-6
D_FF = 4 * D_MODEL
RWKV_MAIN = 3 * D_MODEL
RWKV_COLS = RWKV_MAIN + 2 * DECAY_RANK + 2 * ICLR_RANK + GATE_RANK
GROUP = 1024
N_GROUPS = 9
G_Q, G_KB, G_VB, G_GA, G_GB, G_LORA = 3, 4, 5, 6, 7, 8
GROUPS_PER_STEP = 3
LORA_BLK = G_LORA * GROUP // 128
CHUNK = 64
EXP_M05 = math.exp(-0.5)
LOG2_E = math.log2(math.e)
VMEM_LIMIT = 56 * 1024 * 1024
TM_INPROJ = 512
TM_MIX = 512
TM_MLP = 1024
FF_CHUNK = 1024
TQ_ATTN = 512
TK_ATTN = 256

NN = (((1,), (0,)), ((), ()))
NT = (((1,), (1,)), ((), ()))
TN = (((0,), (0,)), ((), ()))


def _dot(a, b, dims=NN):
    return lax.dot_general(a, b, dims, preferred_element_type=F32)


def _split(x):
    hi = x.astype(BF16)
    lo = (x - hi.astype(F32)).astype(BF16)
    return hi, lo


def _sigmoid(x):
    return 1.0 / (1.0 + jnp.exp(-x))


def _rms(x, g):
    return x * lax.rsqrt(jnp.mean(x * x, axis=-1, keepdims=True) + NORM_EPS) * g


def _inproj_kernel(x_ref, xp_ref, xn_ref, g_ref, w_ref, mu_ref, cos_ref, sa_ref, sb_ref,
                   o_ref, xs_ref, hs_ref, *, tm, tiles_per_seq):
    j = pl.program_id(0)
    i = pl.program_id(1)
    g = g_ref[...]
    xs_ref[...] = _rms(x_ref[...], g).astype(BF16)
    hs_ref[0:8, :] = _rms(xp_ref[...], g)
    hs_ref[8:16, :] = _rms(xn_ref[...], g)

    def store_shifted(p, w, cols):
        ph = _dot(hs_ref[...].astype(BF16), w)
        t_in_seq = i % tiles_per_seq
        prev = jnp.where(t_in_seq == 0, 0.0, ph[7:8, :])
        nxt = jnp.where(t_in_seq == tiles_per_seq - 1, 0.0, ph[8:9, :])
        c2 = 0.5 * mu_ref[:, cols]
        c1 = 1.0 - mu_ref[:, cols]
        core = p * c1 + (pltpu.roll(p, 1, 0) + pltpu.roll(p, tm - 1, 0)) * c2
        o_ref[:, cols] = core
        o_ref[0:1, cols] = core[0:1, :] + (prev - p[tm - 1:tm, :]) * c2
        o_ref[tm - 1:tm, cols] = core[tm - 1:tm, :] + (nxt - p[0:1, :]) * c2

    def rotated(p, scale):
        c = jnp.tile(cos_ref[...], (1, N_PAIR))
        sa = jnp.tile(sa_ref[...], (1, N_PAIR))
        sb = jnp.tile(sb_ref[...], (1, N_PAIR))
        half = ROPE_DIM // 2
        out = p * c + pltpu.roll(p, half, 1) * sa + pltpu.roll(p, GROUP - half, 1) * sb
        return out if scale == 1.0 else out * scale

    for step in range(N_GROUPS // GROUPS_PER_STEP):
        @pl.when(j == step)
        def _(step=step):
            for gi in range(GROUPS_PER_STEP):
                g = step * GROUPS_PER_STEP + gi
                cols = slice(gi * GROUP, (gi + 1) * GROUP)
                w = w_ref[:, cols]
                p = _dot(xs_ref[...], w)
                if g < 3 or g == G_LORA:
                    store_shifted(p, w, cols)
                elif g == G_Q:
                    o_ref[:, cols] = rotated(p, HEAD ** -0.5 * LOG2_E)
                elif g == G_KB:
                    o_ref[:, cols] = rotated(p, 1.0)
                else:
                    o_ref[:, cols] = p


def _inproj(x2d, seq, g, w_re, mu_re, cos_t, sa_t, sb_t):
    t_total = x2d.shape[0]
    tm = min(TM_INPROJ, seq)
    tiles_per_seq = seq // tm
    n_tiles = t_total // tm
    last8 = t_total // 8 - 1
    kern = functools.partial(_inproj_kernel, tm=tm, tiles_per_seq=tiles_per_seq)
    return pl.pallas_call(
        kern,
        grid=(N_GROUPS // GROUPS_PER_STEP, n_tiles),
        in_specs=[
            pl.BlockSpec((tm, D_MODEL), lambda j, i: (i, 0)),
            pl.BlockSpec((8, D_MODEL), lambda j, i: (jnp.maximum(i * (tm // 8) - 1, 0), 0)),
            pl.BlockSpec((8, D_MODEL), lambda j, i: (jnp.minimum((i + 1) * (tm // 8), last8), 0)),
            pl.BlockSpec((1, D_MODEL), lambda j, i: (0, 0)),
            pl.BlockSpec((D_MODEL, GROUPS_PER_STEP * GROUP), lambda j, i: (0, j)),
            pl.BlockSpec((1, GROUPS_PER_STEP * GROUP), lambda j, i: (0, j)),
            pl.BlockSpec((tm, 128), lambda j, i: (i % tiles_per_seq, 0)),
            pl.BlockSpec((tm, 128), lambda j, i: (i % tiles_per_seq, 0)),
            pl.BlockSpec((tm, 128), lambda j, i: (i % tiles_per_seq, 0)),
        ],
        out_specs=pl.BlockSpec((tm, GROUPS_PER_STEP * GROUP), lambda j, i: (i, j)),
        out_shape=jax.ShapeDtypeStruct((t_total, N_GROUPS * GROUP), F32),
        scratch_shapes=[pltpu.VMEM((tm, D_MODEL), BF16), pltpu.VMEM((16, D_MODEL), F32)],
        compiler_params=pltpu.CompilerParams(
            dimension_semantics=("arbitrary", "arbitrary"), vmem_limit_bytes=VMEM_LIMIT),
        name="inproj",
    )(x2d, x2d, x2d, g, w_re, mu_re, cos_t, sa_t, sb_t)


QUAD = 256
N_QUAD = D_MODEL // QUAD
HEADS_PER_QUAD = QUAD // HEAD
SCAN_SUB = 8


def _scan_kernel(r_ref, k_ref, v_ref, lw_ref, la_ref, wu_ref, au_ref,
                 w0_ref, a0_ref, kk_ref, ka_ref, rk_ref, seg_ref, y_ref, s_ref, h_ref, *, n_sub):
    d = pl.program_id(1)
    c = pl.program_id(2)
    C = CHUNK
    bf = lambda x: x.astype(BF16)

    @pl.when(c == 0)
    def _():
        h_ref[...] = jnp.zeros_like(h_ref)

    sgn = 1 - 2 * d
    rr = lax.broadcasted_iota(jnp.int32, (C, C), 0)
    cc = lax.broadcasted_iota(jnp.int32, (C, C), 1)
    tri = jnp.where((rr - cc) * sgn >= 0, 1.0, 0.0).astype(BF16)

    tr = lax.broadcasted_iota(jnp.int32, (C, QUAD), 0)
    tc = lax.broadcasted_iota(jnp.int32, (C, QUAD), 1) % C
    dd = (tr - tc) * sgn
    strict = dd > 0
    incl = dd >= 0
    eye = jnp.where(dd == 0, 1.0, 0.0)
    levels = []
    m = 1
    while m < C:
        levels.append(jnp.logical_and((tr // (2 * m)) == (tc // (2 * m)), (tr // m) != (tc // m)))
        m *= 2
    lane_head = lax.broadcasted_iota(jnp.int32, (1, QUAD), 1) // HEAD
    head_lanes = [lane_head == h for h in range(HEADS_PER_QUAD)]
    same_head = (lax.broadcasted_iota(jnp.int32, (QUAD, QUAD), 0) // HEAD) == (
        lax.broadcasted_iota(jnp.int32, (QUAD, QUAD), 1) // HEAD)
    bd_ones = jnp.where(same_head, 1.0, 0.0).astype(BF16)

    def blockdiag(x):
        return jnp.concatenate([jnp.where(hl, x, 0.0) for hl in head_lanes], axis=0)

    quads = range(N_QUAD)
    sls = [slice(QUAD * q, QUAD * (q + 1)) for q in quads]

    staged = []
    for i in range(n_sub):
        jj = i + d * (n_sub - 1 - 2 * i)
        rows = pl.ds(pl.multiple_of(jj * C, C), C)
        r = r_ref[rows, :]
        k = k_ref[rows, :]
        v = v_ref[rows, :]

        w_raw = w0_ref[0] + _dot(bf(jnp.tanh(lw_ref[rows, :])), wu_ref[0])
        logw = (-EXP_M05 * LOG2_E) * _sigmoid(w_raw)
        a = _sigmoid(a0_ref[0] + _dot(bf(la_ref[rows, :]), au_ref[0]))

        l1, l2 = _split(logw)
        cum = _dot(tri, l1) + _dot(tri, l2)
        total = jnp.sum(logw, axis=0, keepdims=True)
        g_incl = jnp.exp2(cum)
        g_excl = jnp.exp2(cum - logw)
        g_inv = jnp.exp2(-cum)
        g_tail = jnp.exp2(total - cum)
        g_tot = jnp.exp2(total)

        kkv = k * kk_ref[...]
        kd = k * (1.0 + (a - 1.0) * ka_ref[...])
        sq = kkv * kkv

        s_ref[0, rows, :] = _dot(bf(r * kd * rk_ref[...]), seg_ref[...])

        ss = [_dot(bf(sq[:, sl]), bd_ones) for sl in sls]
        kkn = [kkv[:, sl] * lax.rsqrt(jnp.maximum(s, 1e-24)) for sl, s in zip(sls, ss)]
        bvec = [kn * a[:, sl] for sl, kn in zip(sls, kkn)]
        lhs = [bf(jnp.concatenate([-kn * g_excl[:, sl], r[:, sl] * g_incl[:, sl]], axis=0))
               for sl, kn in zip(sls, kkn)]
        rhs = [jnp.concatenate([blockdiag(bf(bv * g_inv[:, sl])), blockdiag(bf(kd[:, sl] * g_inv[:, sl]))], axis=0)
               for sl, bv in zip(sls, bvec)]
        bkh = [bf(jnp.concatenate([bv * g_tail[:, sl], kd[:, sl] * g_tail[:, sl]], axis=0))
               for sl, bv in zip(sls, bvec)]
        vb = [bf(v[:, sl]) for sl in sls]
        v_bd = [blockdiag(x) for x in vb]

        gm = [_dot(lhs[q], rhs[q], NT) for q in quads]
        gb = [bf(g) for g in gm]
        a_ab = [jnp.where(strict, g[0:C, 0:QUAD], 0.0) for g in gm]
        a_abb = [jnp.where(strict, g[0:C, 0:QUAD], 0.0) for g in gb]
        a_ak = [jnp.where(strict, g[0:C, QUAD:2 * QUAD], 0.0) for g in gb]
        a_r = [jnp.concatenate([jnp.where(incl, g[C:2 * C, 0:QUAD], 0.0),
                                jnp.where(incl, g[C:2 * C, QUAD:2 * QUAD], 0.0)], axis=1) for g in gb]
        akv = [_dot(a_ak[q], v_bd[q]) for q in quads]
        staged.append(dict(rows=rows, lhs=lhs, a_ab=a_ab, a_abb=a_abb, a_r=a_r, akv=akv, vb=vb, v_bd=v_bd,
                           bkh=bkh, g_tot=g_tot))

    combos = [(i, q) for i in range(n_sub) for q in quads]
    tinv = {(i, q): eye + jnp.where(levels[0], staged[i]["a_ab"][q], 0.0) for i, q in combos}
    for lvl in levels[1:]:
        tb = {key: bf(t) for key, t in tinv.items()}
        wm = {(i, q): _dot(jnp.where(lvl, staged[i]["a_abb"][q], 0.0), blockdiag(tb[i, q])) for i, q in combos}
        tinv = {key: tinv[key] + _dot(tb[key], blockdiag(bf(wm[key]))) for key in combos}

    ht = [h_ref[q] for q in quads]
    for i in range(n_sub):
        st = staged[i]
        ar = [_dot(st["lhs"][q], bf(ht[q]), NT) for q in quads]
        rhs_u = [ar[q][0:C] + st["akv"][q] for q in quads]
        u = [bf(_dot(bf(tinv[i, q]), blockdiag(bf(rhs_u[q])))) for q in quads]
        uv_bd = [jnp.concatenate([blockdiag(u[q]), st["v_bd"][q]], axis=0) for q in quads]
        for q in quads:
            y_ref[0, st["rows"], sls[q]] = (ar[q][C:2 * C] + _dot(st["a_r"][q], uv_bd[q])).astype(y_ref.dtype)
        upd = [_dot(jnp.concatenate([u[q], st["vb"][q]], axis=0), st["bkh"][q], TN) for q in quads]
        ht = [ht[q] * st["g_tot"][:, sls[q]] + jnp.where(same_head, upd[q], 0.0) for q in quads]

    for q in quads:
        h_ref[q] = ht[q]


def _wkv_scan(proj, batch, seq, wu, au, w0, a0, k_k, k_a, r_k, seg):
    t_total = batch * seq
    n_sub = math.gcd(seq // CHUNK, SCAN_SUB)
    blk = n_sub * CHUNK
    nc = seq // blk

    def rowblk(b, d, c):
        return b * nc + c + d * (nc - 1 - 2 * c)

    def col(jblk):
        return lambda b, d, c: (rowblk(b, d, c), jblk)

    full = lambda b, d, c: (0, 0)
    perdir = lambda b, d, c: (d, 0, 0)
    return pl.pallas_call(
        functools.partial(_scan_kernel, n_sub=n_sub),
        grid=(batch, 2, nc),
        in_specs=[
            pl.BlockSpec((blk, GROUP), col(0)),
            pl.BlockSpec((blk, GROUP), col(1)),
            pl.BlockSpec((blk, GROUP), col(2)),
            pl.BlockSpec((blk, 128), col(LORA_BLK)),
            pl.BlockSpec((blk, 128), col(LORA_BLK + 1)),
            pl.BlockSpec((1, 128, D_MODEL), perdir),
            pl.BlockSpec((1, 128, D_MODEL), perdir),
            pl.BlockSpec((1, 1, D_MODEL), perdir),
            pl.BlockSpec((1, 1, D_MODEL), perdir),
            pl.BlockSpec((1, D_MODEL), full),
            pl.BlockSpec((1, D_MODEL), full),
            pl.BlockSpec((1, D_MODEL), full),
            pl.BlockSpec((D_MODEL, 128), full),
        ],
        out_specs=[
            pl.BlockSpec((1, blk, D_MODEL), lambda b, d, c: (d, rowblk(b, d, c), 0)),
            pl.BlockSpec((1, blk, 128), lambda b, d, c: (d, rowblk(b, d, c), 0)),
        ],
        out_shape=[
            jax.ShapeDtypeStruct((2, t_total, D_MODEL), BF16),
            jax.ShapeDtypeStruct((2, t_total, 128), F32),
        ],
        scratch_shapes=[pltpu.VMEM((N_QUAD, QUAD, QUAD), F32)],
        compiler_params=pltpu.CompilerParams(
            dimension_semantics=("arbitrary", "arbitrary", "arbitrary"), vmem_limit_bytes=VMEM_LIMIT),
        name="wkv_scan",
    )(proj, proj, proj, proj, proj, wu, au, w0, a0, k_k, k_a, r_k, seg)


V_ROWS = 128 + 16
PV_GROUP = 2


def _attn_kernel(q_ref, k_ref, v_ref, lq1_ref, lk1_ref, lq2_ref, lk2_ref, sw_ref, o_ref,
                 k0_ref, k1_ref, vt_ref, s_ref, m_ref, p_ref, *, seq, tq, tk, per_trip, nq, lam_init):
    i = pl.program_id(2)
    n_kc = seq // tk
    lane = lax.broadcasted_iota(jnp.int32, (1, 128), 1)
    first = lane < HEAD

    def prepare(rows):
        kf = k_ref[rows, :]
        k0_ref[rows, :] = jnp.where(first, kf, 0.0).astype(BF16)
        k1_ref[rows, :] = jnp.where(first, 0.0, kf).astype(BF16)
        vt_ref[0:128, rows] = v_ref[rows, :].T.astype(BF16)
        vt_ref[128:V_ROWS, rows] = jnp.ones((V_ROWS - 128, tk), BF16)

    def rows_of(kc):
        return pl.ds(pl.multiple_of(kc * tk, tk), tk)

    def fold(op, s):
        return op(s.reshape(tk // 8, 8, tq), axis=0)

    def query_t():
        return q_ref[...].T.astype(BF16)

    neg = jnp.full((8, tq), -jnp.inf, F32)
    za = jnp.zeros((V_ROWS, tq), F32)
    n_trips = n_kc // per_trip

    def finish(acc0, acc1):
        l0 = acc0[128:129, :]
        l1 = acc1[128:129, :]
        lam = (jnp.exp(jnp.sum(lq1_ref[...] * lk1_ref[...], axis=-1, keepdims=True))
               - jnp.exp(jnp.sum(lq2_ref[...] * lk2_ref[...], axis=-1, keepdims=True)) + lam_init)
        o_t = acc0[0:128, :] * (1.0 / l0) - acc1[0:128, :] * (lam / l1)
        o_t = o_t * lax.rsqrt(jnp.mean(o_t * o_t, axis=0, keepdims=True) + SUBLN_EPS)
        o = o_t.T * (sw_ref[...] * (1.0 - lam_init))
        o_ref[...] = o.astype(o_ref.dtype)

    def run(slot_a):
        slot_b = 1 - slot_a
        mine = (i % 2) == slot_a

        def trip(t, q_t, mx, m_prev, acc, first_tile=False):
            for c in range(per_trip):
                rows = rows_of(t * per_trip + c)
                if first_tile:
                    prepare(rows)
                if q_t is not None:
                    s0 = _dot(k0_ref[rows, :], q_t)
                    s1 = _dot(k1_ref[rows, :], q_t)
                    s_ref[slot_a, 0, rows, :] = s0
                    s_ref[slot_a, 1, rows, :] = s1
                    mx = (jnp.maximum(mx[0], fold(jnp.max, s0)), jnp.maximum(mx[1], fold(jnp.max, s1)))
                if acc is not None:
                    for comp in range(2):
                        x = s_ref[slot_b, comp, rows, :] - m_prev[comp]
                        p_ref[comp, c * tk:(c + 1) * tk, :] = jnp.exp2(x.astype(BF16))
                    if (c + 1) % PV_GROUP == 0 or c + 1 == per_trip:
                        c0 = (c // PV_GROUP) * PV_GROUP
                        span = (c + 1 - c0) * tk
                        start = (t * per_trip + c0) * tk
                        vt = vt_ref[:, pl.ds(pl.multiple_of(start, tk), span)]
                        acc = tuple(acc[comp] + _dot(vt, p_ref[comp, c0 * tk:(c + 1) * tk, :])
                                    for comp in range(2))
            return mx, acc

        def loop(body, init):
            if n_trips == 1:
                return body(0, init)
            return lax.fori_loop(0, n_trips, body, init)

        def prev_max():
            return (jnp.max(m_ref[slot_b, 0], axis=0, keepdims=True),
                    jnp.max(m_ref[slot_b, 1], axis=0, keepdims=True))

        def store_max(mx):
            m_ref[slot_a, 0] = mx[0]
            m_ref[slot_a, 1] = mx[1]

        if slot_a == 0:
            @pl.when(i == 0)
            def _():
                q_t = query_t()
                store_max(loop(lambda t, mx: trip(t, q_t, mx, None, None, first_tile=True)[0], (neg, neg)))

        @pl.when(jnp.logical_and(mine, jnp.logical_and(i > 0, i < nq)))
        def _():
            q_t = query_t()
            m_prev = prev_max()
            mx, acc = loop(lambda t, c: trip(t, q_t, c[0], m_prev, c[1]), ((neg, neg), (za, za)))
            store_max(mx)
            finish(*acc)

        if nq % 2 == slot_a:
            @pl.when(i == nq)
            def _():
                m_prev = prev_max()
                finish(*loop(lambda t, acc: trip(t, None, None, m_prev, acc)[1], (za, za)))

    run(0)
    run(1)


def _diff_attn(proj, batch, seq, lq1, lk1, lq2, lk2, subln_w, lam_init):
    t_total = batch * seq
    tq = min(TQ_ATTN, seq)
    tk = min(TK_ATTN, seq)
    per_trip = math.gcd(seq // tk, 16)
    nq = seq // tq
    n_heads = N_PAIR
    qb, kb, vb = (G_Q * GROUP // 128, G_KB * GROUP // 128, G_VB * GROUP // 128)
    kern = functools.partial(_attn_kernel, seq=seq, tq=tq, tk=tk, per_trip=per_trip, nq=nq, lam_init=lam_init)
    small = lambda b, h, i: (0, 0)
    return pl.pallas_call(
        kern,
        grid=(batch, n_heads, nq + 1),
        in_specs=[
            pl.BlockSpec((tq, 128), lambda b, h, i: (b * nq + jnp.minimum(i, nq - 1), qb + h)),
            pl.BlockSpec((seq, 128), lambda b, h, i: (b, kb + h)),
            pl.BlockSpec((seq, 128), lambda b, h, i: (b, vb + h)),
            pl.BlockSpec((1, HEAD), small),
            pl.BlockSpec((1, HEAD), small),
            pl.BlockSpec((1, HEAD), small),
            pl.BlockSpec((1, HEAD), small),
            pl.BlockSpec((1, 128), small),
        ],
        out_specs=pl.BlockSpec((tq, 128), lambda b, h, i: (b * nq + jnp.maximum(i - 1, 0), h)),
        out_shape=jax.ShapeDtypeStruct((t_total, D_MODEL), BF16),
        scratch_shapes=[pltpu.VMEM((seq, 128), BF16), pltpu.VMEM((seq, 128), BF16),
                        pltpu.VMEM((V_ROWS, seq), BF16), pltpu.VMEM((2, 2, seq, tq), F32),
                        pltpu.VMEM((2, 2, 8, tq), F32), pltpu.VMEM((2, per_trip * tk, tq), BF16)],
        compiler_params=pltpu.CompilerParams(
            dimension_semantics=("arbitrary", "arbitrary", "arbitrary"), vmem_limit_bytes=VMEM_LIMIT),
        name="diff_attn",
    )(proj, proj, proj, lq1, lk1, lq2, lk2, subln_w)


def _mix_kernel(x_ref, yf_ref, yb_ref, sf_ref, sb_ref, v_ref, lg_ref, ga_ref, gb_ref, o_ref,
                lnw_ref, lnb_ref, gup_ref, exp_ref, pa_ref, pb_ref, wo_ref, h_ref, ya_ref):
    y = yf_ref[0].astype(F32) + yb_ref[0].astype(F32)
    rh = lax.broadcasted_iota(jnp.int32, (QUAD, QUAD), 0) // HEAD
    ch = lax.broadcasted_iota(jnp.int32, (QUAD, QUAD), 1) // HEAD
    avg = jnp.where(rh == ch, 1.0 / HEAD, 0.0).astype(BF16)
    for q in range(N_QUAD):
        sl = slice(QUAD * q, QUAD * (q + 1))
        yq = y[:, sl]
        cen = yq - _dot(yq.astype(BF16), avg)
        var = _dot((cen * cen).astype(BF16), avg)
        ya_ref[:, sl] = cen * lax.rsqrt(var + GN_EPS)
    coef = _dot((sf_ref[0] + sb_ref[0]).astype(BF16), exp_ref[...])
    gate = _dot(_sigmoid(lg_ref[...]).astype(BF16), gup_ref[...])
    y_a = (ya_ref[...] * lnw_ref[...] + lnb_ref[...] + coef * v_ref[...]) * gate
    pa = _dot(y_a.astype(BF16), pa_ref[...])
    pb = _dot(o_ref[...], pb_ref[...])
    merged = _sigmoid(ga_ref[...]) * pa + _sigmoid(gb_ref[...]) * pb
    h_ref[...] = x_ref[...] + _dot(merged.astype(BF16), wo_ref[...])


def _mix(x2d, proj, y_dir, s_dir, o_attn, ln_w, ln_b, g_up, expand, proj_a, proj_b, w_out):
    t_total = x2d.shape[0]
    tm = TM_MIX
    row = lambda i: (i, 0)
    full = lambda i: (0, 0)
    wspec = pl.BlockSpec((D_MODEL, D_MODEL), full)
    return pl.pallas_call(
        _mix_kernel,
        grid=(t_total // tm,),
        in_specs=[
            pl.BlockSpec((tm, D_MODEL), row),
            pl.BlockSpec((1, tm, D_MODEL), lambda i: (0, i, 0)),
            pl.BlockSpec((1, tm, D_MODEL), lambda i: (1, i, 0)),
            pl.BlockSpec((1, tm, 128), lambda i: (0, i, 0)),
            pl.BlockSpec((1, tm, 128), lambda i: (1, i, 0)),
            pl.BlockSpec((tm, GROUP), lambda i: (i, 2)),
            pl.BlockSpec((tm, 128), lambda i: (i, LORA_BLK + 2)),
            pl.BlockSpec((tm, GROUP), lambda i: (i, G_GA)),
            pl.BlockSpec((tm, GROUP), lambda i: (i, G_GB)),
            pl.BlockSpec((tm, D_MODEL), row),
            pl.BlockSpec((1, D_MODEL), full),
            pl.BlockSpec((1, D_MODEL), full),
            pl.BlockSpec((GATE_RANK, D_MODEL), full),
            pl.BlockSpec((128, D_MODEL), full),
            wspec, wspec, wspec,
        ],
        out_specs=pl.BlockSpec((tm, D_MODEL), row),
        out_shape=jax.ShapeDtypeStruct((t_total, D_MODEL), F32),
        scratch_shapes=[pltpu.VMEM((tm, D_MODEL), F32)],
        compiler_params=pltpu.CompilerParams(
            dimension_semantics=("arbitrary",), vmem_limit_bytes=VMEM_LIMIT),
        name="mix",
    )(x2d, y_dir, y_dir, s_dir, s_dir, proj, proj, proj, proj, o_attn,
      ln_w, ln_b, g_up, expand, proj_a, proj_b, w_out)


def _mlp_kernel(h_ref, gm_ref, gf_ref, w1_ref, w2_ref, o_ref, *, ff_chunk):
    h = h_ref[...]
    xn = _rms(h, gm_ref[...]).astype(BF16)
    acc = h
    for c in range(D_FF // ff_chunk):
        sl = slice(c * ff_chunk, (c + 1) * ff_chunk)
        hid = jnp.maximum(_dot(xn, w1_ref[:, sl]), 0.0)
        acc = acc + _dot((hid * hid).astype(BF16), w2_ref[sl, :])
    o_ref[...] = _rms(acc, gf_ref[...])


def _mlp(h2d, norm_mlp, norm_final, w1, w2):
    t_total = h2d.shape[0]
    tm = min(TM_MLP, t_total)
    row = lambda i: (i, 0)
    full = lambda i: (0, 0)
    kern = functools.partial(_mlp_kernel, ff_chunk=FF_CHUNK)
    return pl.pallas_call(
        kern,
        grid=(t_total // tm,),
        in_specs=[
            pl.BlockSpec((tm, D_MODEL), row),
            pl.BlockSpec((1, D_MODEL), full),
            pl.BlockSpec((1, D_MODEL), full),
            pl.BlockSpec((D_MODEL, D_FF), full, pipeline_mode=pl.Buffered(1)),
            pl.BlockSpec((D_FF, D_MODEL), full, pipeline_mode=pl.Buffered(1)),
        ],
        out_specs=pl.BlockSpec((tm, D_MODEL), row),
        out_shape=jax.ShapeDtypeStruct((t_total, D_MODEL), F32),
        compiler_params=pltpu.CompilerParams(
            dimension_semantics=("arbitrary",), vmem_limit_bytes=VMEM_LIMIT),
        name="mlp",
    )(h2d, norm_mlp, norm_final, w1, w2)


def _rope_tables(seq):
    half = ROPE_DIM // 2
    pos = jnp.arange(seq, dtype=F32)
    inv_freq = ROPE_THETA ** (-jnp.arange(0, ROPE_DIM, 2, dtype=F32) / ROPE_DIM)
    ang = pos[:, None] * inv_freq[None, :]
    cos, sin = jnp.cos(ang), jnp.sin(ang)
    pad = jnp.zeros((seq, HEAD - ROPE_DIM), F32)
    z8 = jnp.zeros((seq, half), F32)
    c64 = jnp.concatenate([cos, cos, pad + 1.0], axis=1)
    sa64 = jnp.concatenate([z8, sin, pad], axis=1)
    sb64 = jnp.concatenate([-sin, z8, pad], axis=1)
    two = lambda t: jnp.concatenate([t, t], axis=1)
    return two(c64), two(sa64), two(sb64)


def _layer(x, l, prm, norm_final):
    batch, seq, _ = x.shape
    x2d = x.reshape(batch * seq, D_MODEL)
    cos_t, sa_t, sb_t = _rope_tables(seq)
    proj = _inproj(x2d, seq, prm["norm_mix"], prm["w_re"], prm["mu_re"], cos_t, sa_t, sb_t)
    y_dir, s_dir = _wkv_scan(proj, batch, seq, prm["wu"], prm["au"],
                             prm["w0"], prm["a0"], prm["k_k"], prm["k_a"], prm["r_k"], prm["seg"])
    lam_init = 0.8 - 0.6 * math.exp(-0.3 * l)
    o_attn = _diff_attn(proj, batch, seq, prm["lq1"], prm["lk1"], prm["lq2"], prm["lk2"],
                        prm["subln_w"], lam_init)
    h = _mix(x2d, proj, y_dir, s_dir, o_attn, prm["ln_w"], prm["ln_b"], prm["g_up"], prm["expand"],
             prm["proj_a"], prm["proj_b"], prm["w_out"])
    return h


def _prep_layer(l, w_in, mu_shift, w0, w_lora_up, a0, a_lora_up, g_lora_up, k_k, k_a, r_k, ln_x_w,
                ln_x_b, lambda_q1, lambda_k1, lambda_q2, lambda_k2, subln_w, proj_a, proj_b, w_out,
                norm_mix, norm_mlp, w_mlp_in, w_mlp_out):
    w = w_in[l]
    pad_cols = N_GROUPS * GROUP - w.shape[1]
    w_re = jnp.concatenate([w[:, :RWKV_MAIN], w[:, RWKV_COLS:], w[:, RWKV_MAIN:RWKV_COLS],
                            jnp.zeros((D_MODEL, pad_cols), F32)], axis=1).astype(BF16)
    mu = mu_shift[l]
    mu_re = jnp.concatenate([mu[:RWKV_MAIN], jnp.zeros((5 * GROUP,), F32), mu[RWKV_MAIN:],
                             jnp.zeros((pad_cols,), F32)])[None, :]

    def lora_pad(up):
        z = jnp.zeros_like(up[0])
        return jnp.stack([jnp.concatenate([up[0], z], axis=0), jnp.concatenate([z, up[1]], axis=0)])

    wu = lora_pad(w_lora_up[l])
    au = lora_pad(a_lora_up[l])
    head_of_lane = jnp.arange(D_MODEL) // HEAD
    seg = (head_of_lane[:, None] == jnp.arange(128)[None, :]).astype(BF16)
    return dict(
        w_re=w_re, mu_re=mu_re, norm_mix=norm_mix[l][None, :],
        wu=wu.astype(BF16), au=au.astype(BF16),
        w0=w0[l][:, None, :], a0=a0[l][:, None, :],
        k_k=k_k[l][None, :], k_a=k_a[l][None, :], r_k=r_k[l].reshape(1, D_MODEL),
        seg=seg, expand=seg.T,
        lq1=lambda_q1[l][None, :], lk1=lambda_k1[l][None, :],
        lq2=lambda_q2[l][None, :], lk2=lambda_k2[l][None, :],
        subln_w=subln_w[l][None, :],
        ln_w=ln_x_w[l][None, :], ln_b=ln_x_b[l][None, :],
        g_up=g_lora_up[l].astype(BF16),
        proj_a=proj_a[l].astype(BF16), proj_b=proj_b[l].astype(BF16), w_out=w_out[l].astype(BF16),
        norm_mlp=norm_mlp[l][None, :],
        w1=w_mlp_in[l].astype(BF16), w2=w_mlp_out[l].astype(BF16),
    )


def kernel(x_prompt, x_sample, w_in, mu_shift, w0, w_lora_up, a0, a_lora_up, g_lora_up, k_k, k_a, r_k, ln_x_w, ln_x_b, lambda_q1, lambda_k1, lambda_q2, lambda_k2, subln_w, proj_a, proj_b, w_out, norm_mix, norm_mlp, w_mlp_in, w_mlp_out, norm_final):
    depth = w_in.shape[0]
    assert depth == 1, "the final norm is fused into the (single) layer's MLP kernel"
    prm = _prep_layer(0, w_in, mu_shift, w0, w_lora_up, a0, a_lora_up, g_lora_up, k_k, k_a, r_k, ln_x_w,
                      ln_x_b, lambda_q1, lambda_k1, lambda_q2, lambda_k2, subln_w, proj_a, proj_b,
                      w_out, norm_mix, norm_mlp, w_mlp_in, w_mlp_out)
    outs = []
    for x in (x_prompt, x_sample):
        h = _layer(x, 0, prm, norm_final)
        y = _mlp(h, prm["norm_mlp"], norm_final[None, :], prm["w1"], prm["w2"])
        outs.append(y.reshape(x.shape))
    return tuple(outs)
```
